```python
import math
import jax
import jax.numpy as jnp
from jax import lax
import numpy as np

D_MODEL = 2048
BATCH = 4
SEQ = 2048
DEPTH = 2
DEC_BATCH = 8
DEC_SEQ = 1
PAST_LEN = 16384
PAGE_SIZE = 128

HEAD_DIM = 128
MIX_WIDTH = D_MODEL
NSA_HEADS = MIX_WIDTH // (2 * HEAD_DIM)
NSA_KV = 2
NSA_HPG = NSA_HEADS // NSA_KV
GDN_HEADS = MIX_WIDTH // (2 * HEAD_DIM)
GDN_DK = HEAD_DIM
GDN_DV = HEAD_DIM
ROT_DIM = HEAD_DIM // 4
ROPE_THETA = 500000.0
BLOCK = 64
TOPN = 16
WINDOW = 512
CMP_HID = 256
NSA_QBLOCK = 64
CONV_W = 4
GDN_CHUNK = 64
D_FF = -(-8 * D_MODEL // (3 * 256)) * 256
EPS = 1e-6
NSA_Q_W = NSA_HEADS * HEAD_DIM
KV_W = NSA_KV * HEAD_DIM
GATE_W = 3 * NSA_HEADS
GDN_QK_W = GDN_HEADS * GDN_DK
GDN_V_W = GDN_HEADS * GDN_DV
CONV_CH = 2 * GDN_QK_W + GDN_V_W
IN_SPLITS = (NSA_Q_W, KV_W, KV_W, KV_W, KV_W, KV_W, KV_W, GATE_W, CONV_CH, GDN_HEADS, GDN_HEADS, GDN_V_W)
IN_W = sum(IN_SPLITS)

kernel_name = 'hymba_nsa_gdn_decode_step'


def rmsnorm(x, g):
    xf = x.astype(jnp.float32)
    y = xf * lax.rsqrt(jnp.mean(xf * xf, axis=-1, keepdims=True) + EPS) * g.astype(jnp.float32)
    return y.astype(x.dtype)


def l2norm(x):
    xf = x.astype(jnp.float32)
    return xf * lax.rsqrt(jnp.sum(xf * xf, axis=-1, keepdims=True) + EPS)


def rope(x, pos):
    half = ROT_DIM // 2
    inv = ROPE_THETA ** (-jnp.arange(half, dtype=jnp.float32) * 2.0 / ROT_DIM)
    ang = pos.astype(jnp.float32)[:, None] * inv
    cos, sin = jnp.cos(ang)[:, None, :], jnp.sin(ang)[:, None, :]
    xf = x.astype(jnp.float32)
    x1, x2, rest = xf[..., :half], xf[..., half:ROT_DIM], xf[..., ROT_DIM:]
    return jnp.concatenate([x1 * cos - x2 * sin, x2 * cos + x1 * sin, rest], -1).astype(x.dtype)


def masked_softmax(s, mask):
    s = jnp.where(mask, s.astype(jnp.float32), -jnp.inf)
    m = jnp.max(s, axis=-1, keepdims=True)
    m = jnp.where(jnp.isfinite(m), m, 0.0)
    e = jnp.where(mask, jnp.exp(s - m), 0.0)
    return e / jnp.maximum(jnp.sum(e, axis=-1, keepdims=True), 1e-30)


def kv_rows(k, v):
    return jnp.stack([k, v], axis=3).transpose(0, 2, 1, 3, 4)


def compress_blocks(rows, pe, w1, w2):
    B, L, G, D = rows.shape
    nb = L // BLOCK
    blk = rows.reshape(B, nb, BLOCK, G, D) + pe[None, None, :, None, :].astype(rows.dtype)
    blk = blk.transpose(0, 1, 3, 2, 4).reshape(B, nb, G, BLOCK * D)
    return jax.nn.gelu(blk @ w1) @ w2


def nsa_core(q, q_pos, gates, kc, vc, sel_gather, n_sel, kw, vw, kw_pos):
    B, Tq = q.shape[:2]
    scale = HEAD_DIM ** -0.5
    qf = q.reshape(B, Tq, NSA_KV, NSA_HPG, HEAD_DIM)
    nbc = kc.shape[1]
    j = jnp.arange(nbc)
    cmask = (j * BLOCK + (BLOCK - 1))[None, :] <= q_pos[:, None]
    s_c = jnp.einsum('btghd,bjgd->btghj', qf, kc).astype(jnp.float32) * scale
    p_c = masked_softmax(s_c, cmask[None, :, None, None, :])
    o_c = jnp.einsum('btghj,bjgd->btghd', p_c, vc)
    imp = jnp.sum(p_c, axis=3)
    cur = q_pos // BLOCK
    forced = (j[None] == 0) | (j[None] == cur[:, None]) | (j[None] == cur[:, None] - 1)
    future = j[None] > cur[:, None]
    score = jnp.where(forced[None, :, None, :], jnp.inf, jnp.where(future[None, :, None, :], -jnp.inf, imp))
    _, idx = lax.top_k(score, n_sel)
    ks, vs = sel_gather(idx)
    kpos = idx[..., None] * BLOCK + jnp.arange(BLOCK)
    smask = (kpos <= q_pos[None, :, None, None, None]).reshape(B, Tq, NSA_KV, 1, n_sel * BLOCK)
    s_s = jnp.einsum('btghd,btgnrd->btghnr', qf, ks).astype(jnp.float32) * scale
    p_s = masked_softmax(s_s.reshape(B, Tq, NSA_KV, NSA_HPG, n_sel * BLOCK), smask)
    o_s = jnp.einsum('btghm,btgmd->btghd', p_s, vs.reshape(B, Tq, NSA_KV, n_sel * BLOCK, HEAD_DIM))
    wmask = (kw_pos[None, :] <= q_pos[:, None]) & (kw_pos[None, :] >= q_pos[:, None] - WINDOW) & (kw_pos[None, :] >= 0)
    s_w = jnp.einsum('btghd,blgd->btghl', qf, kw).astype(jnp.float32) * scale
    p_w = masked_softmax(s_w, wmask[None, :, None, None, :])
    o_w = jnp.einsum('btghl,blgd->btghd', p_w, vw)
    g = gates.reshape(B, Tq, NSA_KV, NSA_HPG, 3)
    o = g[..., 0:1] * o_c + g[..., 1:2] * o_s + g[..., 2:3] * o_w
    return o.reshape(B, Tq, NSA_HEADS * HEAD_DIM).astype(q.dtype)


def gated_delta_chunked(q, k, v, g, beta, s0):
    B, T, H, _ = q.shape
    DV = v.shape[-1]
    C = GDN_CHUNK
    n = -(-T // C)
    pad = n * C - T

    def chunks(t):
        t = jnp.pad(t, ((0, 0), (0, pad)) + ((0, 0),) * (t.ndim - 2))
        t = t.reshape((B, n, C) + t.shape[2:])
        return jnp.moveaxis(t, 3, 1)

    q, k, v, g, beta = (chunks(t) for t in (q, k, v, g, beta))
    gc = jnp.cumsum(g, axis=-1)
    tri = jnp.tril(jnp.ones((C, C), bool))
    strict = jnp.tril(jnp.ones((C, C), bool), -1)
    decay = jnp.exp(jnp.where(tri, gc[..., :, None] - gc[..., None, :], -jnp.inf))
    kb = k * beta[..., None]
    L = jnp.where(strict, jnp.einsum('bhnid,bhnjd->bhnij', kb, k) * decay, 0.0)
    rhs = jnp.concatenate([v * beta[..., None], kb * jnp.exp(gc)[..., None]], axis=-1)
    sol = lax.linalg.triangular_solve(L + jnp.eye(C, dtype=L.dtype), rhs, left_side=True, lower=True, unit_diagonal=True)
    u, w = sol[..., :DV], sol[..., DV:]
    qk = jnp.where(tri, jnp.einsum('bhnid,bhnjd->bhnij', q, k) * decay, 0.0)

    def step(S, xs):
        q_c, k_c, u_c, w_c, g_c, qk_c = xs
        v_new = u_c - jnp.einsum('bhcd,bhde->bhce', w_c, S)
        o = jnp.einsum('bhcd,bhde->bhce', q_c * jnp.exp(g_c)[..., None], S) + jnp.einsum('bhij,bhje->bhie', qk_c, v_new)
        g_last = g_c[..., -1:]
        S = S * jnp.exp(g_last)[..., None] + jnp.einsum('bhcd,bhce->bhde', k_c * jnp.exp(g_last - g_c)[..., None], v_new)
        return S, o

    xs = tuple(jnp.moveaxis(t, 2, 0) for t in (q, k, u, w, gc, qk))
    S, o = lax.scan(step, s0, xs)
    o = o.transpose(1, 0, 3, 2, 4).reshape(B, n * C, H, DV)[:, :T]
    return o, S


def gdn_mixer(conv_in, a, b, z, conv_buf, s0, conv_w, a_log, dt_bias, norm_w):
    B, T, _ = conv_in.shape
    xc = jnp.concatenate([conv_buf.astype(conv_in.dtype), conv_in], axis=1)
    y = lax.conv_general_dilated(xc, conv_w.astype(xc.dtype)[:, None, :], window_strides=(1,), padding='VALID',
                                 dimension_numbers=('NWC', 'WIO', 'NWC'), feature_group_count=CONV_CH)
    y = jax.nn.silu(y)
    q, k, v = jnp.split(y, [GDN_QK_W, 2 * GDN_QK_W], axis=-1)
    q = l2norm(q.reshape(B, T, GDN_HEADS, GDN_DK)) * (GDN_DK ** -0.5)
    k = l2norm(k.reshape(B, T, GDN_HEADS, GDN_DK))
    v = v.reshape(B, T, GDN_HEADS, GDN_DV).astype(jnp.float32)
    beta = jax.nn.sigmoid(b.astype(jnp.float32))
    g = -jnp.exp(a_log.astype(jnp.float32)) * jax.nn.softplus(a.astype(jnp.float32) + dt_bias.astype(jnp.float32))
    o, S = gated_delta_chunked(q, k, v, g, beta, s0.astype(jnp.float32))
    o = rmsnorm(o, norm_w) * jax.nn.silu(z.reshape(B, T, GDN_HEADS, GDN_DV).astype(jnp.float32))
    return o.reshape(B, T, GDN_V_W).astype(conv_in.dtype), S.astype(conv_in.dtype), xc[:, -(CONV_W - 1):]


def run_trunk(x, c, pos, nsa_fn, gdn_fn, w_ada, b_ada, g_pre_mix, w_in, w_out, g_post_mix, g_pre_ffn, w_gate, w_up, w_down, g_post_ffn):
    B, T, _ = x.shape
    offs = np.cumsum(IN_SPLITS)[:-1].tolist()
    states = []
    for l in range(DEPTH):
        mod = jax.nn.silu(c) @ w_ada[l] + b_ada[l]
        sh1, sc1, ga1, sh2, sc2, ga2 = jnp.split(mod[:, None, :], 6, axis=-1)
        h = rmsnorm(x, g_pre_mix[l]) * (1 + sc1) + sh1
        q, kc, vc, ks, vs, kw, vw, gl, conv_in, a, bb, z = jnp.split(h @ w_in[l], offs, axis=-1)
        q = rope(q.reshape(B, T, NSA_HEADS, HEAD_DIM), pos)
        kc, ks, kw = (rope(t.reshape(B, T, NSA_KV, HEAD_DIM), pos) for t in (kc, ks, kw))
        vc, vs, vw = (t.reshape(B, T, NSA_KV, HEAD_DIM) for t in (vc, vs, vw))
        gates = jax.nn.sigmoid(gl.astype(jnp.float32)).reshape(B, T, NSA_HEADS, 3)
        o_nsa, nsa_state = nsa_fn(l, q, gates, kc, vc, ks, vs, kw, vw)
        o_gdn, gdn_state = gdn_fn(l, conv_in, a, bb, z)
        mix = jnp.concatenate([o_nsa, o_gdn], axis=-1) @ w_out[l]
        x = x + ga1 * rmsnorm(mix, g_post_mix[l])
        h = rmsnorm(x, g_pre_ffn[l]) * (1 + sc2) + sh2
        f = (jax.nn.silu(h @ w_gate[l]) * (h @ w_up[l])) @ w_down[l]
        x = x + ga2 * rmsnorm(f, g_post_ffn[l])
        states.append(nsa_state + gdn_state)
    stacked = [jnp.stack(s, axis=1) for s in zip(*states)]
    return x, stacked


def setup_inputs(seed: int = 0) -> dict:
    key = jax.random.key(seed)
    keys = iter(jax.random.split(key, 40))

    def nrm(shape, scale):
        return jax.random.normal(next(keys), shape, jnp.float32) * scale

    def gain(width=D_MODEL):
        return 1.0 + nrm((DEPTH, width), 0.05)

    n_pages = PAST_LEN // PAGE_SIZE
    n_pool = (5 * DEC_BATCH * n_pages + 3) // 4
    w_buf = min(WINDOW, PAST_LEN)
    perm = jax.random.permutation(next(keys), n_pool)
    page_table = perm[:DEC_BATCH * n_pages].reshape(DEC_BATCH, n_pages).astype(jnp.int32)
    dt = jnp.exp(jax.random.uniform(next(keys), (DEPTH, GDN_HEADS), jnp.float32, math.log(1e-3), math.log(1e-1)))
    return {
        'x_prompt': nrm((BATCH, SEQ, D_MODEL), 1.0),
        'x_sample': nrm((DEC_BATCH, DEC_SEQ, D_MODEL), 1.0),
        'cache_cmp_kv': nrm((n_pool, DEPTH, NSA_KV, PAGE_SIZE, 2, HEAD_DIM), 1.0),
        'cache_sel_kv': nrm((n_pool, DEPTH, NSA_KV, PAGE_SIZE, 2, HEAD_DIM), 1.0),
        'cache_win_kv': nrm((DEC_BATCH, DEPTH, NSA_KV, w_buf, 2, HEAD_DIM), 1.0),
        'state_gdn': nrm((DEC_BATCH, DEPTH, GDN_HEADS, GDN_DK, GDN_DV), GDN_DK ** -0.5),
        'state_conv': nrm((DEC_BATCH, DEPTH, CONV_W - 1, CONV_CH), 1.0),
        'page_table': page_table,
        'c_prompt': nrm((BATCH, D_MODEL), 1.0),
        'c_sample': nrm((DEC_BATCH, D_MODEL), 1.0),
        'w_ada': nrm((DEPTH, D_MODEL, 6 * D_MODEL), 0.5 * D_MODEL ** -0.5),
        'b_ada': nrm((DEPTH, 6 * D_MODEL), 0.02),
        'g_pre_mix': gain(),
        'w_in': nrm((DEPTH, D_MODEL, IN_W), D_MODEL ** -0.5),
        'cmp_pe': nrm((DEPTH, 2, BLOCK, HEAD_DIM), 0.1),
        'cmp_w1': nrm((DEPTH, 2, BLOCK * HEAD_DIM, CMP_HID), (BLOCK * HEAD_DIM) ** -0.5),
        'cmp_w2': nrm((DEPTH, 2, CMP_HID, HEAD_DIM), CMP_HID ** -0.5),
        'conv_w': nrm((DEPTH, CONV_W, CONV_CH), CONV_W ** -0.5),
        'gdn_a_log': jnp.log(jax.random.uniform(next(keys), (DEPTH, GDN_HEADS), jnp.float32, 1.0, 16.0)),
        'gdn_dt_bias': dt + jnp.log(-jnp.expm1(-dt)),
        'gdn_norm': gain(GDN_DV),
        'w_out': nrm((DEPTH, MIX_WIDTH, D_MODEL), MIX_WIDTH ** -0.5),
        'g_post_mix': gain(),
        'g_pre_ffn': gain(),
        'w_gate': nrm((DEPTH, D_MODEL, D_FF), D_MODEL ** -0.5),
        'w_up': nrm((DEPTH, D_MODEL, D_FF), D_MODEL ** -0.5),
        'w_down': nrm((DEPTH, D_FF, D_MODEL), D_FF ** -0.5),
        'g_post_ffn': gain(),
    }


def reference(x_prompt, x_sample, cache_cmp_kv, cache_sel_kv, cache_win_kv, state_gdn, state_conv, page_table,
              c_prompt, c_sample, w_ada, b_ada, g_pre_mix, w_in, cmp_pe, cmp_w1, cmp_w2, conv_w, gdn_a_log,
              gdn_dt_bias, gdn_norm, w_out, g_post_mix, g_pre_ffn, w_gate, w_up, w_down, g_post_ffn):
    bi_kv = jnp.arange(NSA_KV)[None, None, :, None]

    def comp(l, rows, i):
        return compress_blocks(rows, cmp_pe[l, i], cmp_w1[l, i], cmp_w2[l, i])

    def gdn_run(l, conv_in, a, b, z, buf0, s0):
        o, S, buf = gdn_mixer(conv_in, a, b, z, buf0, s0, conv_w[l], gdn_a_log[l], gdn_dt_bias[l], gdn_norm[l])
        return o, (S, buf)

    def nsa_prompt(l, q, gates, kc, vc, ks, vs, kw, vw):
        B, S = q.shape[:2]
        nb = S // BLOCK
        kcb, vcb = comp(l, kc, 0), comp(l, vc, 1)
        ks_store = ks.reshape(B, nb, BLOCK, NSA_KV, HEAD_DIM).transpose(0, 3, 1, 2, 4)
        vs_store = vs.reshape(B, nb, BLOCK, NSA_KV, HEAD_DIM).transpose(0, 3, 1, 2, 4)
        bi = jnp.arange(B)[:, None, None, None]

        def gather(idx):
            return ks_store[bi, bi_kv, idx], vs_store[bi, bi_kv, idx]

        kw_pad = jnp.pad(kw, ((0, 0), (WINDOW, 0), (0, 0), (0, 0)))
        vw_pad = jnp.pad(vw, ((0, 0), (WINDOW, 0), (0, 0), (0, 0)))
        n_sel = min(TOPN, nb)

        def step(i):
            q0 = i * NSA_QBLOCK
            qb = lax.dynamic_slice_in_dim(q, q0, NSA_QBLOCK, axis=1)
            gb = lax.dynamic_slice_in_dim(gates, q0, NSA_QBLOCK, axis=1)
            kwb = lax.dynamic_slice_in_dim(kw_pad, q0, WINDOW + NSA_QBLOCK, axis=1)
            vwb = lax.dynamic_slice_in_dim(vw_pad, q0, WINDOW + NSA_QBLOCK, axis=1)
            q_pos = q0 + jnp.arange(NSA_QBLOCK)
            kw_pos = q0 - WINDOW + jnp.arange(WINDOW + NSA_QBLOCK)
            return nsa_core(qb, q_pos, gb, kcb, vcb, gather, n_sel, kwb, vwb, kw_pos)

        o = lax.map(step, jnp.arange(S // NSA_QBLOCK))
        o = jnp.moveaxis(o, 0, 1).reshape(B, S, NSA_Q_W)
        wl = min(WINDOW, S)
        return o, (kv_rows(kc, vc), kv_rows(ks, vs), kv_rows(kw[:, S - wl:], vw[:, S - wl:]))

    def gdn_prompt(l, conv_in, a, b, z):
        B = conv_in.shape[0]
        buf0 = jnp.zeros((B, CONV_W - 1, CONV_CH), conv_in.dtype)
        s0 = jnp.zeros((B, GDN_HEADS, GDN_DK, GDN_DV), jnp.float32)
        return gdn_run(l, conv_in, a, b, z, buf0, s0)

    def nsa_sample(l, q, gates, kc, vc, ks, vs, kw, vw):
        B, T = q.shape[:2]
        npb = PAST_LEN // BLOCK
        nb_new = -(-T // BLOCK)
        pad = nb_new * BLOCK - T
        bpp = PAGE_SIZE // BLOCK
        bi = jnp.arange(B)[:, None, None, None]

        def pad_rows(t):
            return jnp.pad(t, ((0, 0), (0, pad), (0, 0), (0, 0)))

        pages = cache_cmp_kv.reshape(-1, NSA_KV, PAGE_SIZE, 2, HEAD_DIM)[page_table * DEPTH + l]
        past = pages.transpose(0, 1, 3, 4, 2, 5).reshape(B, PAST_LEN, 2, NSA_KV, HEAD_DIM).astype(kc.dtype)
        kcb = comp(l, jnp.concatenate([past[:, :, 0], pad_rows(kc)], axis=1), 0)
        vcb = comp(l, jnp.concatenate([past[:, :, 1], pad_rows(vc)], axis=1), 1)
        sel_flat = cache_sel_kv.reshape(-1, BLOCK, 2, HEAD_DIM)
        new_store = jnp.stack([pad_rows(ks), pad_rows(vs)], axis=3)
        new_store = new_store.reshape(B, nb_new, BLOCK, NSA_KV, 2, HEAD_DIM).transpose(0, 3, 1, 2, 4, 5)

        def gather(idx):
            in_past = idx < npb
            jp = jnp.minimum(idx, npb - 1)
            phys = page_table[bi, jp // bpp]
            lin = ((phys * DEPTH + l) * NSA_KV + bi_kv) * bpp + jp % bpp
            kv_past = sel_flat[lin].astype(new_store.dtype)
            kv_new = new_store[bi, bi_kv, jnp.clip(idx - npb, 0, nb_new - 1)]
            kv = jnp.where(in_past[..., None, None, None], kv_past, kv_new)
            return kv[..., 0, :], kv[..., 1, :]

        win = cache_win_kv[:, l]
        w_buf = win.shape[2]
        kw_all = jnp.concatenate([win[..., 0, :].transpose(0, 2, 1, 3).astype(kw.dtype), kw], axis=1)
        vw_all = jnp.concatenate([win[..., 1, :].transpose(0, 2, 1, 3).astype(vw.dtype), vw], axis=1)
        kw_pos = PAST_LEN - w_buf + jnp.arange(w_buf + T)
        q_pos = PAST_LEN + jnp.arange(T)
        o = nsa_core(q, q_pos, gates, kcb, vcb, gather, min(TOPN, npb + nb_new), kw_all, vw_all, kw_pos)
        return o, (kv_rows(kc, vc), kv_rows(ks, vs), kv_rows(kw_all[:, -w_buf:], vw_all[:, -w_buf:]))

    def gdn_sample(l, conv_in, a, b, z):
        return gdn_run(l, conv_in, a, b, z, state_conv[:, l], state_gdn[:, l])

    y_prompt, p_states = run_trunk(x_prompt, c_prompt, jnp.arange(x_prompt.shape[1]), nsa_prompt, gdn_prompt,
                                   w_ada, b_ada, g_pre_mix, w_in, w_out, g_post_mix, g_pre_ffn, w_gate, w_up, w_down, g_post_ffn)
    y_sample, s_states = run_trunk(x_sample, c_sample, PAST_LEN + jnp.arange(x_sample.shape[1]), nsa_sample, gdn_sample,
                                   w_ada, b_ada, g_pre_mix, w_in, w_out, g_post_mix, g_pre_ffn, w_gate, w_up, w_down, g_post_ffn)
    p_cmp, p_sel, p_win, p_gdn, p_conv = p_states
    s_cmp, s_sel, s_win, s_gdn, s_conv = s_states
    return (y_prompt, y_sample, p_cmp, p_sel, p_win, p_gdn, p_conv, s_cmp, s_sel, s_win, s_gdn, s_conv)
```

```python
import functools
import math

import jax
import jax.numpy as jnp
from jax import lax
from jax.experimental import pallas as pl
from jax.experimental.pallas import tpu as pltpu

F32 = jnp.float32
BF16 = jnp.bfloat16

HEAD_DIM = 128
NSA_HEADS = 8
NSA_KV = 2
NSA_HPG = NSA_HEADS // NSA_KV
GDN_HEADS = 8
ROT_DIM = HEAD_DIM // 4
ROPE_THETA = 500000.0
BLOCK = 64
TOPN = 16
WINDOW = 512
CMP_HID = 256
CONV_W = 4
GDN_CHUNK = 64
EPS = 1e-6
KV_ROW = 2 * HEAD_DIM
BLK_ROW = BLOCK * KV_ROW
NSA_Q_W = NSA_HEADS * HEAD_DIM
GDN_W = GDN_HEADS * HEAD_DIM
CONV_CH = 3 * GDN_W
SMALL_W = 2 * HEAD_DIM
A_COL = 12
B_COL = 20
OFF_CONV = 0
OFF_Z = CONV_CH
OFF_Q = OFF_Z + GDN_W
OFF_CMP = OFF_Q + NSA_Q_W
OFF_SEL = OFF_CMP + NSA_KV * KV_ROW
OFF_WIN = OFF_SEL + NSA_KV * KV_ROW
MAIN_W = OFF_WIN + NSA_KV * KV_ROW
VMEM_LIMIT = 48 * 1024 * 1024
NEG_INF = float("-inf")
SCALE = HEAD_DIM ** -0.5


def _cparams(sem):
    return pltpu.CompilerParams(dimension_semantics=sem, vmem_limit_bytes=VMEM_LIMIT)


def _dot(a, b):
    return jnp.dot(a.astype(BF16), b.astype(BF16), preferred_element_type=F32)


def _dot_nt(a, b):
    return lax.dot_general(a.astype(BF16), b.astype(BF16), (((1,), (1,)), ((), ())),
                           preferred_element_type=F32)


def _dot_exact(a, b):
    return jnp.dot(a, b, preferred_element_type=F32, precision=lax.Precision.HIGHEST)


def _dot_nt_exact(a, b):
    return lax.dot_general(a, b, (((1,), (1,)), ((), ())), preferred_element_type=F32,
                           precision=lax.Precision.HIGHEST)


def _sigmoid(x):
    return 1.0 / (1.0 + jnp.exp(-x))


def _silu(x):
    return x * _sigmoid(x)


def _rms(x, g):
    return x * lax.rsqrt(jnp.mean(x * x, axis=-1, keepdims=True) + EPS) * g


def _masked_softmax(s, mask):
    s = jnp.where(mask, s, NEG_INF)
    m = jnp.max(s, axis=-1, keepdims=True)
    m = jnp.where(m == NEG_INF, 0.0, m)
    e = jnp.where(mask, jnp.exp(s - m), 0.0)
    return e / jnp.maximum(jnp.sum(e, axis=-1, keepdims=True), 1e-30)


def _ada_kernel(c_ref, w_ref, b_ref, o_ref):
    o_ref[0] = _dot(_silu(c_ref[...]), w_ref[0]) + b_ref[0]


def _ada_mod(c_all, w_ada, b_ada, tn=1024):
    depth, d, n = w_ada.shape
    r = c_all.shape[0]
    return pl.pallas_call(
        _ada_kernel,
        out_shape=jax.ShapeDtypeStruct((depth, r, n), F32),
        grid=(depth, n // tn),
        in_specs=[pl.BlockSpec((r, d), lambda l, j: (0, 0)),
                  pl.BlockSpec((1, d, tn), lambda l, j: (l, 0, j)),
                  pl.BlockSpec((1, 1, tn), lambda l, j: (l, 0, j))],
        out_specs=pl.BlockSpec((1, r, tn), lambda l, j: (l, 0, j)),
        compiler_params=_cparams(("arbitrary", "arbitrary")),
        name="ada_mod",
    )(c_all, w_ada, b_ada.reshape(depth, 1, n))


def _inproj_kernel(x_ref, g_ref, sc_ref, sh_ref, w_ref, ws_ref, o_ref, os_ref, h_ref):
    @pl.when(pl.program_id(1) == 0)
    def _():
        h = _rms(x_ref[...], g_ref[...]) * (1.0 + sc_ref[0]) + sh_ref[0]
        hb = h.astype(BF16)
        h_ref[...] = hb
        os_ref[...] = jnp.dot(hb, ws_ref[0], preferred_element_type=F32)

    o_ref[...] = jnp.dot(h_ref[...], w_ref[0], preferred_element_type=F32)


def _mod_spec(mod, rows_per_mod, tm):
    d = mod.shape[-1]
    if mod.shape[1] == 1:
        return pl.BlockSpec((1, 1, d), lambda i, *_: ((i * tm) // rows_per_mod, 0, 0))
    return pl.BlockSpec((1, tm, d), lambda i, *_: (0, i, 0))


def _in_proj(x, g, sc, sh, w_main, w_small, l, rows_per_mod, tm, tn=512):
    m, d = x.shape
    return pl.pallas_call(
        _inproj_kernel,
        out_shape=(jax.ShapeDtypeStruct((m, MAIN_W), F32), jax.ShapeDtypeStruct((m, SMALL_W), F32)),
        grid=(m // tm, MAIN_W // tn),
        in_specs=[pl.BlockSpec((tm, d), lambda i, j: (i, 0)),
                  pl.BlockSpec((1, d), lambda i, j: (0, 0)),
                  _mod_spec(sc, rows_per_mod, tm), _mod_spec(sh, rows_per_mod, tm),
                  pl.BlockSpec((1, d, tn), lambda i, j: (l, 0, j)),
                  pl.BlockSpec((1, d, SMALL_W), lambda i, j: (l, 0, 0))],
        out_specs=(pl.BlockSpec((tm, tn), lambda i, j: (i, j)),
                   pl.BlockSpec((tm, SMALL_W), lambda i, j: (i, 0))),
        scratch_shapes=[pltpu.VMEM((tm, d), BF16)],
        compiler_params=_cparams(("arbitrary", "arbitrary")),
        name="in_proj",
    )(x, g, sc, sh, w_main, w_small)


def _rope(x, c, s1, s2):
    half = ROT_DIM // 2
    return x * c + pltpu.roll(x, HEAD_DIM - half, 1) * s1 + pltpu.roll(x, half, 1) * s2


def _postproj_kernel(q_ref, cmp_ref, sel_ref, win_ref, c_ref, s1_ref, s2_ref,
                     qo_ref, cmpo_ref, selo_ref, wino_ref):
    c, s1, s2 = c_ref[...], s1_ref[...], s2_ref[...]
    for h in range(NSA_HEADS):
        sl = slice(h * HEAD_DIM, (h + 1) * HEAD_DIM)
        qo_ref[:, sl] = _rope(q_ref[:, sl], c, s1, s2).astype(BF16)
    for src, dst in ((cmp_ref, cmpo_ref), (sel_ref, selo_ref), (win_ref, wino_ref)):
        for g in range(NSA_KV):
            base = g * KV_ROW
            dst[0, g, :, 0:HEAD_DIM] = _rope(src[:, base:base + HEAD_DIM], c, s1, s2)
            dst[0, g, :, HEAD_DIM:KV_ROW] = src[:, base + HEAD_DIM:base + KV_ROW]


def _post_proj(proj, rope_tabs, b, t, tt):
    nt = t // tt
    state = jax.ShapeDtypeStruct((b, NSA_KV, t, KV_ROW), F32)
    kvw = NSA_KV * KV_ROW
    tab_spec = pl.BlockSpec((tt, HEAD_DIM), lambda bi, ti: (ti, 0))
    st_spec = pl.BlockSpec((1, NSA_KV, tt, KV_ROW), lambda bi, ti: (bi, 0, ti, 0))
    return pl.pallas_call(
        _postproj_kernel,
        out_shape=(jax.ShapeDtypeStruct((b * t, NSA_Q_W), BF16), state, state, state),
        grid=(b, nt),
        in_specs=[pl.BlockSpec((tt, NSA_Q_W), lambda bi, ti: (bi * nt + ti, OFF_Q // NSA_Q_W)),
                  pl.BlockSpec((tt, kvw), lambda bi, ti: (bi * nt + ti, OFF_CMP // kvw)),
                  pl.BlockSpec((tt, kvw), lambda bi, ti: (bi * nt + ti, OFF_SEL // kvw)),
                  pl.BlockSpec((tt, kvw), lambda bi, ti: (bi * nt + ti, OFF_WIN // kvw)),
                  tab_spec, tab_spec, tab_spec],
        out_specs=(pl.BlockSpec((tt, NSA_Q_W), lambda bi, ti: (bi * nt + ti, 0)),
                   st_spec, st_spec, st_spec),
        compiler_params=_cparams(("arbitrary", "arbitrary")),
        name="post_proj",
    )(proj, proj, proj, proj, *rope_tabs)


def _rope_tables(pos):
    half = ROT_DIM // 2
    inv = ROPE_THETA ** (-jnp.arange(half, dtype=F32) * 2.0 / ROT_DIM)
    ang = pos.astype(F32)[:, None] * inv
    cos, sin = jnp.cos(ang), jnp.sin(ang)
    n = pos.shape[0]
    rest = HEAD_DIM - ROT_DIM
    c = jnp.concatenate([cos, cos, jnp.ones((n, rest), F32)], axis=1)
    s1 = jnp.concatenate([-sin, jnp.zeros((n, HEAD_DIM - half), F32)], axis=1)
    s2 = jnp.concatenate([jnp.zeros((n, half), F32), sin, jnp.zeros((n, rest), F32)], axis=1)
    return c, s1, s2


def _gelu_tanh(x):
    return 0.5 * x * (1.0 + jnp.tanh(math.sqrt(2.0 / math.pi) * (x + 0.044715 * (x * x * x))))


def _compress_body(x_ref, pe_ref, w1k_ref, w1v_ref, w2k_ref, w2v_ref, ko_ref, vo_ref):
    rows = x_ref.shape[0]

    def step(r, acc):
        hk, hv = acc
        off = pl.multiple_of(r * KV_ROW, KV_ROW)
        xk = x_ref[:, pl.ds(off, HEAD_DIM)] + pe_ref[0, pl.ds(r, 1), :]
        xv = x_ref[:, pl.ds(pl.multiple_of(off + HEAD_DIM, HEAD_DIM), HEAD_DIM)] + pe_ref[1, pl.ds(r, 1), :]
        return hk + _dot(xk, w1k_ref[r]), hv + _dot(xv, w1v_ref[r])

    zero = jnp.zeros((rows, CMP_HID), F32)
    hk, hv = lax.fori_loop(0, BLOCK, step, (zero, zero))
    ko_ref[...] = _dot(_gelu_tanh(hk), w2k_ref[...])
    vo_ref[...] = _dot(_gelu_tanh(hv), w2v_ref[...])


def _compress_kernel(x_ref, *rest):
    _compress_body(x_ref, *rest)


def _compress_weights(cmp_pe, cmp_w1, cmp_w2, l):
    w1 = cmp_w1[l].astype(BF16).reshape(2, BLOCK, HEAD_DIM, CMP_HID)
    w2 = cmp_w2[l].astype(BF16)
    return cmp_pe[l], w1[0], w1[1], w2[0], w2[1]


def _cw_specs(nargs):
    z = (0,) * 3
    return [pl.BlockSpec((2, BLOCK, HEAD_DIM), lambda *a: z),
            pl.BlockSpec((BLOCK, HEAD_DIM, CMP_HID), lambda *a: z),
            pl.BlockSpec((BLOCK, HEAD_DIM, CMP_HID), lambda *a: z),
            pl.BlockSpec((CMP_HID, HEAD_DIM), lambda *a: (0, 0)),
            pl.BlockSpec((CMP_HID, HEAD_DIM), lambda *a: (0, 0))]


def _compress(x_blocks, cw, tr):
    r = x_blocks.shape[0]
    out = jax.ShapeDtypeStruct((r, HEAD_DIM), F32)
    return pl.pallas_call(
        _compress_kernel,
        out_shape=(out, out),
        grid=(r // tr,),
        in_specs=[pl.BlockSpec((tr, BLK_ROW), lambda i: (i, 0))] + _cw_specs(1),
        out_specs=(pl.BlockSpec((tr, HEAD_DIM), lambda i: (i, 0)),) * 2,
        compiler_params=_cparams(("arbitrary",)),
        name="compress",
    )(x_blocks, *cw)


PAGES_PER_STEP = 16


def _compress_paged_kernel(pt_ref, *refs):
    n = PAGES_PER_STEP
    page_refs, rest, x_scr = refs[:n], refs[n:-1], refs[-1]
    rpp = page_refs[0].shape[2]
    for k in range(n):
        x_scr[k * rpp:(k + 1) * rpp, :] = page_refs[k][0, 0]
    _compress_body(x_scr, *rest)


def _compress_paged(cache, page_table, cw, l):
    b, n_pages = page_table.shape
    rpp = cache.shape[2]
    n = PAGES_PER_STEP
    steps = n_pages // n
    tr = n * rpp
    out = jax.ShapeDtypeStruct((b * steps * tr, HEAD_DIM), F32)

    def page_spec(k):
        return pl.BlockSpec((1, 1, rpp, BLK_ROW), lambda bi, s, pt: (pt[bi, s * n + k], l, 0, 0))

    grid_spec = pltpu.PrefetchScalarGridSpec(
        num_scalar_prefetch=1,
        grid=(b, steps),
        in_specs=[page_spec(k) for k in range(n)] + _cw_specs(3),
        out_specs=(pl.BlockSpec((tr, HEAD_DIM), lambda bi, s, pt: (bi * steps + s, 0)),) * 2,
        scratch_shapes=[pltpu.VMEM((tr, BLK_ROW), F32)],
    )
    return pl.pallas_call(
        _compress_paged_kernel,
        out_shape=(out, out),
        grid_spec=grid_spec,
        compiler_params=_cparams(("arbitrary", "arbitrary")),
        name="compress_paged",
    )(page_table, *([cache] * n), *cw)


def _stable_topn_mask(score, n_sel):
    nb = score.shape[-1]
    jcol = lax.broadcasted_iota(jnp.int32, score.shape, 1)
    rank = jnp.zeros(score.shape, F32)
    for k in range(nb):
        ck = score[:, k:k + 1]
        beats = (ck > score) | ((ck == score) & (jcol > k))
        rank = rank + jnp.where(beats, 1.0, 0.0)
    return jnp.where(rank < n_sel, 1.0, 0.0)


def _nsa_prompt_kernel(q_ref, sm_ref, kcb_ref, vcb_ref, sel_ref, win_ref, exp_ref, o_ref, *, n_sel, wl):
    i = pl.program_id(2)
    t_all = sel_ref.shape[2]
    nb = kcb_ref.shape[2]
    rows = NSA_HPG * BLOCK
    q = q_ref[...]
    qs = jnp.concatenate([q[:, h * HEAD_DIM:(h + 1) * HEAD_DIM] for h in range(NSA_HPG)], axis=0)
    q0 = i * BLOCK
    qpos = q0 + (lax.broadcasted_iota(jnp.int32, (rows, 1), 0) & (BLOCK - 1))

    jc = lax.broadcasted_iota(jnp.int32, (rows, nb), 1)
    cmask = (jc * BLOCK + (BLOCK - 1)) <= qpos
    p_c = _masked_softmax(_dot_nt(qs, kcb_ref[0, 0]) * SCALE, cmask)
    o_c = _dot(p_c, vcb_ref[0, 0])

    imp = p_c[0:BLOCK]
    for h in range(1, NSA_HPG):
        imp = imp + p_c[h * BLOCK:(h + 1) * BLOCK]
    jb = lax.broadcasted_iota(jnp.int32, (BLOCK, nb), 1)
    forced = (jb == 0) | (jb == i) | (jb == i - 1)
    score = jnp.where(forced, jnp.inf, jnp.where(jb > i, NEG_INF, imp))
    sel = _stable_topn_mask(score, n_sel)
    selk = jnp.dot(sel.astype(BF16), exp_ref[...], preferred_element_type=F32)
    selk = jnp.concatenate([selk] * NSA_HPG, axis=0)
    kpos = lax.broadcasted_iota(jnp.int32, (rows, t_all), 1)
    smask = (selk > 0.5) & (kpos <= qpos)
    p_s = _masked_softmax(_dot_nt(qs, sel_ref[0, 0, :, 0:HEAD_DIM]) * SCALE, smask)
    o_s = _dot(p_s, sel_ref[0, 0, :, HEAD_DIM:KV_ROW])

    start = pl.multiple_of(jnp.clip(q0 - WINDOW, 0, t_all - wl), BLOCK)
    wpos = start + lax.broadcasted_iota(jnp.int32, (rows, wl), 1)
    wmask = (wpos <= qpos) & (wpos >= qpos - WINDOW)
    p_w = _masked_softmax(_dot_nt(qs, win_ref[0, 0, pl.ds(start, wl), 0:HEAD_DIM]) * SCALE, wmask)
    o_w = _dot(p_w, win_ref[0, 0, pl.ds(start, wl), HEAD_DIM:KV_ROW])

    gt = _sigmoid(sm_ref[...])
    for h in range(NSA_HPG):
        r = slice(h * BLOCK, (h + 1) * BLOCK)
        o = (gt[:, 3 * h:3 * h + 1] * o_c[r] + gt[:, 3 * h + 1:3 * h + 2] * o_s[r]
             + gt[:, 3 * h + 2:3 * h + 3] * o_w[r])
        o_ref[:, h * HEAD_DIM:(h + 1) * HEAD_DIM] = o.astype(BF16)


def _nsa_prompt(q_rot, small, kcb, vcb, sel_st, win_st, b, t):
    nq = t // BLOCK
    nb = kcb.shape[2]
    gw = NSA_HPG * HEAD_DIM
    wl = min(WINDOW + BLOCK, t)
    expand = (jnp.arange(t)[None, :] // BLOCK == jnp.arange(nb)[:, None]).astype(BF16)
    st_spec = pl.BlockSpec((1, 1, t, KV_ROW), lambda bi, g, i: (bi, g, 0, 0))
    cb_spec = pl.BlockSpec((1, 1, nb, HEAD_DIM), lambda bi, g, i: (bi, g, 0, 0))
    return pl.pallas_call(
        functools.partial(_nsa_prompt_kernel, n_sel=min(TOPN, nb), wl=wl),
        out_shape=jax.ShapeDtypeStruct((b * t, NSA_Q_W), BF16),
        grid=(b, NSA_KV, nq),
        in_specs=[pl.BlockSpec((BLOCK, gw), lambda bi, g, i: (bi * nq + i, g)),
                  pl.BlockSpec((BLOCK, HEAD_DIM), lambda bi, g, i: (bi * nq + i, g)),
                  cb_spec, cb_spec, st_spec, st_spec,
                  pl.BlockSpec((nb, t), lambda bi, g, i: (0, 0))],
        out_specs=pl.BlockSpec((BLOCK, gw), lambda bi, g, i: (bi * nq + i, g)),
        compiler_params=_cparams(("arbitrary", "arbitrary", "arbitrary")),
        name="nsa_prompt",
    )(q_rot, small, kcb, vcb, sel_st, win_st, expand)


def _softplus(x):
    return jnp.maximum(x, 0.0) + jnp.log(1.0 + jnp.exp(-jnp.abs(x)))


def _unit_lower_solve(l_strict, rhs):
    a = -l_strict
    x = rhs + _dot_exact(a, rhs)
    n = 2
    while n < l_strict.shape[0]:
        a = _dot_exact(a, a)
        x = x + _dot_exact(a, x)
        n *= 2
    return x


def _gdn_prompt_kernel(x_ref, z_ref, sm_ref, abr_ref, cw_ref, prow_ref, pcol_ref, nw_ref,
                       o_ref, s_out_ref, conv_out_ref, s_ref, xe_ref):
    n = pl.program_id(1)
    c = GDN_CHUNK
    pad = 8

    @pl.when(n == 0)
    def _():
        s_ref[...] = jnp.zeros_like(s_ref)
        xe_ref[0:pad, :] = jnp.zeros((pad, CONV_CH), F32)

    xe_ref[pad:pad + c, :] = x_ref[...]
    y = xe_ref[pl.ds(pad, c), :] * cw_ref[CONV_W - 1:CONV_W, :]
    for w in range(CONV_W - 1):
        y = y + xe_ref[pl.ds(pad - (CONV_W - 1) + w, c), :] * cw_ref[w:w + 1, :]
    y = _silu(y)

    ri = lax.broadcasted_iota(jnp.int32, (c, c), 0)
    ci = lax.broadcasted_iota(jnp.int32, (c, c), 1)
    tri = ri >= ci
    strict = ri > ci
    tri_f = jnp.where(tri, 1.0, 0.0)

    a_col = sm_ref[:, A_COL:A_COL + GDN_HEADS]
    b_col = sm_ref[:, B_COL:B_COL + GDN_HEADS]
    g_col = -jnp.exp(prow_ref[0:1, :]) * _softplus(a_col + prow_ref[1:2, :])
    g_row = -jnp.exp(pcol_ref[:, 0:1]) * _softplus(abr_ref[0, 0, 0:GDN_HEADS, :] + pcol_ref[:, 1:2])
    gc_col = _dot_exact(tri_f, g_col)
    gc_row = _dot_nt_exact(g_row, tri_f)
    beta_col = _sigmoid(b_col)

    for h in range(GDN_HEADS):
        sl = slice(h * HEAD_DIM, (h + 1) * HEAD_DIM)
        yq, yk, v = y[:, sl], y[:, GDN_W + h * HEAD_DIM:GDN_W + (h + 1) * HEAD_DIM], \
            y[:, 2 * GDN_W + h * HEAD_DIM:2 * GDN_W + (h + 1) * HEAD_DIM]
        q = yq * lax.rsqrt(jnp.sum(yq * yq, axis=-1, keepdims=True) + EPS) * SCALE
        k = yk * lax.rsqrt(jnp.sum(yk * yk, axis=-1, keepdims=True) + EPS)
        gcc = gc_col[:, h:h + 1]
        gcr = gc_row[h:h + 1, :]
        beta = beta_col[:, h:h + 1]
        decay = jnp.exp(jnp.where(tri, gcc - gcr, NEG_INF))
        kb = k * beta
        lmat = jnp.where(strict, _dot_nt(kb, k) * decay, 0.0)
        rhs = jnp.concatenate([v * beta, kb * jnp.exp(gcc)], axis=1)
        sol = _unit_lower_solve(lmat, rhs)
        u, w = sol[:, :HEAD_DIM], sol[:, HEAD_DIM:]
        qk = _dot_nt(q, k) * decay
        s = s_ref[h]
        v_new = u - _dot(w, s)
        o = _dot(q * jnp.exp(gcc), s) + _dot(qk, v_new)
        g_last = gcc[c - 1:c, :]
        k_dec = k * jnp.exp(g_last - gcc)
        s_new = s * jnp.exp(g_last) + lax.dot_general(
            k_dec.astype(BF16), v_new.astype(BF16), (((0,), (0,)), ((), ())), preferred_element_type=F32)
        s_ref[h] = s_new
        o = _rms(o, nw_ref[...]) * _silu(z_ref[:, sl])
        o_ref[:, sl] = o.astype(BF16)

    xe_ref[0:pad, :] = xe_ref[c:c + pad, :]

    @pl.when(n == pl.num_programs(1) - 1)
    def _():
        s_out_ref[0] = s_ref[...]
        conv_out_ref[0] = xe_ref[pad - (CONV_W - 1):pad, :]


def _gdn_prompt(proj, small, conv_w, a_log, dt_bias, norm_w, b, t):
    c = GDN_CHUNK
    nc = t // c
    ab_row = small.reshape(b, nc, c, SMALL_W)[..., A_COL:A_COL + 2 * GDN_HEADS].transpose(0, 1, 3, 2)
    prow = jnp.stack([a_log, dt_bias], axis=0)
    pcol = prow.T
    return pl.pallas_call(
        _gdn_prompt_kernel,
        out_shape=(jax.ShapeDtypeStruct((b * t, GDN_W), BF16),
                   jax.ShapeDtypeStruct((b, GDN_HEADS, HEAD_DIM, HEAD_DIM), F32),
                   jax.ShapeDtypeStruct((b, CONV_W - 1, CONV_CH), F32)),
        grid=(b, nc),
        in_specs=[pl.BlockSpec((c, CONV_CH), lambda bi, n: (bi * nc + n, OFF_CONV // CONV_CH)),
                  pl.BlockSpec((c, GDN_W), lambda bi, n: (bi * nc + n, OFF_Z // GDN_W)),
                  pl.BlockSpec((c, HEAD_DIM), lambda bi, n: (bi * nc + n, 0)),
                  pl.BlockSpec((1, 1, 2 * GDN_HEADS, c), lambda bi, n: (bi, n, 0, 0)),
                  pl.BlockSpec((CONV_W, CONV_CH), lambda bi, n: (0, 0)),
                  pl.BlockSpec((2, GDN_HEADS), lambda bi, n: (0, 0)),
                  pl.BlockSpec((GDN_HEADS, 2), lambda bi, n: (0, 0)),
                  pl.BlockSpec((1, HEAD_DIM), lambda bi, n: (0, 0))],
        out_specs=(pl.BlockSpec((c, GDN_W), lambda bi, n: (bi * nc + n, 0)),
                   pl.BlockSpec((1, GDN_HEADS, HEAD_DIM, HEAD_DIM), lambda bi, n: (bi, 0, 0, 0)),
                   pl.BlockSpec((1, CONV_W - 1, CONV_CH), lambda bi, n: (bi, 0, 0))),
        scratch_shapes=[pltpu.VMEM((GDN_HEADS, HEAD_DIM, HEAD_DIM), F32),
                        pltpu.VMEM((c + 16, CONV_CH), F32)],
        compiler_params=_cparams(("arbitrary", "arbitrary")),
        name="gdn_prompt",
    )(proj, proj, small, ab_row, conv_w, prow, pcol, norm_w.reshape(1, HEAD_DIM))


def _outproj_kernel(on_ref, og_ref, w_ref, x_ref, ga_ref, g_ref, o_ref):
    o = jnp.concatenate([on_ref[...], og_ref[...]], axis=1)
    mix = jnp.dot(o, w_ref[0], preferred_element_type=F32)
    o_ref[...] = x_ref[...] + ga_ref[0] * _rms(mix, g_ref[...])


def _out_proj(o_nsa, o_gdn, w_out, x, ga, g, l, rows_per_mod, tm):
    m, d = x.shape
    kw = o_nsa.shape[1]
    return pl.pallas_call(
        _outproj_kernel,
        out_shape=jax.ShapeDtypeStruct((m, d), F32),
        grid=(m // tm,),
        in_specs=[pl.BlockSpec((tm, kw), lambda i: (i, 0)),
                  pl.BlockSpec((tm, kw), lambda i: (i, 0)),
                  pl.BlockSpec((1, 2 * kw, d), lambda i: (l, 0, 0)),
                  pl.BlockSpec((tm, d), lambda i: (i, 0)),
                  _mod_spec(ga, rows_per_mod, tm),
                  pl.BlockSpec((1, d), lambda i: (0, 0))],
        out_specs=pl.BlockSpec((tm, d), lambda i: (i, 0)),
        compiler_params=_cparams(("arbitrary",)),
        name="out_proj",
    )(o_nsa, o_gdn, w_out, x, ga, g)


def _ffn_kernel(x_ref, g1_ref, sc_ref, sh_ref, wg_ref, wu_ref, wd_ref, ga_ref, g2_ref, o_ref, h_ref, acc_ref):
    j = pl.program_id(1)

    @pl.when(j == 0)
    def _():
        h = _rms(x_ref[...], g1_ref[...]) * (1.0 + sc_ref[0]) + sh_ref[0]
        h_ref[...] = h.astype(BF16)
        acc_ref[...] = jnp.zeros_like(acc_ref)

    h = h_ref[...]
    a = jnp.dot(h, wg_ref[0], preferred_element_type=F32)
    u = jnp.dot(h, wu_ref[0], preferred_element_type=F32)
    acc_ref[...] += _dot(_silu(a) * u, wd_ref[0])

    @pl.when(j == pl.num_programs(1) - 1)
    def _():
        o_ref[...] = x_ref[...] + ga_ref[0] * _rms(acc_ref[...], g2_ref[...])


def _ffn(x, g1, sc, sh, w_gate, w_up, w_down, ga, g2, l, rows_per_mod, tm, tf=512):
    m, d = x.shape
    f = w_gate.shape[2]
    vec = pl.BlockSpec((1, d), lambda i, j: (0, 0))
    return pl.pallas_call(
        _ffn_kernel,
        out_shape=jax.ShapeDtypeStruct((m, d), F32),
        grid=(m // tm, f // tf),
        in_specs=[pl.BlockSpec((tm, d), lambda i, j: (i, 0)), vec,
                  _mod_spec(sc, rows_per_mod, tm), _mod_spec(sh, rows_per_mod, tm),
                  pl.BlockSpec((1, d, tf), lambda i, j: (l, 0, j)),
                  pl.BlockSpec((1, d, tf), lambda i, j: (l, 0, j)),
                  pl.BlockSpec((1, tf, d), lambda i, j: (l, j, 0)),
                  _mod_spec(ga, rows_per_mod, tm), vec],
        out_specs=pl.BlockSpec((tm, d), lambda i, j: (i, 0)),
        scratch_shapes=[pltpu.VMEM((tm, d), BF16), pltpu.VMEM((tm, d), F32)],
        compiler_params=_cparams(("arbitrary", "arbitrary")),
        name="ffn",
    )(x, g1, sc, sh, w_gate, w_up, w_down, ga, g2)


def _stack_heads(q, h0):
    rows = [q[:, (h0 + h) * HEAD_DIM:(h0 + h + 1) * HEAD_DIM] for h in range(NSA_HPG)]
    return jnp.concatenate(rows + [jnp.zeros((NSA_HPG, HEAD_DIM), F32)], axis=0)


def _nsa_sample_cmp_kernel(q_ref, kcb_ref, vcb_ref, oc_ref, idx_ref, *, n_sel, q_pos):
    nbp = kcb_ref.shape[2]
    cur = q_pos // BLOCK
    q = q_ref[0].astype(F32)
    for g in range(NSA_KV):
        qs = _stack_heads(q, g * NSA_HPG)
        jc = lax.broadcasted_iota(jnp.int32, (qs.shape[0], nbp), 1)
        cmask = (jc * BLOCK + (BLOCK - 1)) <= q_pos
        p_c = _masked_softmax(_dot_nt(qs, kcb_ref[0, g]) * SCALE, cmask)
        o_c = _dot(p_c, vcb_ref[0, g])
        for h in range(NSA_HPG):
            c0 = (g * NSA_HPG + h) * HEAD_DIM
            oc_ref[0, :, c0:c0 + HEAD_DIM] = o_c[h:h + 1]
        imp = p_c[0:1]
        for h in range(1, NSA_HPG):
            imp = imp + p_c[h:h + 1]
        jb = jc[0:1]
        forced = (jb == 0) | (jb == cur) | (jb == cur - 1)
        score = jnp.where(forced, jnp.inf, jnp.where(jb > cur, NEG_INF, imp))
        left = jb >= 0
        for r in range(n_sel):
            best = jnp.max(jnp.where(left, score, NEG_INF), axis=1, keepdims=True)
            pick = jnp.min(jnp.where(left & (score == best), jb, nbp), axis=1, keepdims=True)
            idx_ref[0, g, r:r + 1, :] = jnp.broadcast_to(pick, (1, HEAD_DIM))
            left = left & (jb != pick)


def _nsa_sample_cmp(q_rot, kcb, vcb, q_pos, n_blocks):
    b = q_rot.shape[0]
    nbp = kcb.shape[2]
    n_sel = min(TOPN, n_blocks)
    cb_spec = pl.BlockSpec((1, NSA_KV, nbp, HEAD_DIM), lambda bi: (bi, 0, 0, 0))
    return pl.pallas_call(
        functools.partial(_nsa_sample_cmp_kernel, n_sel=n_sel, q_pos=q_pos),
        out_shape=(jax.ShapeDtypeStruct((b, 1, NSA_Q_W), F32),
                   jax.ShapeDtypeStruct((b, NSA_KV, n_sel, HEAD_DIM), jnp.int32)),
        grid=(b,),
        in_specs=[pl.BlockSpec((1, 1, NSA_Q_W), lambda bi: (bi, 0, 0)), cb_spec, cb_spec],
        out_specs=(pl.BlockSpec((1, 1, NSA_Q_W), lambda bi: (bi, 0, 0)),
                   pl.BlockSpec((1, NSA_KV, n_sel, HEAD_DIM), lambda bi: (bi, 0, 0, 0))),
        compiler_params=_cparams(("arbitrary",)),
        name="nsa_sample_cmp",
    )(q_rot.reshape(b, 1, NSA_Q_W), kcb, vcb)


def _nsa_sample_attn_kernel(idx_ref, pt_ref, q_ref, sm_ref, oc_ref, blk_ref, newsel_ref, win_ref, newwin_ref,
                            o_ref, wino_ref, m_ref, l_ref, acc_ref, *, npb):
    bi, g, n = pl.program_id(0), pl.program_id(1), pl.program_id(2)
    qs = _stack_heads(q_ref[0].astype(F32), 0)

    def row_scores(k):
        return jnp.sum(qs * k.astype(BF16).astype(F32), axis=-1, keepdims=True) * SCALE

    @pl.when(n == 0)
    def _():
        m_ref[...] = jnp.full_like(m_ref, NEG_INF)
        l_ref[...] = jnp.zeros_like(l_ref)
        acc_ref[...] = jnp.zeros_like(acc_ref)

    def accumulate(s, pv):
        m_new = jnp.maximum(m_ref[...], jnp.max(s, axis=-1, keepdims=True))
        alpha = jnp.exp(m_ref[...] - m_new)
        e = jnp.exp(s - m_new)
        l_ref[...] = alpha * l_ref[...] + jnp.sum(e, axis=-1, keepdims=True)
        acc_ref[...] = alpha * acc_ref[...] + pv(e)
        m_ref[...] = m_new

    in_past = idx_ref[bi, g, n] < npb

    @pl.when(in_past)
    def _():
        accumulate(_dot_nt(qs, blk_ref[0, 0, 0, 0, :, 0:HEAD_DIM]) * SCALE,
                   lambda e: _dot(e, blk_ref[0, 0, 0, 0, :, HEAD_DIM:KV_ROW]))

    @pl.when(jnp.logical_not(in_past))
    def _():
        accumulate(row_scores(newsel_ref[0, 0, :, 0:HEAD_DIM]),
                   lambda e: e * newsel_ref[0, 0, :, HEAD_DIM:KV_ROW].astype(BF16).astype(F32))

    @pl.when(n == pl.num_programs(2) - 1)
    def _():
        o_s = acc_ref[...] / jnp.maximum(l_ref[...], 1e-30)
        s_buf = _dot_nt(qs, win_ref[0, 0, 0, :, 0:HEAD_DIM]) * SCALE
        s_new = row_scores(newwin_ref[0, 0, :, 0:HEAD_DIM])
        m = jnp.maximum(jnp.max(s_buf, axis=-1, keepdims=True), s_new)
        e_buf, e_new = jnp.exp(s_buf - m), jnp.exp(s_new - m)
        den = jnp.maximum(jnp.sum(e_buf, axis=-1, keepdims=True) + e_new, 1e-30)
        o_w = (_dot(e_buf, win_ref[0, 0, 0, :, HEAD_DIM:KV_ROW])
               + e_new * newwin_ref[0, 0, :, HEAD_DIM:KV_ROW].astype(BF16).astype(F32)) / den
        gt = _sigmoid(sm_ref[0])
        for h in range(NSA_HPG):
            c0 = h * HEAD_DIM
            o = (gt[:, 3 * h:3 * h + 1] * oc_ref[0, :, c0:c0 + HEAD_DIM] + gt[:, 3 * h + 1:3 * h + 2] * o_s[h:h + 1]
                 + gt[:, 3 * h + 2:3 * h + 3] * o_w[h:h + 1])
            o_ref[0, :, c0:c0 + HEAD_DIM] = o.astype(BF16)
        wb = win_ref.shape[3]
        wino_ref[0, 0, 0:wb - 1, :] = win_ref[0, 0, 0, 1:wb, :]
        wino_ref[0, 0, wb - 1:wb, :] = newwin_ref[0, 0]


def _nsa_sample_attn(idx, page_table, q_rot, small, o_c, cache_sel, new_sel, cache_win, new_win, l, npb):
    b = q_rot.shape[0]
    n_sel = idx.shape[2]
    bpp = cache_sel.shape[3]
    wb = cache_win.shape[3]
    gw = NSA_HPG * HEAD_DIM

    def blk_map(bi, g, n, idx_ref, pt_ref):
        jp = jnp.minimum(idx_ref[bi, g, n], npb - 1)
        return (pt_ref[bi, jp // bpp], l, g, jp % bpp, 0, 0)

    new_spec = pl.BlockSpec((1, 1, 1, KV_ROW), lambda bi, g, n, *_: (bi, g, 0, 0))
    grid_spec = pltpu.PrefetchScalarGridSpec(
        num_scalar_prefetch=2,
        grid=(b, NSA_KV, n_sel),
        in_specs=[pl.BlockSpec((1, 1, gw), lambda bi, g, n, *_: (bi, 0, g)),
                  pl.BlockSpec((1, 1, HEAD_DIM), lambda bi, g, n, *_: (bi, 0, g)),
                  pl.BlockSpec((1, 1, gw), lambda bi, g, n, *_: (bi, 0, g)),
                  pl.BlockSpec((1, 1, 1, 1, BLOCK, KV_ROW), blk_map),
                  new_spec,
                  pl.BlockSpec((1, 1, 1, wb, KV_ROW), lambda bi, g, n, *_: (bi, l, g, 0, 0)),
                  new_spec],
        out_specs=(pl.BlockSpec((1, 1, gw), lambda bi, g, n, *_: (bi, 0, g)),
                   pl.BlockSpec((1, 1, wb, KV_ROW), lambda bi, g, n, *_: (bi, g, 0, 0))),
        scratch_shapes=[pltpu.VMEM((2 * NSA_HPG, 1), F32), pltpu.VMEM((2 * NSA_HPG, 1), F32),
                        pltpu.VMEM((2 * NSA_HPG, HEAD_DIM), F32)],
    )
    return pl.pallas_call(
        functools.partial(_nsa_sample_attn_kernel, npb=npb),
        out_shape=(jax.ShapeDtypeStruct((b, 1, NSA_Q_W), BF16),
                   jax.ShapeDtypeStruct((b, NSA_KV, wb, KV_ROW), F32)),
        grid_spec=grid_spec,
        compiler_params=_cparams(("arbitrary", "arbitrary", "arbitrary")),
        name="nsa_sample_attn",
    )(idx, page_table, q_rot.reshape(b, 1, NSA_Q_W), small.reshape(b, 1, SMALL_W), o_c,
      cache_sel, new_sel, cache_win, new_win)


def _gdn_sample_kernel(x_ref, z_ref, sm_ref, buf_ref, s0_ref, cw_ref, prow_ref, nw_ref, eye_ref,
                       o_ref, s_out_ref, conv_out_ref):
    x = x_ref[0]
    buf = buf_ref[0, 0]
    y = x * cw_ref[CONV_W - 1:CONV_W, :]
    for w in range(CONV_W - 1):
        y = y + buf[w:w + 1, :] * cw_ref[w:w + 1, :]
    y = _silu(y)
    conv_out_ref[0, 0:CONV_W - 2, :] = buf[1:CONV_W - 1, :]
    conv_out_ref[0, CONV_W - 2:CONV_W - 1, :] = x

    def heads(off):
        return jnp.concatenate([y[:, off + h * HEAD_DIM:off + (h + 1) * HEAD_DIM] for h in range(GDN_HEADS)], axis=0)

    yq, yk, v = heads(0), heads(GDN_W), heads(2 * GDN_W)
    q = yq * lax.rsqrt(jnp.sum(yq * yq, axis=-1, keepdims=True) + EPS) * SCALE
    k = yk * lax.rsqrt(jnp.sum(yk * yk, axis=-1, keepdims=True) + EPS)
    sm = sm_ref[0]
    g_row = -jnp.exp(prow_ref[0:1, :]) * _softplus(sm[:, A_COL:A_COL + GDN_HEADS] + prow_ref[1:2, :])
    beta_row = _sigmoid(sm[:, B_COL:B_COL + GDN_HEADS])
    eg_row = jnp.exp(g_row)
    k_t = _dot_nt_exact(eye_ref[...], k)
    q_t = _dot_nt_exact(eye_ref[...], q)
    z = z_ref[0]
    for h in range(GDN_HEADS):
        s0 = s0_ref[0, 0, h]
        eg = eg_row[:, h:h + 1]
        kc = k_t[:, h:h + 1].astype(BF16).astype(F32)
        qc = q_t[:, h:h + 1].astype(BF16).astype(F32)
        s0b = s0.astype(BF16).astype(F32)
        ks = jnp.sum(kc * beta_row[:, h:h + 1] * eg * s0b, axis=0, keepdims=True)
        v_new = v[h:h + 1] * beta_row[:, h:h + 1] - ks
        vb = v_new.astype(BF16).astype(F32)
        s1 = s0 * eg + kc * vb
        qe = (q_t[:, h:h + 1] * eg).astype(BF16).astype(F32)
        o = jnp.sum(qe * s0b, axis=0, keepdims=True) + jnp.sum(qc * kc, axis=0, keepdims=True) * vb
        s_out_ref[0, h] = s1
        sl = slice(h * HEAD_DIM, (h + 1) * HEAD_DIM)
        o = _rms(o, nw_ref[...]) * _silu(z[:, sl])
        o_ref[0, :, sl] = o.astype(BF16)


def _gdn_sample(proj, small, state_conv, state_gdn, conv_w, a_log, dt_bias, norm_w, l, b):
    prow = jnp.stack([a_log, dt_bias], axis=0)
    rows = proj.shape[0]
    return pl.pallas_call(
        _gdn_sample_kernel,
        out_shape=(jax.ShapeDtypeStruct((b, 1, GDN_W), BF16),
                   jax.ShapeDtypeStruct((b, GDN_HEADS, HEAD_DIM, HEAD_DIM), F32),
                   jax.ShapeDtypeStruct((b, CONV_W - 1, CONV_CH), F32)),
        grid=(b,),
        in_specs=[pl.BlockSpec((1, 1, CONV_CH), lambda bi: (bi, 0, OFF_CONV // CONV_CH)),
                  pl.BlockSpec((1, 1, GDN_W), lambda bi: (bi, 0, OFF_Z // GDN_W)),
                  pl.BlockSpec((1, 1, SMALL_W), lambda bi: (bi, 0, 0)),
                  pl.BlockSpec((1, 1, CONV_W - 1, CONV_CH), lambda bi: (bi, l, 0, 0)),
                  pl.BlockSpec((1, 1, GDN_HEADS, HEAD_DIM, HEAD_DIM), lambda bi: (bi, l, 0, 0, 0)),
                  pl.BlockSpec((CONV_W, CONV_CH), lambda bi: (0, 0)),
                  pl.BlockSpec((2, GDN_HEADS), lambda bi: (0, 0)),
                  pl.BlockSpec((1, HEAD_DIM), lambda bi: (0, 0)),
                  pl.BlockSpec((HEAD_DIM, HEAD_DIM), lambda bi: (0, 0))],
        out_specs=(pl.BlockSpec((1, 1, GDN_W), lambda bi: (bi, 0, 0)),
                   pl.BlockSpec((1, GDN_HEADS, HEAD_DIM, HEAD_DIM), lambda bi: (bi, 0, 0, 0)),
                   pl.BlockSpec((1, CONV_W - 1, CONV_CH), lambda bi: (bi, 0, 0))),
        compiler_params=_cparams(("arbitrary",)),
        name="gdn_sample",
    )(proj.reshape(rows, 1, MAIN_W), proj.reshape(rows, 1, MAIN_W), small.reshape(rows, 1, SMALL_W),
      state_conv, state_gdn, conv_w, prow, norm_w.reshape(1, HEAD_DIM), jnp.eye(HEAD_DIM, dtype=F32))


def _repack_w_in(w_in):
    kvw = NSA_KV * HEAD_DIM
    o_q = 0
    o_kv = [NSA_Q_W + i * kvw for i in range(6)]
    o_gl = NSA_Q_W + 6 * kvw
    o_conv = o_gl + 3 * NSA_HEADS
    o_a = o_conv + CONV_CH
    o_b = o_a + GDN_HEADS
    o_z = o_b + GDN_HEADS

    def cols(a, n):
        return w_in[:, :, a:a + n]

    pieces = [cols(o_conv, CONV_CH), cols(o_z, GDN_W), cols(o_q, NSA_Q_W)]
    for pair in range(3):
        for g in range(NSA_KV):
            pieces.append(cols(o_kv[2 * pair] + g * HEAD_DIM, HEAD_DIM))
            pieces.append(cols(o_kv[2 * pair + 1] + g * HEAD_DIM, HEAD_DIM))
    main = jnp.concatenate(pieces, axis=-1).astype(BF16)
    depth, d, _ = w_in.shape
    ngl = 3 * NSA_HPG
    small = jnp.concatenate([
        cols(o_gl, ngl), cols(o_a, GDN_HEADS), cols(o_b, GDN_HEADS),
        jnp.zeros((depth, d, HEAD_DIM - ngl - 2 * GDN_HEADS), w_in.dtype),
        cols(o_gl + ngl, ngl), jnp.zeros((depth, d, HEAD_DIM - ngl), w_in.dtype)], axis=-1).astype(BF16)
    return main, small


def _mods(mod_l, lo, hi, broadcast):
    d = mod_l.shape[1] // 6
    out = []
    for k in range(6):
        m = mod_l[lo:hi, k * d:(k + 1) * d]
        out.append(m[:, None, :] if broadcast else m[None])
    return out


def kernel(x_prompt, x_sample, cache_cmp_kv, cache_sel_kv, cache_win_kv, state_gdn, state_conv, page_table,
           c_prompt, c_sample, w_ada, b_ada, g_pre_mix, w_in, cmp_pe, cmp_w1, cmp_w2, conv_w, gdn_a_log,
           gdn_dt_bias, gdn_norm, w_out, g_post_mix, g_pre_ffn, w_gate, w_up, w_down, g_post_ffn):
    bp, t, d = x_prompt.shape
    bs = x_sample.shape[0]
    depth = w_in.shape[0]
    n_pool, _, _, page, _, _ = cache_cmp_kv.shape
    past = page_table.shape[1] * page
    npb = past // BLOCK
    bpp = page // BLOCK
    wb = cache_win_kv.shape[3]
    ts = 16
    assert x_sample.shape[1] == 1 and bs <= ts and t % BLOCK == 0 and page % BLOCK == 0

    c_all = jnp.concatenate([c_sample, jnp.zeros((ts - bs, d), F32), c_prompt,
                             jnp.zeros((-bp % 8, d), F32)], axis=0)
    mod = _ada_mod(c_all, w_ada, b_ada)

    w_main, w_small = _repack_w_in(w_in)
    w_out_b, w_gate_b, w_up_b, w_down_b = (w.astype(BF16) for w in (w_out, w_gate, w_up, w_down))
    rope_p = _rope_tables(jnp.arange(t))
    rope_s = _rope_tables(jnp.full((ts,), past))

    tm = min(512, t)
    xp = x_prompt.reshape(bp * t, d)
    xs = jnp.concatenate([x_sample.reshape(bs, d), jnp.zeros((ts - bs, d), F32)], axis=0)
    cache_cmp_blocks = cache_cmp_kv.reshape(n_pool, depth, NSA_KV * bpp, BLK_ROW)
    cache_sel_blocks = cache_sel_kv.reshape(n_pool, depth, NSA_KV, bpp, BLOCK, KV_ROW)
    cache_win_rows = cache_win_kv.reshape(bs, depth, NSA_KV, wb, KV_ROW)
    nbp = -(-(npb + 1) // HEAD_DIM) * HEAD_DIM

    p_states, s_states = [], []
    for l in range(depth):
        gpm, gpo, gpf, gpof = (g[l][None] for g in (g_pre_mix, g_post_mix, g_pre_ffn, g_post_ffn))
        cw = _compress_weights(cmp_pe, cmp_w1, cmp_w2, l)

        sh1, sc1, ga1, sh2, sc2, ga2 = _mods(mod[l], ts, ts + bp, True)
        proj, small = _in_proj(xp, gpm, sc1, sh1, w_main, w_small, l, t, tm)
        q_rot, cmp_st, sel_st, win_st = _post_proj(proj, rope_p, bp, t, min(256, t))
        nb = t // BLOCK
        xb = cmp_st.reshape(bp * NSA_KV * nb, BLK_ROW)
        kcb, vcb = _compress(xb, cw, math.gcd(xb.shape[0], 128))
        kcb, vcb = (a.reshape(bp, NSA_KV, nb, HEAD_DIM) for a in (kcb, vcb))
        o_nsa = _nsa_prompt(q_rot, small, kcb, vcb, sel_st, win_st, bp, t)
        o_gdn, s_fin, conv_fin = _gdn_prompt(proj, small, conv_w[l], gdn_a_log[l], gdn_dt_bias[l], gdn_norm[l], bp, t)
        xp = _out_proj(o_nsa, o_gdn, w_out_b, xp, ga1, gpo, l, t, tm)
        xp = _ffn(xp, gpf, sc2, sh2, w_gate_b, w_up_b, w_down_b, ga2, gpof, l, t, tm)
        wl = min(WINDOW, t)
        p_states.append((cmp_st.reshape(bp, NSA_KV, t, 2, HEAD_DIM), sel_st.reshape(bp, NSA_KV, t, 2, HEAD_DIM),
                         win_st[:, :, t - wl:].reshape(bp, NSA_KV, wl, 2, HEAD_DIM), s_fin, conv_fin))

        sh1, sc1, ga1, sh2, sc2, ga2 = _mods(mod[l], 0, ts, False)
        proj, small = _in_proj(xs, gpm, sc1, sh1, w_main, w_small, l, ts, ts)
        q_rot, cmp_new, sel_new, win_new = _post_proj(proj, rope_s, 1, ts, ts)
        cmp_new, sel_new, win_new = (a[0, :, :bs].transpose(1, 0, 2)[:, :, None, :] for a in (cmp_new, sel_new, win_new))
        kc_past, vc_past = _compress_paged(cache_cmp_blocks, page_table, cw, l)
        new_blk = jnp.pad(cmp_new.reshape(bs * NSA_KV, KV_ROW), ((0, 0), (0, BLK_ROW - KV_ROW)))
        kc_new, vc_new = _compress(new_blk, cw, bs * NSA_KV)

        def summaries(past_rows, new_rows):
            a = past_rows.reshape(bs, past // page, NSA_KV, bpp, HEAD_DIM).transpose(0, 2, 1, 3, 4)
            a = a.reshape(bs, NSA_KV, npb, HEAD_DIM)
            a = jnp.concatenate([a, new_rows.reshape(bs, NSA_KV, 1, HEAD_DIM)], axis=2)
            return jnp.pad(a, ((0, 0), (0, 0), (0, nbp - npb - 1), (0, 0)))

        kcb, vcb = summaries(kc_past, kc_new), summaries(vc_past, vc_new)
        o_c, idx = _nsa_sample_cmp(q_rot[:bs], kcb, vcb, past, npb + 1)
        o_nsa, win_out = _nsa_sample_attn(idx[..., 0], page_table, q_rot[:bs], small[:bs], o_c, cache_sel_blocks,
                                          sel_new, cache_win_rows, win_new, l, npb)
        o_gdn, s_fin, conv_fin = _gdn_sample(proj, small, state_conv, state_gdn, conv_w[l], gdn_a_log[l],
                                             gdn_dt_bias[l], gdn_norm[l], l, bs)
        pad_rows = ((0, ts - bs), (0, 0))
        o_nsa = jnp.pad(o_nsa.reshape(bs, NSA_Q_W), pad_rows)
        o_gdn = jnp.pad(o_gdn.reshape(bs, GDN_W), pad_rows)
        xs = _out_proj(o_nsa, o_gdn, w_out_b, xs, ga1, gpo, l, ts, ts)
        xs = _ffn(xs, gpf, sc2, sh2, w_gate_b, w_up_b, w_down_b, ga2, gpof, l, ts, ts)
        s_states.append((cmp_new.reshape(bs, NSA_KV, 1, 2, HEAD_DIM), sel_new.reshape(bs, NSA_KV, 1, 2, HEAD_DIM),
                         win_out.reshape(bs, NSA_KV, wb, 2, HEAD_DIM), s_fin, conv_fin))

    p_st = [jnp.stack(s, axis=1) for s in zip(*p_states)]
    s_st = [jnp.stack(s, axis=1) for s in zip(*s_states)]
    return (xp.reshape(bp, t, d), xs[:bs].reshape(bs, 1, d), *p_st, *s_st)
```

```python
import functools
import math

import jax
import jax.numpy as jnp
from jax import lax
from jax.experimental import pallas as pl
from jax.experimental.pallas import tpu as pltpu

F32 = jnp.float32
BF16 = jnp.bfloat16

HEAD_DIM = 128
NSA_HEADS = 8
NSA_KV = 2
NSA_HPG = NSA_HEADS // NSA_KV
GDN_HEADS = 8
ROT_DIM = HEAD_DIM // 4
ROPE_THETA = 500000.0
BLOCK = 64
TOPN = 16
WINDOW = 512
CMP_HID = 256
CONV_W = 4
GDN_CHUNK = 64
EPS = 1e-6
KV_ROW = 2 * HEAD_DIM
BLK_ROWS = 2 * BLOCK
BLK_PITCH = BLK_ROWS + 8
NSA_Q_W = NSA_HEADS * HEAD_DIM
GDN_W = GDN_HEADS * HEAD_DIM
CONV_CH = 3 * GDN_W
SMALL_W = 2 * HEAD_DIM
A_COL = 12
B_COL = 20
OFF_CONV = 0
OFF_Z = CONV_CH
OFF_Q = OFF_Z + GDN_W
OFF_CMP = OFF_Q + NSA_Q_W
OFF_SEL = OFF_CMP + NSA_KV * KV_ROW
OFF_WIN = OFF_SEL + NSA_KV * KV_ROW
MAIN_W = OFF_WIN + NSA_KV * KV_ROW
VMEM_LIMIT = 48 * 1024 * 1024
NEG_INF = float("-inf")
SCALE = HEAD_DIM ** -0.5


def _cparams(sem):
    return pltpu.CompilerParams(dimension_semantics=sem, vmem_limit_bytes=VMEM_LIMIT)


def _dot(a, b):
    return jnp.dot(a.astype(BF16), b.astype(BF16), preferred_element_type=F32)


def _dot_nt(a, b):
    return lax.dot_general(a.astype(BF16), b.astype(BF16), (((1,), (1,)), ((), ())),
                           preferred_element_type=F32)


def _dot_exact(a, b):
    return jnp.dot(a, b, preferred_element_type=F32, precision=lax.Precision.HIGHEST)


def _dot_nt_exact(a, b):
    return lax.dot_general(a, b, (((1,), (1,)), ((), ())), preferred_element_type=F32,
                           precision=lax.Precision.HIGHEST)


def _sigmoid(x):
    return 1.0 / (1.0 + jnp.exp(-x))


def _silu(x):
    return x * _sigmoid(x)


def _rms(x, g):
    return x * lax.rsqrt(jnp.mean(x * x, axis=-1, keepdims=True) + EPS) * g


def _masked_softmax(s, mask):
    s = jnp.where(mask, s, NEG_INF)
    m = jnp.max(s, axis=-1, keepdims=True)
    m = jnp.where(m == NEG_INF, 0.0, m)
    e = jnp.where(mask, jnp.exp(s - m), 0.0)
    return e / jnp.maximum(jnp.sum(e, axis=-1, keepdims=True), 1e-30)


def _ada_kernel(c_ref, w_ref, b_ref, o_ref):
    o_ref[0] = _dot(_silu(c_ref[...]), w_ref[0]) + b_ref[0]


def _ada_mod(c_all, w_ada, b_ada, tn=1024):
    depth, d, n = w_ada.shape
    r = c_all.shape[0]
    return pl.pallas_call(
        _ada_kernel,
        out_shape=jax.ShapeDtypeStruct((depth, r, n), F32),
        grid=(depth, n // tn),
        in_specs=[pl.BlockSpec((r, d), lambda l, j: (0, 0)),
                  pl.BlockSpec((1, d, tn), lambda l, j: (l, 0, j)),
                  pl.BlockSpec((1, 1, tn), lambda l, j: (l, 0, j))],
        out_specs=pl.BlockSpec((1, r, tn), lambda l, j: (l, 0, j)),
        compiler_params=_cparams(("arbitrary", "arbitrary")),
        name="ada_mod",
    )(c_all, w_ada, b_ada.reshape(depth, 1, n))


def _inproj_kernel(x_ref, g_ref, sc_ref, sh_ref, w_ref, ws_ref, o_ref, os_ref, h_ref):
    @pl.when(pl.program_id(1) == 0)
    def _():
        h = _rms(x_ref[...], g_ref[...]) * (1.0 + sc_ref[0]) + sh_ref[0]
        hb = h.astype(BF16)
        h_ref[...] = hb
        os_ref[...] = jnp.dot(hb, ws_ref[0], preferred_element_type=F32)

    o_ref[...] = jnp.dot(h_ref[...], w_ref[0], preferred_element_type=F32)


def _mod_spec(mod, rows_per_mod, tm):
    d = mod.shape[-1]
    if mod.shape[1] == 1:
        return pl.BlockSpec((1, 1, d), lambda i, *_: ((i * tm) // rows_per_mod, 0, 0))
    return pl.BlockSpec((1, tm, d), lambda i, *_: (0, i, 0))


def _in_proj(x, g, sc, sh, w_main, w_small, l, rows_per_mod, tm, tn=512):
    m, d = x.shape
    return pl.pallas_call(
        _inproj_kernel,
        out_shape=(jax.ShapeDtypeStruct((m, MAIN_W), F32), jax.ShapeDtypeStruct((m, SMALL_W), F32)),
        grid=(m // tm, MAIN_W // tn),
        in_specs=[pl.BlockSpec((tm, d), lambda i, j: (i, 0)),
                  pl.BlockSpec((1, d), lambda i, j: (0, 0)),
                  _mod_spec(sc, rows_per_mod, tm), _mod_spec(sh, rows_per_mod, tm),
                  pl.BlockSpec((1, d, tn), lambda i, j: (l, 0, j)),
                  pl.BlockSpec((1, d, SMALL_W), lambda i, j: (l, 0, 0))],
        out_specs=(pl.BlockSpec((tm, tn), lambda i, j: (i, j)),
                   pl.BlockSpec((tm, SMALL_W), lambda i, j: (i, 0))),
        scratch_shapes=[pltpu.VMEM((tm, d), BF16)],
        compiler_params=_cparams(("arbitrary", "arbitrary")),
        name="in_proj",
    )(x, g, sc, sh, w_main, w_small)


def _rope(x, c, s1, s2):
    half = ROT_DIM // 2
    return x * c + pltpu.roll(x, HEAD_DIM - half, 1) * s1 + pltpu.roll(x, half, 1) * s2


def _postproj_kernel(q_ref, cmp_ref, sel_ref, win_ref, c_ref, s1_ref, s2_ref,
                     qo_ref, cmpo_ref, selo_ref, wino_ref):
    c, s1, s2 = c_ref[...], s1_ref[...], s2_ref[...]
    for h in range(NSA_HEADS):
        sl = slice(h * HEAD_DIM, (h + 1) * HEAD_DIM)
        qo_ref[:, sl] = _rope(q_ref[:, sl], c, s1, s2).astype(BF16)
    tt = c.shape[0]
    for src, dst in ((cmp_ref, cmpo_ref), (sel_ref, selo_ref), (win_ref, wino_ref)):
        for g in range(NSA_KV):
            base = g * KV_ROW
            dst[0, g, pl.ds(0, tt, stride=2), :] = _rope(src[:, base:base + HEAD_DIM], c, s1, s2)
            dst[0, g, pl.ds(1, tt, stride=2), :] = src[:, base + HEAD_DIM:base + KV_ROW]


def _post_proj(proj, rope_tabs, b, t, tt):
    nt = t // tt
    state = jax.ShapeDtypeStruct((b, NSA_KV, 2 * t, HEAD_DIM), F32)
    kvw = NSA_KV * KV_ROW
    tab_spec = pl.BlockSpec((tt, HEAD_DIM), lambda bi, ti: (ti, 0))
    st_spec = pl.BlockSpec((1, NSA_KV, 2 * tt, HEAD_DIM), lambda bi, ti: (bi, 0, ti, 0))
    return pl.pallas_call(
        _postproj_kernel,
        out_shape=(jax.ShapeDtypeStruct((b * t, NSA_Q_W), BF16), state, state, state),
        grid=(b, nt),
        in_specs=[pl.BlockSpec((tt, NSA_Q_W), lambda bi, ti: (bi * nt + ti, OFF_Q // NSA_Q_W)),
                  pl.BlockSpec((tt, kvw), lambda bi, ti: (bi * nt + ti, OFF_CMP // kvw)),
                  pl.BlockSpec((tt, kvw), lambda bi, ti: (bi * nt + ti, OFF_SEL // kvw)),
                  pl.BlockSpec((tt, kvw), lambda bi, ti: (bi * nt + ti, OFF_WIN // kvw)),
                  tab_spec, tab_spec, tab_spec],
        out_specs=(pl.BlockSpec((tt, NSA_Q_W), lambda bi, ti: (bi * nt + ti, 0)),
                   st_spec, st_spec, st_spec),
        compiler_params=_cparams(("arbitrary", "arbitrary")),
        name="post_proj",
    )(proj, proj, proj, proj, *rope_tabs)


def _rope_tables(pos):
    half = ROT_DIM // 2
    inv = ROPE_THETA ** (-jnp.arange(half, dtype=F32) * 2.0 / ROT_DIM)
    ang = pos.astype(F32)[:, None] * inv
    cos, sin = jnp.cos(ang), jnp.sin(ang)
    n = pos.shape[0]
    rest = HEAD_DIM - ROT_DIM
    c = jnp.concatenate([cos, cos, jnp.ones((n, rest), F32)], axis=1)
    s1 = jnp.concatenate([-sin, jnp.zeros((n, HEAD_DIM - half), F32)], axis=1)
    s2 = jnp.concatenate([jnp.zeros((n, half), F32), sin, jnp.zeros((n, rest), F32)], axis=1)
    return c, s1, s2


def _gelu_tanh(x):
    return 0.5 * x * (1.0 + jnp.tanh(math.sqrt(2.0 / math.pi) * (x + 0.044715 * (x * x * x))))


def _compress_blocks(x_ref, base, pitch, nblk, pe_ref, w1k_ref, w1v_ref, w2k_ref, w2v_ref, ko_ref, vo_ref):
    hk = jnp.zeros((nblk, CMP_HID), F32)
    hv = jnp.zeros((nblk, CMP_HID), F32)
    for r in range(BLOCK):
        xk = x_ref[pl.ds(base + 2 * r, nblk, stride=pitch), :] + pe_ref[0, r:r + 1, :]
        xv = x_ref[pl.ds(base + 2 * r + 1, nblk, stride=pitch), :] + pe_ref[1, r:r + 1, :]
        hk = hk + _dot(xk, w1k_ref[r])
        hv = hv + _dot(xv, w1v_ref[r])
    ko_ref[...] = _dot(_gelu_tanh(hk), w2k_ref[...])
    vo_ref[...] = _dot(_gelu_tanh(hv), w2v_ref[...])


def _compress_kernel(x_ref, *rest):
    _compress_blocks(x_ref, 0, BLK_ROWS, x_ref.shape[0] // BLK_ROWS, *rest)


def _compress_weights(cmp_pe, cmp_w1, cmp_w2, l):
    w1 = cmp_w1[l].astype(BF16).reshape(2, BLOCK, HEAD_DIM, CMP_HID)
    w2 = cmp_w2[l].astype(BF16)
    return cmp_pe[l], w1[0], w1[1], w2[0], w2[1]


def _cw_specs(nargs):
    z = (0,) * 3
    return [pl.BlockSpec((2, BLOCK, HEAD_DIM), lambda *a: z),
            pl.BlockSpec((BLOCK, HEAD_DIM, CMP_HID), lambda *a: z),
            pl.BlockSpec((BLOCK, HEAD_DIM, CMP_HID), lambda *a: z),
            pl.BlockSpec((CMP_HID, HEAD_DIM), lambda *a: (0, 0)),
            pl.BlockSpec((CMP_HID, HEAD_DIM), lambda *a: (0, 0))]


def _compress(x_rows, cw, tr):
    nblocks = x_rows.shape[0] // BLK_ROWS
    out = jax.ShapeDtypeStruct((nblocks, HEAD_DIM), F32)
    return pl.pallas_call(
        _compress_kernel,
        out_shape=(out, out),
        grid=(nblocks // tr,),
        in_specs=[pl.BlockSpec((tr * BLK_ROWS, HEAD_DIM), lambda i: (i, 0))] + _cw_specs(1),
        out_specs=(pl.BlockSpec((tr, HEAD_DIM), lambda i: (i, 0)),) * 2,
        compiler_params=_cparams(("arbitrary",)),
        name="compress",
    )(x_rows, *cw)


MAX_PAGES_PER_STEP = 32


def _compress_paged_kernel(pt_ref, cache_ref, pe_ref, w1k_ref, w1v_ref, w2k_ref, w2v_ref, ko_ref, vo_ref,
                           xbuf, sem, *, l):
    bpp = cache_ref.shape[2]
    nblk = ko_ref.shape[0]
    pages_per_step = nblk // bpp
    ns = pl.num_programs(1)
    step = pl.program_id(0) * ns + pl.program_id(1)
    total = pl.num_programs(0) * ns

    def copies(st, slot):
        bi, s = st // ns, st % ns
        out = []
        for k in range(pages_per_step):
            pg = pt_ref[bi, s * pages_per_step + k]
            for j in range(bpp):
                row0 = pl.multiple_of(slot * (nblk * BLK_PITCH) + (k * bpp + j) * BLK_PITCH, 8)
                out.append(pltpu.make_async_copy(cache_ref.at[pg, l, j], xbuf.at[pl.ds(row0, BLK_ROWS), :],
                                                 sem.at[slot]))
        return out

    slot = step % 2

    @pl.when(step == 0)
    def _():
        for c in copies(step, slot):
            c.start()

    @pl.when(step + 1 < total)
    def _():
        for c in copies(step + 1, 1 - slot):
            c.start()

    for c in copies(step, slot):
        c.wait()
    _compress_blocks(xbuf, slot * (nblk * BLK_PITCH), BLK_PITCH, nblk,
                     pe_ref, w1k_ref, w1v_ref, w2k_ref, w2v_ref, ko_ref, vo_ref)


def _compress_paged(cache, page_table, cw, l):
    b, n_pages = page_table.shape
    bpp = cache.shape[2]
    pages_per_step = math.gcd(n_pages, MAX_PAGES_PER_STEP)
    steps = n_pages // pages_per_step
    tr = pages_per_step * bpp
    out = jax.ShapeDtypeStruct((b * steps * tr, HEAD_DIM), F32)
    grid_spec = pltpu.PrefetchScalarGridSpec(
        num_scalar_prefetch=1,
        grid=(b, steps),
        in_specs=[pl.BlockSpec(memory_space=pl.ANY)] + _cw_specs(3),
        out_specs=(pl.BlockSpec((tr, HEAD_DIM), lambda bi, s, pt: (bi * steps + s, 0)),) * 2,
        scratch_shapes=[pltpu.VMEM((2 * tr * BLK_PITCH, HEAD_DIM), F32), pltpu.SemaphoreType.DMA((2,))],
    )
    return pl.pallas_call(
        functools.partial(_compress_paged_kernel, l=l),
        out_shape=(out, out),
        grid_spec=grid_spec,
        compiler_params=_cparams(("arbitrary", "arbitrary")),
        name="compress_paged",
    )(page_table, cache, *cw)


def _stable_topn_mask(score, n_sel):
    nb = score.shape[-1]
    jcol = lax.broadcasted_iota(jnp.int32, score.shape, 1)
    rank = jnp.zeros(score.shape, F32)
    for k in range(nb):
        ck = score[:, k:k + 1]
        beats = (ck > score) | ((ck == score) & (jcol > k))
        rank = rank + jnp.where(beats, 1.0, 0.0)
    return jnp.where(rank < n_sel, 1.0, 0.0)


MASK_BIAS = -1e30


def _softmax_pv(parts):
    m = jnp.max(parts[0][0], axis=-1, keepdims=True)
    for s, _ in parts[1:]:
        m = jnp.maximum(m, jnp.max(s, axis=-1, keepdims=True))
    den, acc = 0.0, 0.0
    for s, v in parts:
        e = jnp.exp(s - m)
        den = den + jnp.sum(e, axis=-1, keepdims=True)
        acc = acc + _dot(e, v)
    return acc / den


def _nsa_prompt_kernel(q_ref, sm_ref, kcb_ref, vcb_ref, sel_ref, win_ref, exp_ref, o_ref, os_ref, *, n_sel, wl, n_cls):
    i = pl.program_id(2)
    t_all = sel_ref.shape[2] // 2
    nb = kcb_ref.shape[2]
    rows = NSA_HPG * BLOCK
    q = q_ref[...]
    qs = jnp.concatenate([q[:, h * HEAD_DIM:(h + 1) * HEAD_DIM] for h in range(NSA_HPG)], axis=0)
    q0 = i * BLOCK
    tq = lax.broadcasted_iota(jnp.int32, (rows, 1), 0) & (BLOCK - 1)
    qpos = q0 + tq

    jc = lax.broadcasted_iota(jnp.int32, (rows, nb), 1)
    cmask = (jc * BLOCK + (BLOCK - 1)) <= qpos
    p_c = _masked_softmax(_dot_nt(qs, kcb_ref[0, 0]) * SCALE, cmask)
    o_c = _dot(p_c, vcb_ref[0, 0])

    imp = p_c[0:BLOCK]
    for h in range(1, NSA_HPG):
        imp = imp + p_c[h * BLOCK:(h + 1) * BLOCK]
    jb = lax.broadcasted_iota(jnp.int32, (BLOCK, nb), 1)
    forced = (jb == 0) | (jb == i) | (jb == i - 1)
    score = jnp.where(forced, jnp.inf, jnp.where(jb > i, NEG_INF, imp))
    sel = _stable_topn_mask(score, n_sel)
    sel_before = jnp.where(jb < i, sel, 0.0).astype(BF16)
    row0 = pl.multiple_of(2 * q0, 2 * BLOCK)
    tl = lax.broadcasted_iota(jnp.int32, (rows, BLOCK), 1)
    s_d = (_dot_nt(qs, sel_ref[0, 0, pl.ds(row0, BLOCK, stride=2), :]) * SCALE
           + jnp.where(tl <= tq, 0.0, MASK_BIAS))
    v_d = sel_ref[0, 0, pl.ds(row0 + 1, BLOCK, stride=2), :]
    per_cls = nb // n_cls
    for c in range(n_cls):
        nk = (c + 1) * per_cls * BLOCK

        @pl.when((i >= c * per_cls) & (i < (c + 1) * per_cls))
        def _():
            selk = jnp.dot(sel_before, exp_ref[:, 0:nk], preferred_element_type=F32)
            bias = jnp.concatenate([(1.0 - selk) * MASK_BIAS] * NSA_HPG, axis=0)
            s_b = _dot_nt(qs, sel_ref[0, 0, pl.ds(0, nk, stride=2), :]) * SCALE + bias
            os_ref[...] = _softmax_pv([(s_b, sel_ref[0, 0, pl.ds(1, nk, stride=2), :]), (s_d, v_d)])

    o_s = os_ref[...]

    start = pl.multiple_of(jnp.clip(q0 - WINDOW, 0, t_all - wl), BLOCK)
    wpos = start + lax.broadcasted_iota(jnp.int32, (rows, wl), 1)
    wbias = jnp.where((wpos <= qpos) & (wpos >= qpos - WINDOW), 0.0, MASK_BIAS)
    s_w = _dot_nt(qs, win_ref[0, 0, pl.ds(2 * start, wl, stride=2), :]) * SCALE + wbias
    o_w = _softmax_pv([(s_w, win_ref[0, 0, pl.ds(2 * start + 1, wl, stride=2), :])])

    gt = _sigmoid(sm_ref[...])
    for h in range(NSA_HPG):
        r = slice(h * BLOCK, (h + 1) * BLOCK)
        o = (gt[:, 3 * h:3 * h + 1] * o_c[r] + gt[:, 3 * h + 1:3 * h + 2] * o_s[r]
             + gt[:, 3 * h + 2:3 * h + 3] * o_w[r])
        o_ref[:, h * HEAD_DIM:(h + 1) * HEAD_DIM] = o.astype(BF16)


def _nsa_prompt(q_rot, small, kcb, vcb, sel_st, win_st, b, t):
    nq = t // BLOCK
    nb = kcb.shape[2]
    gw = NSA_HPG * HEAD_DIM
    wl = min(WINDOW + BLOCK, t)
    expand = (jnp.arange(t)[None, :] // BLOCK == jnp.arange(nb)[:, None]).astype(BF16)
    st_spec = pl.BlockSpec((1, 1, 2 * t, HEAD_DIM), lambda bi, g, i: (bi, g, 0, 0))
    cb_spec = pl.BlockSpec((1, 1, nb, HEAD_DIM), lambda bi, g, i: (bi, g, 0, 0))
    return pl.pallas_call(
        functools.partial(_nsa_prompt_kernel, n_sel=min(TOPN, nb), wl=wl, n_cls=math.gcd(nb, 4)),
        out_shape=jax.ShapeDtypeStruct((b * t, NSA_Q_W), BF16),
        grid=(b, NSA_KV, nq),
        in_specs=[pl.BlockSpec((BLOCK, gw), lambda bi, g, i: (bi * nq + i, g)),
                  pl.BlockSpec((BLOCK, HEAD_DIM), lambda bi, g, i: (bi * nq + i, g)),
                  cb_spec, cb_spec, st_spec, st_spec,
                  pl.BlockSpec((nb, t), lambda bi, g, i: (0, 0))],
        out_specs=pl.BlockSpec((BLOCK, gw), lambda bi, g, i: (bi * nq + i, g)),
        scratch_shapes=[pltpu.VMEM((NSA_HPG * BLOCK, HEAD_DIM), F32)],
        compiler_params=_cparams(("arbitrary", "arbitrary", "arbitrary")),
        name="nsa_prompt",
    )(q_rot, small, kcb, vcb, sel_st, win_st, expand)


def _softplus(x):
    return jnp.maximum(x, 0.0) + jnp.log(1.0 + jnp.exp(-jnp.abs(x)))


def _unit_lower_solve(l_strict, rhs):
    n = l_strict[0].shape[0]
    sub = 8
    nt = n // sub
    tiles = [[r[t * sub:(t + 1) * sub, :] for t in range(nt)] for r in rhs]
    for i in range(n - 1):
        t0, s = divmod(i, sub)
        for p, lm in enumerate(l_strict):
            xi = tiles[p][t0][s:s + 1, :]
            col = lm[:, i:i + 1]
            for t in range(t0, nt):
                tiles[p][t] = tiles[p][t] - col[t * sub:(t + 1) * sub, :] * xi
    return [jnp.concatenate(tp, axis=0) for tp in tiles]


def _gdn_prompt_kernel(x_ref, z_ref, sm_ref, abr_ref, cw_ref, prow_ref, pcol_ref, nw_ref,
                       o_ref, s_out_ref, conv_out_ref, s_ref, xe_ref):
    n = pl.program_id(1)
    c = GDN_CHUNK
    pad = 8

    @pl.when(n == 0)
    def _():
        s_ref[...] = jnp.zeros_like(s_ref)
        xe_ref[0:pad, :] = jnp.zeros((pad, CONV_CH), F32)

    xe_ref[pad:pad + c, :] = x_ref[...]
    y = xe_ref[pl.ds(pad, c), :] * cw_ref[CONV_W - 1:CONV_W, :]
    for w in range(CONV_W - 1):
        y = y + xe_ref[pl.ds(pad - (CONV_W - 1) + w, c), :] * cw_ref[w:w + 1, :]
    y = _silu(y)

    ri = lax.broadcasted_iota(jnp.int32, (c, c), 0)
    ci = lax.broadcasted_iota(jnp.int32, (c, c), 1)
    tri = ri >= ci
    strict = ri > ci
    tri_f = jnp.where(tri, 1.0, 0.0)

    a_col = sm_ref[:, A_COL:A_COL + GDN_HEADS]
    b_col = sm_ref[:, B_COL:B_COL + GDN_HEADS]
    g_col = -jnp.exp(prow_ref[0:1, :]) * _softplus(a_col + prow_ref[1:2, :])
    g_row = -jnp.exp(pcol_ref[:, 0:1]) * _softplus(abr_ref[0, 0, 0:GDN_HEADS, :] + pcol_ref[:, 1:2])
    gc_col = _dot_exact(tri_f, g_col)
    gc_row = _dot_nt_exact(g_row, tri_f)
    beta_col = _sigmoid(b_col)

    pre = []
    for h in range(GDN_HEADS):
        sl = slice(h * HEAD_DIM, (h + 1) * HEAD_DIM)
        yq, yk, v = y[:, sl], y[:, GDN_W + h * HEAD_DIM:GDN_W + (h + 1) * HEAD_DIM], \
            y[:, 2 * GDN_W + h * HEAD_DIM:2 * GDN_W + (h + 1) * HEAD_DIM]
        q = yq * lax.rsqrt(jnp.sum(yq * yq, axis=-1, keepdims=True) + EPS) * SCALE
        k = yk * lax.rsqrt(jnp.sum(yk * yk, axis=-1, keepdims=True) + EPS)
        gcc = gc_col[:, h:h + 1]
        beta = beta_col[:, h:h + 1]
        decay = jnp.exp(jnp.where(tri, gcc - gc_row[h:h + 1, :], NEG_INF))
        kb = k * beta
        lmat = jnp.where(strict, _dot_nt(kb, k) * decay, 0.0)
        rhs = jnp.concatenate([v * beta, kb * jnp.exp(gcc)], axis=1)
        pre.append((q, k, gcc, _dot_nt(q, k) * decay, lmat, rhs))

    sols = _unit_lower_solve([p[4] for p in pre], [p[5] for p in pre])

    s_out, o_out = [], []
    for h in range(GDN_HEADS):
        q, k, gcc, qk, _, _ = pre[h]
        u, w = sols[h][:, :HEAD_DIM], sols[h][:, HEAD_DIM:]
        s = s_ref[h]
        v_new = u - _dot(w, s)
        o = _dot(q * jnp.exp(gcc), s) + _dot(qk, v_new)
        g_last = gcc[c - 1:c, :]
        k_dec = k * jnp.exp(g_last - gcc)
        s_out.append(s * jnp.exp(g_last) + lax.dot_general(
            k_dec.astype(BF16), v_new.astype(BF16), (((0,), (0,)), ((), ())), preferred_element_type=F32))
        o_out.append((_rms(o, nw_ref[...]) * _silu(z_ref[:, h * HEAD_DIM:(h + 1) * HEAD_DIM])).astype(BF16))
    for h in range(GDN_HEADS):
        s_ref[h] = s_out[h]
        o_ref[:, h * HEAD_DIM:(h + 1) * HEAD_DIM] = o_out[h]
    xe_ref[0:pad, :] = xe_ref[c:c + pad, :]

    @pl.when(n == pl.num_programs(1) - 1)
    def _():
        s_out_ref[0] = s_ref[...]
        conv_out_ref[0] = xe_ref[pad - (CONV_W - 1):pad, :]


def _gdn_prompt(proj, small, conv_w, a_log, dt_bias, norm_w, b, t):
    c = GDN_CHUNK
    nc = t // c
    ab_row = small.reshape(b, nc, c, SMALL_W)[..., A_COL:A_COL + 2 * GDN_HEADS].transpose(0, 1, 3, 2)
    prow = jnp.stack([a_log, dt_bias], axis=0)
    pcol = prow.T
    return pl.pallas_call(
        _gdn_prompt_kernel,
        out_shape=(jax.ShapeDtypeStruct((b * t, GDN_W), BF16),
                   jax.ShapeDtypeStruct((b, GDN_HEADS, HEAD_DIM, HEAD_DIM), F32),
                   jax.ShapeDtypeStruct((b, CONV_W - 1, CONV_CH), F32)),
        grid=(b, nc),
        in_specs=[pl.BlockSpec((c, CONV_CH), lambda bi, n: (bi * nc + n, OFF_CONV // CONV_CH)),
                  pl.BlockSpec((c, GDN_W), lambda bi, n: (bi * nc + n, OFF_Z // GDN_W)),
                  pl.BlockSpec((c, HEAD_DIM), lambda bi, n: (bi * nc + n, 0)),
                  pl.BlockSpec((1, 1, 2 * GDN_HEADS, c), lambda bi, n: (bi, n, 0, 0)),
                  pl.BlockSpec((CONV_W, CONV_CH), lambda bi, n: (0, 0)),
                  pl.BlockSpec((2, GDN_HEADS), lambda bi, n: (0, 0)),
                  pl.BlockSpec((GDN_HEADS, 2), lambda bi, n: (0, 0)),
                  pl.BlockSpec((1, HEAD_DIM), lambda bi, n: (0, 0))],
        out_specs=(pl.BlockSpec((c, GDN_W), lambda bi, n: (bi * nc + n, 0)),
                   pl.BlockSpec((1, GDN_HEADS, HEAD_DIM, HEAD_DIM), lambda bi, n: (bi, 0, 0, 0)),
                   pl.BlockSpec((1, CONV_W - 1, CONV_CH), lambda bi, n: (bi, 0, 0))),
        scratch_shapes=[pltpu.VMEM((GDN_HEADS, HEAD_DIM, HEAD_DIM), F32),
                        pltpu.VMEM((c + 16, CONV_CH), F32)],
        compiler_params=_cparams(("arbitrary", "arbitrary")),
        name="gdn_prompt",
    )(proj, proj, small, ab_row, conv_w, prow, pcol, norm_w.reshape(1, HEAD_DIM))


def _outproj_kernel(on_ref, og_ref, w_ref, x_ref, ga_ref, g_ref, o_ref):
    o = jnp.concatenate([on_ref[...], og_ref[...]], axis=1)
    mix = jnp.dot(o, w_ref[0], preferred_element_type=F32)
    o_ref[...] = x_ref[...] + ga_ref[0] * _rms(mix, g_ref[...])


def _out_proj(o_nsa, o_gdn, w_out, x, ga, g, l, rows_per_mod, tm):
    m, d = x.shape
    kw = o_nsa.shape[1]
    return pl.pallas_call(
        _outproj_kernel,
        out_shape=jax.ShapeDtypeStruct((m, d), F32),
        grid=(m // tm,),
        in_specs=[pl.BlockSpec((tm, kw), lambda i: (i, 0)),
                  pl.BlockSpec((tm, kw), lambda i: (i, 0)),
                  pl.BlockSpec((1, 2 * kw, d), lambda i: (l, 0, 0)),
                  pl.BlockSpec((tm, d), lambda i: (i, 0)),
                  _mod_spec(ga, rows_per_mod, tm),
                  pl.BlockSpec((1, d), lambda i: (0, 0))],
        out_specs=pl.BlockSpec((tm, d), lambda i: (i, 0)),
        compiler_params=_cparams(("arbitrary",)),
        name="out_proj",
    )(o_nsa, o_gdn, w_out, x, ga, g)


def _ffn_kernel(x_ref, g1_ref, sc_ref, sh_ref, wg_ref, wu_ref, wd_ref, ga_ref, g2_ref, o_ref, h_ref, acc_ref):
    j = pl.program_id(1)

    @pl.when(j == 0)
    def _():
        h = _rms(x_ref[...], g1_ref[...]) * (1.0 + sc_ref[0]) + sh_ref[0]
        h_ref[...] = h.astype(BF16)
        acc_ref[...] = jnp.zeros_like(acc_ref)

    h = h_ref[...]
    a = jnp.dot(h, wg_ref[0], preferred_element_type=F32)
    u = jnp.dot(h, wu_ref[0], preferred_element_type=F32)
    acc_ref[...] += _dot(_silu(a) * u, wd_ref[0])

    @pl.when(j == pl.num_programs(1) - 1)
    def _():
        o_ref[...] = x_ref[...] + ga_ref[0] * _rms(acc_ref[...], g2_ref[...])


def _ffn(x, g1, sc, sh, w_gate, w_up, w_down, ga, g2, l, rows_per_mod, tm, tf=512):
    m, d = x.shape
    f = w_gate.shape[2]
    vec = pl.BlockSpec((1, d), lambda i, j: (0, 0))
    return pl.pallas_call(
        _ffn_kernel,
        out_shape=jax.ShapeDtypeStruct((m, d), F32),
        grid=(m // tm, f // tf),
        in_specs=[pl.BlockSpec((tm, d), lambda i, j: (i, 0)), vec,
                  _mod_spec(sc, rows_per_mod, tm), _mod_spec(sh, rows_per_mod, tm),
                  pl.BlockSpec((1, d, tf), lambda i, j: (l, 0, j)),
                  pl.BlockSpec((1, d, tf), lambda i, j: (l, 0, j)),
                  pl.BlockSpec((1, tf, d), lambda i, j: (l, j, 0)),
                  _mod_spec(ga, rows_per_mod, tm), vec],
        out_specs=pl.BlockSpec((tm, d), lambda i, j: (i, 0)),
        scratch_shapes=[pltpu.VMEM((tm, d), BF16), pltpu.VMEM((tm, d), F32)],
        compiler_params=_cparams(("arbitrary", "arbitrary")),
        name="ffn",
    )(x, g1, sc, sh, w_gate, w_up, w_down, ga, g2)


def _stack_heads(q, h0):
    rows = [q[:, (h0 + h) * HEAD_DIM:(h0 + h + 1) * HEAD_DIM] for h in range(NSA_HPG)]
    return jnp.concatenate(rows + [jnp.zeros((NSA_HPG, HEAD_DIM), F32)], axis=0)


def _nsa_sample_cmp_kernel(q_ref, kcb_ref, vcb_ref, oc_ref, idx_ref, *, n_sel, q_pos):
    nbp = kcb_ref.shape[2]
    cur = q_pos // BLOCK
    q = q_ref[0].astype(F32)
    for g in range(NSA_KV):
        qs = _stack_heads(q, g * NSA_HPG)
        jc = lax.broadcasted_iota(jnp.int32, (qs.shape[0], nbp), 1)
        cmask = (jc * BLOCK + (BLOCK - 1)) <= q_pos
        p_c = _masked_softmax(_dot_nt(qs, kcb_ref[0, g]) * SCALE, cmask)
        o_c = _dot(p_c, vcb_ref[0, g])
        for h in range(NSA_HPG):
            c0 = (g * NSA_HPG + h) * HEAD_DIM
            oc_ref[0, :, c0:c0 + HEAD_DIM] = o_c[h:h + 1]
        imp = p_c[0:1]
        for h in range(1, NSA_HPG):
            imp = imp + p_c[h:h + 1]
        jb = jc[0:1]
        forced = (jb == 0) | (jb == cur) | (jb == cur - 1)
        score = jnp.where(forced, jnp.inf, jnp.where(jb > cur, NEG_INF, imp))
        left = jb >= 0
        for r in range(n_sel):
            best = jnp.max(jnp.where(left, score, NEG_INF), axis=1, keepdims=True)
            pick = jnp.min(jnp.where(left & (score == best), jb, nbp), axis=1, keepdims=True)
            idx_ref[0, g, r:r + 1, :] = jnp.broadcast_to(pick, (1, HEAD_DIM))
            left = left & (jb != pick)


def _nsa_sample_cmp(q_rot, kcb, vcb, q_pos, n_blocks):
    b = q_rot.shape[0]
    nbp = kcb.shape[2]
    n_sel = min(TOPN, n_blocks)
    cb_spec = pl.BlockSpec((1, NSA_KV, nbp, HEAD_DIM), lambda bi: (bi, 0, 0, 0))
    return pl.pallas_call(
        functools.partial(_nsa_sample_cmp_kernel, n_sel=n_sel, q_pos=q_pos),
        out_shape=(jax.ShapeDtypeStruct((b, 1, NSA_Q_W), F32),
                   jax.ShapeDtypeStruct((b, NSA_KV, n_sel, HEAD_DIM), jnp.int32)),
        grid=(b,),
        in_specs=[pl.BlockSpec((1, 1, NSA_Q_W), lambda bi: (bi, 0, 0)), cb_spec, cb_spec],
        out_specs=(pl.BlockSpec((1, 1, NSA_Q_W), lambda bi: (bi, 0, 0)),
                   pl.BlockSpec((1, NSA_KV, n_sel, HEAD_DIM), lambda bi: (bi, 0, 0, 0))),
        compiler_params=_cparams(("arbitrary",)),
        name="nsa_sample_cmp",
    )(q_rot.reshape(b, 1, NSA_Q_W), kcb, vcb)


def _nsa_sample_attn_kernel(idx_ref, pt_ref, q_ref, sm_ref, oc_ref, blk_ref, newsel_ref, win_ref, newwin_ref,
                            o_ref, wino_ref, m_ref, l_ref, acc_ref, *, npb):
    bi, g, n = pl.program_id(0), pl.program_id(1), pl.program_id(2)
    qs = _stack_heads(q_ref[0].astype(F32), 0)

    def row_scores(k):
        return jnp.sum(qs * k.astype(BF16).astype(F32), axis=-1, keepdims=True) * SCALE

    @pl.when(n == 0)
    def _():
        m_ref[...] = jnp.full_like(m_ref, NEG_INF)
        l_ref[...] = jnp.zeros_like(l_ref)
        acc_ref[...] = jnp.zeros_like(acc_ref)

    def accumulate(s, pv):
        m_new = jnp.maximum(m_ref[...], jnp.max(s, axis=-1, keepdims=True))
        alpha = jnp.exp(m_ref[...] - m_new)
        e = jnp.exp(s - m_new)
        l_ref[...] = alpha * l_ref[...] + jnp.sum(e, axis=-1, keepdims=True)
        acc_ref[...] = alpha * acc_ref[...] + pv(e)
        m_ref[...] = m_new

    in_past = idx_ref[bi, g, n] < npb

    @pl.when(in_past)
    def _():
        accumulate(_dot_nt(qs, blk_ref[0, 0, 0, 0, pl.ds(0, BLOCK, stride=2), :]) * SCALE,
                   lambda e: _dot(e, blk_ref[0, 0, 0, 0, pl.ds(1, BLOCK, stride=2), :]))

    @pl.when(jnp.logical_not(in_past))
    def _():
        accumulate(row_scores(newsel_ref[0, 0, 0:1, :]),
                   lambda e: e * newsel_ref[0, 0, 1:2, :].astype(BF16).astype(F32))

    @pl.when(n == pl.num_programs(2) - 1)
    def _():
        wb = win_ref.shape[3] // 2
        o_s = acc_ref[...] / jnp.maximum(l_ref[...], 1e-30)
        s_buf = _dot_nt(qs, win_ref[0, 0, 0, pl.ds(0, wb, stride=2), :]) * SCALE
        s_new = row_scores(newwin_ref[0, 0, 0:1, :])
        m = jnp.maximum(jnp.max(s_buf, axis=-1, keepdims=True), s_new)
        e_buf, e_new = jnp.exp(s_buf - m), jnp.exp(s_new - m)
        den = jnp.maximum(jnp.sum(e_buf, axis=-1, keepdims=True) + e_new, 1e-30)
        o_w = (_dot(e_buf, win_ref[0, 0, 0, pl.ds(1, wb, stride=2), :])
               + e_new * newwin_ref[0, 0, 1:2, :].astype(BF16).astype(F32)) / den
        gt = _sigmoid(sm_ref[0])
        for h in range(NSA_HPG):
            c0 = h * HEAD_DIM
            o = (gt[:, 3 * h:3 * h + 1] * oc_ref[0, :, c0:c0 + HEAD_DIM] + gt[:, 3 * h + 1:3 * h + 2] * o_s[h:h + 1]
                 + gt[:, 3 * h + 2:3 * h + 3] * o_w[h:h + 1])
            o_ref[0, :, c0:c0 + HEAD_DIM] = o.astype(BF16)
        wino_ref[0, 0, 0:2 * wb - 2, :] = win_ref[0, 0, 0, 2:2 * wb, :]
        wino_ref[0, 0, 2 * wb - 2:2 * wb, :] = newwin_ref[0, 0]


def _nsa_sample_attn(idx, page_table, q_rot, small, o_c, cache_sel, new_sel, cache_win, new_win, l, npb):
    b = q_rot.shape[0]
    n_sel = idx.shape[2]
    bpp = cache_sel.shape[3]
    wb2 = cache_win.shape[3]
    gw = NSA_HPG * HEAD_DIM

    def blk_map(bi, g, n, idx_ref, pt_ref):
        jp = jnp.minimum(idx_ref[bi, g, n], npb - 1)
        return (pt_ref[bi, jp // bpp], l, g, jp % bpp, 0, 0)

    new_spec = pl.BlockSpec((1, 1, 2, HEAD_DIM), lambda bi, g, n, *_: (bi, g, 0, 0))
    grid_spec = pltpu.PrefetchScalarGridSpec(
        num_scalar_prefetch=2,
        grid=(b, NSA_KV, n_sel),
        in_specs=[pl.BlockSpec((1, 1, gw), lambda bi, g, n, *_: (bi, 0, g)),
                  pl.BlockSpec((1, 1, HEAD_DIM), lambda bi, g, n, *_: (bi, 0, g)),
                  pl.BlockSpec((1, 1, gw), lambda bi, g, n, *_: (bi, 0, g)),
                  pl.BlockSpec((1, 1, 1, 1, BLK_ROWS, HEAD_DIM), blk_map),
                  new_spec,
                  pl.BlockSpec((1, 1, 1, wb2, HEAD_DIM), lambda bi, g, n, *_: (bi, l, g, 0, 0)),
                  new_spec],
        out_specs=(pl.BlockSpec((1, 1, gw), lambda bi, g, n, *_: (bi, 0, g)),
                   pl.BlockSpec((1, 1, wb2, HEAD_DIM), lambda bi, g, n, *_: (bi, g, 0, 0))),
        scratch_shapes=[pltpu.VMEM((2 * NSA_HPG, 1), F32), pltpu.VMEM((2 * NSA_HPG, 1), F32),
                        pltpu.VMEM((2 * NSA_HPG, HEAD_DIM), F32)],
    )
    return pl.pallas_call(
        functools.partial(_nsa_sample_attn_kernel, npb=npb),
        out_shape=(jax.ShapeDtypeStruct((b, 1, NSA_Q_W), BF16),
                   jax.ShapeDtypeStruct((b, NSA_KV, wb2, HEAD_DIM), F32)),
        grid_spec=grid_spec,
        compiler_params=_cparams(("arbitrary", "arbitrary", "arbitrary")),
        name="nsa_sample_attn",
    )(idx, page_table, q_rot.reshape(b, 1, NSA_Q_W), small.reshape(b, 1, SMALL_W), o_c,
      cache_sel, new_sel, cache_win, new_win)


def _gdn_sample_kernel(x_ref, z_ref, sm_ref, buf_ref, s0_ref, cw_ref, prow_ref, nw_ref, eye_ref,
                       o_ref, s_out_ref, conv_out_ref):
    x = x_ref[0]
    buf = buf_ref[0, 0]
    y = x * cw_ref[CONV_W - 1:CONV_W, :]
    for w in range(CONV_W - 1):
        y = y + buf[w:w + 1, :] * cw_ref[w:w + 1, :]
    y = _silu(y)
    conv_out_ref[0, 0:CONV_W - 2, :] = buf[1:CONV_W - 1, :]
    conv_out_ref[0, CONV_W - 2:CONV_W - 1, :] = x

    def heads(off):
        return jnp.concatenate([y[:, off + h * HEAD_DIM:off + (h + 1) * HEAD_DIM] for h in range(GDN_HEADS)], axis=0)

    yq, yk, v = heads(0), heads(GDN_W), heads(2 * GDN_W)
    q = yq * lax.rsqrt(jnp.sum(yq * yq, axis=-1, keepdims=True) + EPS) * SCALE
    k = yk * lax.rsqrt(jnp.sum(yk * yk, axis=-1, keepdims=True) + EPS)
    sm = sm_ref[0]
    g_row = -jnp.exp(prow_ref[0:1, :]) * _softplus(sm[:, A_COL:A_COL + GDN_HEADS] + prow_ref[1:2, :])
    beta_row = _sigmoid(sm[:, B_COL:B_COL + GDN_HEADS])
    eg_row = jnp.exp(g_row)
    k_t = _dot_nt_exact(eye_ref[...], k)
    q_t = _dot_nt_exact(eye_ref[...], q)
    z = z_ref[0]
    for h in range(GDN_HEADS):
        s0 = s0_ref[0, 0, h]
        eg = eg_row[:, h:h + 1]
        kc = k_t[:, h:h + 1].astype(BF16).astype(F32)
        qc = q_t[:, h:h + 1].astype(BF16).astype(F32)
        s0b = s0.astype(BF16).astype(F32)
        ks = jnp.sum(kc * beta_row[:, h:h + 1] * eg * s0b, axis=0, keepdims=True)
        v_new = v[h:h + 1] * beta_row[:, h:h + 1] - ks
        vb = v_new.astype(BF16).astype(F32)
        s1 = s0 * eg + kc * vb
        qe = (q_t[:, h:h + 1] * eg).astype(BF16).astype(F32)
        o = jnp.sum(qe * s0b, axis=0, keepdims=True) + jnp.sum(qc * kc, axis=0, keepdims=True) * vb
        s_out_ref[0, h] = s1
        sl = slice(h * HEAD_DIM, (h + 1) * HEAD_DIM)
        o = _rms(o, nw_ref[...]) * _silu(z[:, sl])
        o_ref[0, :, sl] = o.astype(BF16)


def _gdn_sample(proj, small, state_conv, state_gdn, conv_w, a_log, dt_bias, norm_w, l, b):
    prow = jnp.stack([a_log, dt_bias], axis=0)
    rows = proj.shape[0]
    return pl.pallas_call(
        _gdn_sample_kernel,
        out_shape=(jax.ShapeDtypeStruct((b, 1, GDN_W), BF16),
                   jax.ShapeDtypeStruct((b, GDN_HEADS, HEAD_DIM, HEAD_DIM), F32),
                   jax.ShapeDtypeStruct((b, CONV_W - 1, CONV_CH), F32)),
        grid=(b,),
        in_specs=[pl.BlockSpec((1, 1, CONV_CH), lambda bi: (bi, 0, OFF_CONV // CONV_CH)),
                  pl.BlockSpec((1, 1, GDN_W), lambda bi: (bi, 0, OFF_Z // GDN_W)),
                  pl.BlockSpec((1, 1, SMALL_W), lambda bi: (bi, 0, 0)),
                  pl.BlockSpec((1, 1, CONV_W - 1, CONV_CH), lambda bi: (bi, l, 0, 0)),
                  pl.BlockSpec((1, 1, GDN_HEADS, HEAD_DIM, HEAD_DIM), lambda bi: (bi, l, 0, 0, 0)),
                  pl.BlockSpec((CONV_W, CONV_CH), lambda bi: (0, 0)),
                  pl.BlockSpec((2, GDN_HEADS), lambda bi: (0, 0)),
                  pl.BlockSpec((1, HEAD_DIM), lambda bi: (0, 0)),
                  pl.BlockSpec((HEAD_DIM, HEAD_DIM), lambda bi: (0, 0))],
        out_specs=(pl.BlockSpec((1, 1, GDN_W), lambda bi: (bi, 0, 0)),
                   pl.BlockSpec((1, GDN_HEADS, HEAD_DIM, HEAD_DIM), lambda bi: (bi, 0, 0, 0)),
                   pl.BlockSpec((1, CONV_W - 1, CONV_CH), lambda bi: (bi, 0, 0))),
        compiler_params=_cparams(("arbitrary",)),
        name="gdn_sample",
    )(proj.reshape(rows, 1, MAIN_W), proj.reshape(rows, 1, MAIN_W), small.reshape(rows, 1, SMALL_W),
      state_conv, state_gdn, conv_w, prow, norm_w.reshape(1, HEAD_DIM), jnp.eye(HEAD_DIM, dtype=F32))


def _repack_w_in(w_in):
    kvw = NSA_KV * HEAD_DIM
    o_q = 0
    o_kv = [NSA_Q_W + i * kvw for i in range(6)]
    o_gl = NSA_Q_W + 6 * kvw
    o_conv = o_gl + 3 * NSA_HEADS
    o_a = o_conv + CONV_CH
    o_b = o_a + GDN_HEADS
    o_z = o_b + GDN_HEADS

    def cols(a, n):
        return w_in[:, :, a:a + n]

    pieces = [cols(o_conv, CONV_CH), cols(o_z, GDN_W), cols(o_q, NSA_Q_W)]
    for pair in range(3):
        for g in range(NSA_KV):
            pieces.append(cols(o_kv[2 * pair] + g * HEAD_DIM, HEAD_DIM))
            pieces.append(cols(o_kv[2 * pair + 1] + g * HEAD_DIM, HEAD_DIM))
    main = jnp.concatenate(pieces, axis=-1).astype(BF16)
    depth, d, _ = w_in.shape
    ngl = 3 * NSA_HPG
    small = jnp.concatenate([
        cols(o_gl, ngl), cols(o_a, GDN_HEADS), cols(o_b, GDN_HEADS),
        jnp.zeros((depth, d, HEAD_DIM - ngl - 2 * GDN_HEADS), w_in.dtype),
        cols(o_gl + ngl, ngl), jnp.zeros((depth, d, HEAD_DIM - ngl), w_in.dtype)], axis=-1).astype(BF16)
    return main, small


def _mods(mod_l, lo, hi, broadcast):
    d = mod_l.shape[1] // 6
    out = []
    for k in range(6):
        m = mod_l[lo:hi, k * d:(k + 1) * d]
        out.append(m[:, None, :] if broadcast else m[None])
    return out


def kernel(x_prompt, x_sample, cache_cmp_kv, cache_sel_kv, cache_win_kv, state_gdn, state_conv, page_table,
           c_prompt, c_sample, w_ada, b_ada, g_pre_mix, w_in, cmp_pe, cmp_w1, cmp_w2, conv_w, gdn_a_log,
           gdn_dt_bias, gdn_norm, w_out, g_post_mix, g_pre_ffn, w_gate, w_up, w_down, g_post_ffn):
    bp, t, d = x_prompt.shape
    bs = x_sample.shape[0]
    depth = w_in.shape[0]
    n_pool, _, _, page, _, _ = cache_cmp_kv.shape
    past = page_table.shape[1] * page
    npb = past // BLOCK
    bpp = page // BLOCK
    wb = cache_win_kv.shape[3]
    ts = 16
    assert x_sample.shape[1] == 1 and bs <= ts and t % BLOCK == 0 and page % BLOCK == 0

    c_all = jnp.concatenate([c_sample, jnp.zeros((ts - bs, d), F32), c_prompt,
                             jnp.zeros((-bp % 8, d), F32)], axis=0)
    mod = _ada_mod(c_all, w_ada, b_ada)

    w_main, w_small = _repack_w_in(w_in)
    w_out_b, w_gate_b, w_up_b, w_down_b = (w.astype(BF16) for w in (w_out, w_gate, w_up, w_down))
    rope_p = _rope_tables(jnp.arange(t))
    rope_s = _rope_tables(jnp.full((ts,), past))

    tm = min(512, t)
    xp = x_prompt.reshape(bp * t, d)
    xs = jnp.concatenate([x_sample.reshape(bs, d), jnp.zeros((ts - bs, d), F32)], axis=0)
    cache_cmp_blocks = cache_cmp_kv.reshape(n_pool, depth, NSA_KV * bpp, BLK_ROWS, HEAD_DIM)
    cache_sel_blocks = cache_sel_kv.reshape(n_pool, depth, NSA_KV, bpp, BLK_ROWS, HEAD_DIM)
    cache_win_rows = cache_win_kv.reshape(bs, depth, NSA_KV, 2 * wb, HEAD_DIM)
    nbp = -(-(npb + 1) // HEAD_DIM) * HEAD_DIM

    p_states, s_states = [], []
    for l in range(depth):
        gpm, gpo, gpf, gpof = (g[l][None] for g in (g_pre_mix, g_post_mix, g_pre_ffn, g_post_ffn))
        cw = _compress_weights(cmp_pe, cmp_w1, cmp_w2, l)

        sh1, sc1, ga1, sh2, sc2, ga2 = _mods(mod[l], ts, ts + bp, True)
        proj, small = _in_proj(xp, gpm, sc1, sh1, w_main, w_small, l, t, tm)
        q_rot, cmp_st, sel_st, win_st = _post_proj(proj, rope_p, bp, t, min(256, t))
        nb = t // BLOCK
        kcb, vcb = _compress(cmp_st.reshape(bp * NSA_KV * 2 * t, HEAD_DIM), cw, math.gcd(bp * NSA_KV * nb, 128))
        kcb, vcb = (a.reshape(bp, NSA_KV, nb, HEAD_DIM) for a in (kcb, vcb))
        o_nsa = _nsa_prompt(q_rot, small, kcb, vcb, sel_st, win_st, bp, t)
        o_gdn, s_fin, conv_fin = _gdn_prompt(proj, small, conv_w[l], gdn_a_log[l], gdn_dt_bias[l], gdn_norm[l], bp, t)
        xp = _out_proj(o_nsa, o_gdn, w_out_b, xp, ga1, gpo, l, t, tm)
        xp = _ffn(xp, gpf, sc2, sh2, w_gate_b, w_up_b, w_down_b, ga2, gpof, l, t, tm)
        wl = min(WINDOW, t)
        p_states.append((cmp_st.reshape(bp, NSA_KV, t, 2, HEAD_DIM), sel_st.reshape(bp, NSA_KV, t, 2, HEAD_DIM),
                         win_st[:, :, 2 * (t - wl):].reshape(bp, NSA_KV, wl, 2, HEAD_DIM), s_fin, conv_fin))

        sh1, sc1, ga1, sh2, sc2, ga2 = _mods(mod[l], 0, ts, False)
        proj, small = _in_proj(xs, gpm, sc1, sh1, w_main, w_small, l, ts, ts)
        q_rot, cmp_new, sel_new, win_new = _post_proj(proj, rope_s, 1, ts, ts)
        cmp_new, sel_new, win_new = (a.reshape(NSA_KV, ts, 2, HEAD_DIM)[:, :bs].transpose(1, 0, 2, 3)
                                     for a in (cmp_new, sel_new, win_new))
        kc_past, vc_past = _compress_paged(cache_cmp_blocks, page_table, cw, l)
        new_blk = jnp.pad(cmp_new.reshape(bs * NSA_KV, 2, HEAD_DIM), ((0, 0), (0, BLK_ROWS - 2), (0, 0)))
        kc_new, vc_new = _compress(new_blk.reshape(bs * NSA_KV * BLK_ROWS, HEAD_DIM), cw, bs * NSA_KV)

        def summaries(past_rows, new_rows):
            a = past_rows.reshape(bs, past // page, NSA_KV, bpp, HEAD_DIM).transpose(0, 2, 1, 3, 4)
            a = a.reshape(bs, NSA_KV, npb, HEAD_DIM)
            a = jnp.concatenate([a, new_rows.reshape(bs, NSA_KV, 1, HEAD_DIM)], axis=2)
            return jnp.pad(a, ((0, 0), (0, 0), (0, nbp - npb - 1), (0, 0)))

        kcb, vcb = summaries(kc_past, kc_new), summaries(vc_past, vc_new)
        o_c, idx = _nsa_sample_cmp(q_rot[:bs], kcb, vcb, past, npb + 1)
        o_nsa, win_out = _nsa_sample_attn(idx[..., 0], page_table, q_rot[:bs], small[:bs], o_c, cache_sel_blocks,
                                          sel_new, cache_win_rows, win_new, l, npb)
        o_gdn, s_fin, conv_fin = _gdn_sample(proj, small, state_conv, state_gdn, conv_w[l], gdn_a_log[l],
                                             gdn_dt_bias[l], gdn_norm[l], l, bs)
        pad_rows = ((0, ts - bs), (0, 0))
        o_nsa = jnp.pad(o_nsa.reshape(bs, NSA_Q_W), pad_rows)
        o_gdn = jnp.pad(o_gdn.reshape(bs, GDN_W), pad_rows)
        xs = _out_proj(o_nsa, o_gdn, w_out_b, xs, ga1, gpo, l, ts, ts)
        xs = _ffn(xs, gpf, sc2, sh2, w_gate_b, w_up_b, w_down_b, ga2, gpof, l, ts, ts)
        s_states.append((cmp_new.reshape(bs, NSA_KV, 1, 2, HEAD_DIM), sel_new.reshape(bs, NSA_KV, 1, 2, HEAD_DIM),
                         win_out.reshape(bs, NSA_KV, wb, 2, HEAD_DIM), s_fin, conv_fin))

    p_st = [jnp.stack(s, axis=1) for s in zip(*p_states)]
    s_st = [jnp.stack(s, axis=1) for s in zip(*s_states)]
    return (xp.reshape(bp, t, d), xs[:bs].reshape(bs, 1, d), *p_st, *s_st)
```

```python
import functools
import math

import jax
import jax.numpy as jnp
from jax import lax
from jax.experimental import pallas as pl
from jax.experimental.pallas import tpu as pltpu

F32 = jnp.float32
BF16 = jnp.bfloat16

HEAD_DIM = 128
NSA_HEADS = 8
NSA_KV = 2
NSA_HPG = NSA_HEADS // NSA_KV
GDN_HEADS = 8
ROT_DIM = HEAD_DIM // 4
ROPE_THETA = 500000.0
BLOCK = 64
TOPN = 16
WINDOW = 512
CMP_HID = 256
CONV_W = 4
GDN_CHUNK = 64
EPS = 1e-6
KV_ROW = 2 * HEAD_DIM
BLK_ROWS = 2 * BLOCK
BLK_PITCH = BLK_ROWS + 8
NSA_Q_W = NSA_HEADS * HEAD_DIM
GDN_W = GDN_HEADS * HEAD_DIM
CONV_CH = 3 * GDN_W
SMALL_W = 2 * HEAD_DIM
A_COL = 12
B_COL = 20
OFF_CONV = 0
OFF_Z = CONV_CH
OFF_Q = OFF_Z + GDN_W
OFF_CMP = OFF_Q + NSA_Q_W
OFF_SEL = OFF_CMP + NSA_KV * KV_ROW
OFF_WIN = OFF_SEL + NSA_KV * KV_ROW
MAIN_W = OFF_WIN + NSA_KV * KV_ROW
VMEM_LIMIT = 48 * 1024 * 1024
NEG_INF = float("-inf")
SCALE = HEAD_DIM ** -0.5


def _cparams(sem):
    return pltpu.CompilerParams(dimension_semantics=sem, vmem_limit_bytes=VMEM_LIMIT)


def _dot(a, b):
    return jnp.dot(a.astype(BF16), b.astype(BF16), preferred_element_type=F32)


def _dot_nt(a, b):
    return lax.dot_general(a.astype(BF16), b.astype(BF16), (((1,), (1,)), ((), ())),
                           preferred_element_type=F32)


def _dot_exact(a, b):
    return jnp.dot(a, b, preferred_element_type=F32, precision=lax.Precision.HIGHEST)


def _dot_nt_exact(a, b):
    return lax.dot_general(a, b, (((1,), (1,)), ((), ())), preferred_element_type=F32,
                           precision=lax.Precision.HIGHEST)


def _sigmoid(x):
    return 1.0 / (1.0 + jnp.exp(-x))


def _silu(x):
    return x * _sigmoid(x)


def _rms(x, g):
    return x * lax.rsqrt(jnp.mean(x * x, axis=-1, keepdims=True) + EPS) * g


def _masked_softmax(s, mask):
    s = jnp.where(mask, s, NEG_INF)
    m = jnp.max(s, axis=-1, keepdims=True)
    m = jnp.where(m == NEG_INF, 0.0, m)
    e = jnp.where(mask, jnp.exp(s - m), 0.0)
    return e / jnp.maximum(jnp.sum(e, axis=-1, keepdims=True), 1e-30)


def _ada_kernel(c_ref, w_ref, b_ref, o_ref):
    o_ref[0] = _dot(_silu(c_ref[...]), w_ref[0]) + b_ref[0]


def _ada_mod(c_all, w_ada, b_ada, tn=1024):
    depth, d, n = w_ada.shape
    r = c_all.shape[0]
    return pl.pallas_call(
        _ada_kernel,
        out_shape=jax.ShapeDtypeStruct((depth, r, n), F32),
        grid=(depth, n // tn),
        in_specs=[pl.BlockSpec((r, d), lambda l, j: (0, 0)),
                  pl.BlockSpec((1, d, tn), lambda l, j: (l, 0, j)),
                  pl.BlockSpec((1, 1, tn), lambda l, j: (l, 0, j))],
        out_specs=pl.BlockSpec((1, r, tn), lambda l, j: (l, 0, j)),
        compiler_params=_cparams(("arbitrary", "arbitrary")),
        name="ada_mod",
    )(c_all, w_ada, b_ada.reshape(depth, 1, n))


def _inproj_kernel(x_ref, g_ref, sc_ref, sh_ref, w_ref, ws_ref, o_ref, os_ref, h_ref):
    @pl.when(pl.program_id(1) == 0)
    def _():
        h = _rms(x_ref[...], g_ref[...]) * (1.0 + sc_ref[0]) + sh_ref[0]
        hb = h.astype(BF16)
        h_ref[...] = hb
        os_ref[...] = jnp.dot(hb, ws_ref[0], preferred_element_type=F32)

    o_ref[...] = jnp.dot(h_ref[...], w_ref[0], preferred_element_type=F32)


def _mod_spec(mod, rows_per_mod, tm):
    d = mod.shape[-1]
    if mod.shape[1] == 1:
        return pl.BlockSpec((1, 1, d), lambda i, *_: ((i * tm) // rows_per_mod, 0, 0))
    return pl.BlockSpec((1, tm, d), lambda i, *_: (0, i, 0))


def _in_proj(x, g, sc, sh, w_main, w_small, l, rows_per_mod, tm, tn=MAIN_W // 4):
    m, d = x.shape
    return pl.pallas_call(
        _inproj_kernel,
        out_shape=(jax.ShapeDtypeStruct((m, MAIN_W), F32), jax.ShapeDtypeStruct((m, SMALL_W), F32)),
        grid=(m // tm, MAIN_W // tn),
        in_specs=[pl.BlockSpec((tm, d), lambda i, j: (i, 0)),
                  pl.BlockSpec((1, d), lambda i, j: (0, 0)),
                  _mod_spec(sc, rows_per_mod, tm), _mod_spec(sh, rows_per_mod, tm),
                  pl.BlockSpec((1, d, tn), lambda i, j: (l, 0, j)),
                  pl.BlockSpec((1, d, SMALL_W), lambda i, j: (l, 0, 0))],
        out_specs=(pl.BlockSpec((tm, tn), lambda i, j: (i, j)),
                   pl.BlockSpec((tm, SMALL_W), lambda i, j: (i, 0))),
        scratch_shapes=[pltpu.VMEM((tm, d), BF16)],
        compiler_params=_cparams(("arbitrary", "arbitrary")),
        name="in_proj",
    )(x, g, sc, sh, w_main, w_small)


def _rope(x, c, s1, s2):
    half = ROT_DIM // 2
    return x * c + pltpu.roll(x, HEAD_DIM - half, 1) * s1 + pltpu.roll(x, half, 1) * s2


ATT_W = 5 * HEAD_DIM


def _postproj_kernel(q_ref, cmp_ref, sel_ref, win_ref, c_ref, s1_ref, s2_ref,
                     qo_ref, cmpo_ref, selo_ref, wino_ref, att_ref):
    c, s1, s2 = c_ref[...], s1_ref[...], s2_ref[...]
    for h in range(NSA_HEADS):
        sl = slice(h * HEAD_DIM, (h + 1) * HEAD_DIM)
        qo_ref[:, sl] = _rope(q_ref[:, sl], c, s1, s2).astype(BF16)
    tt = c.shape[0]
    blk = (pl.program_id(1) * tt + lax.broadcasted_iota(jnp.int32, (tt, HEAD_DIM), 0)) // BLOCK
    onehot = jnp.where(lax.broadcasted_iota(jnp.int32, (tt, HEAD_DIM), 1) == blk, 1.0, 0.0).astype(BF16)
    for src, dst, col in ((cmp_ref, cmpo_ref, None), (sel_ref, selo_ref, 0), (win_ref, wino_ref, 3 * HEAD_DIM)):
        for g in range(NSA_KV):
            base = g * KV_ROW
            k = _rope(src[:, base:base + HEAD_DIM], c, s1, s2)
            v = src[:, base + HEAD_DIM:base + KV_ROW]
            dst[0, g, pl.ds(0, tt, stride=2), :] = k
            dst[0, g, pl.ds(1, tt, stride=2), :] = v
            if col == 0:
                att_ref[0, g, :, 0:HEAD_DIM] = k.astype(BF16)
                att_ref[0, g, :, HEAD_DIM:2 * HEAD_DIM] = onehot
                att_ref[0, g, :, 2 * HEAD_DIM:3 * HEAD_DIM] = v.astype(BF16)
            elif col is not None:
                att_ref[0, g, :, col:col + HEAD_DIM] = k.astype(BF16)
                att_ref[0, g, :, col + HEAD_DIM:col + 2 * HEAD_DIM] = v.astype(BF16)


def _post_proj(proj, rope_tabs, b, t, tt):
    nt = t // tt
    state = jax.ShapeDtypeStruct((b, NSA_KV, 2 * t, HEAD_DIM), F32)
    att = jax.ShapeDtypeStruct((b, NSA_KV, t, ATT_W), BF16)
    kvw = NSA_KV * KV_ROW
    tab_spec = pl.BlockSpec((tt, HEAD_DIM), lambda bi, ti: (ti, 0))
    st_spec = pl.BlockSpec((1, NSA_KV, 2 * tt, HEAD_DIM), lambda bi, ti: (bi, 0, ti, 0))
    att_spec = pl.BlockSpec((1, NSA_KV, tt, ATT_W), lambda bi, ti: (bi, 0, ti, 0))
    return pl.pallas_call(
        _postproj_kernel,
        out_shape=(jax.ShapeDtypeStruct((b * t, NSA_Q_W), BF16), state, state, state, att),
        grid=(b, nt),
        in_specs=[pl.BlockSpec((tt, NSA_Q_W), lambda bi, ti: (bi * nt + ti, OFF_Q // NSA_Q_W)),
                  pl.BlockSpec((tt, kvw), lambda bi, ti: (bi * nt + ti, OFF_CMP // kvw)),
                  pl.BlockSpec((tt, kvw), lambda bi, ti: (bi * nt + ti, OFF_SEL // kvw)),
                  pl.BlockSpec((tt, kvw), lambda bi, ti: (bi * nt + ti, OFF_WIN // kvw)),
                  tab_spec, tab_spec, tab_spec],
        out_specs=(pl.BlockSpec((tt, NSA_Q_W), lambda bi, ti: (bi * nt + ti, 0)),
                   st_spec, st_spec, st_spec, att_spec),
        compiler_params=_cparams(("arbitrary", "arbitrary")),
        name="post_proj",
    )(proj, proj, proj, proj, *rope_tabs)


def _rope_tables(pos):
    half = ROT_DIM // 2
    inv = ROPE_THETA ** (-jnp.arange(half, dtype=F32) * 2.0 / ROT_DIM)
    ang = pos.astype(F32)[:, None] * inv
    cos, sin = jnp.cos(ang), jnp.sin(ang)
    n = pos.shape[0]
    rest = HEAD_DIM - ROT_DIM
    c = jnp.concatenate([cos, cos, jnp.ones((n, rest), F32)], axis=1)
    s1 = jnp.concatenate([-sin, jnp.zeros((n, HEAD_DIM - half), F32)], axis=1)
    s2 = jnp.concatenate([jnp.zeros((n, half), F32), sin, jnp.zeros((n, rest), F32)], axis=1)
    return c, s1, s2


def _gelu_tanh(x):
    return 0.5 * x * (1.0 + jnp.tanh(math.sqrt(2.0 / math.pi) * (x + 0.044715 * (x * x * x))))


def _compress_blocks(x_ref, base, pitch, nblk, pe_ref, w1k_ref, w1v_ref, w2k_ref, w2v_ref, ko_ref, vo_ref):
    hk = jnp.zeros((nblk, CMP_HID), F32)
    hv = jnp.zeros((nblk, CMP_HID), F32)
    for r in range(BLOCK):
        xk = x_ref[pl.ds(base + 2 * r, nblk, stride=pitch), :] + pe_ref[0, r:r + 1, :]
        xv = x_ref[pl.ds(base + 2 * r + 1, nblk, stride=pitch), :] + pe_ref[1, r:r + 1, :]
        hk = hk + _dot(xk, w1k_ref[r])
        hv = hv + _dot(xv, w1v_ref[r])
    ko_ref[...] = _dot(_gelu_tanh(hk), w2k_ref[...])
    vo_ref[...] = _dot(_gelu_tanh(hv), w2v_ref[...])


def _compress_kernel(x_ref, *rest):
    _compress_blocks(x_ref, 0, BLK_ROWS, x_ref.shape[0] // BLK_ROWS, *rest)


def _compress_weights(cmp_pe, cmp_w1, cmp_w2, l):
    w1 = cmp_w1[l].astype(BF16).reshape(2, BLOCK, HEAD_DIM, CMP_HID)
    w2 = cmp_w2[l].astype(BF16)
    return cmp_pe[l], w1[0], w1[1], w2[0], w2[1]


def _cw_specs(nargs):
    z = (0,) * 3
    return [pl.BlockSpec((2, BLOCK, HEAD_DIM), lambda *a: z),
            pl.BlockSpec((BLOCK, HEAD_DIM, CMP_HID), lambda *a: z),
            pl.BlockSpec((BLOCK, HEAD_DIM, CMP_HID), lambda *a: z),
            pl.BlockSpec((CMP_HID, HEAD_DIM), lambda *a: (0, 0)),
            pl.BlockSpec((CMP_HID, HEAD_DIM), lambda *a: (0, 0))]


def _compress(x_rows, cw, tr):
    nblocks = x_rows.shape[0] // BLK_ROWS
    out = jax.ShapeDtypeStruct((nblocks, HEAD_DIM), F32)
    return pl.pallas_call(
        _compress_kernel,
        out_shape=(out, out),
        grid=(nblocks // tr,),
        in_specs=[pl.BlockSpec((tr * BLK_ROWS, HEAD_DIM), lambda i: (i, 0))] + _cw_specs(1),
        out_specs=(pl.BlockSpec((tr, HEAD_DIM), lambda i: (i, 0)),) * 2,
        compiler_params=_cparams(("arbitrary",)),
        name="compress",
    )(x_rows, *cw)


MAX_PAGES_PER_STEP = 32


def _compress_paged_kernel(pt_ref, cache_ref, pe_ref, w1k_ref, w1v_ref, w2k_ref, w2v_ref, ko_ref, vo_ref,
                           xbuf, sem, *, l):
    bpp = cache_ref.shape[2]
    nblk = ko_ref.shape[0]
    pages_per_step = nblk // bpp
    ns = pl.num_programs(1)
    step = pl.program_id(0) * ns + pl.program_id(1)
    total = pl.num_programs(0) * ns

    def copies(st, slot):
        bi, s = st // ns, st % ns
        out = []
        for k in range(pages_per_step):
            pg = pt_ref[bi, s * pages_per_step + k]
            for j in range(bpp):
                row0 = pl.multiple_of(slot * (nblk * BLK_PITCH) + (k * bpp + j) * BLK_PITCH, 8)
                out.append(pltpu.make_async_copy(cache_ref.at[pg, l, j], xbuf.at[pl.ds(row0, BLK_ROWS), :],
                                                 sem.at[slot]))
        return out

    slot = step % 2

    @pl.when(step == 0)
    def _():
        for c in copies(step, slot):
            c.start()

    @pl.when(step + 1 < total)
    def _():
        for c in copies(step + 1, 1 - slot):
            c.start()

    for c in copies(step, slot):
        c.wait()
    _compress_blocks(xbuf, slot * (nblk * BLK_PITCH), BLK_PITCH, nblk,
                     pe_ref, w1k_ref, w1v_ref, w2k_ref, w2v_ref, ko_ref, vo_ref)


def _compress_paged(cache, page_table, cw, l):
    b, n_pages = page_table.shape
    bpp = cache.shape[2]
    pages_per_step = math.gcd(n_pages, MAX_PAGES_PER_STEP)
    steps = n_pages // pages_per_step
    tr = pages_per_step * bpp
    out = jax.ShapeDtypeStruct((b * steps * tr, HEAD_DIM), F32)
    grid_spec = pltpu.PrefetchScalarGridSpec(
        num_scalar_prefetch=1,
        grid=(b, steps),
        in_specs=[pl.BlockSpec(memory_space=pl.ANY)] + _cw_specs(3),
        out_specs=(pl.BlockSpec((tr, HEAD_DIM), lambda bi, s, pt: (bi * steps + s, 0)),) * 2,
        scratch_shapes=[pltpu.VMEM((2 * tr * BLK_PITCH, HEAD_DIM), F32), pltpu.SemaphoreType.DMA((2,))],
    )
    return pl.pallas_call(
        functools.partial(_compress_paged_kernel, l=l),
        out_shape=(out, out),
        grid_spec=grid_spec,
        compiler_params=_cparams(("arbitrary", "arbitrary")),
        name="compress_paged",
    )(page_table, cache, *cw)


def _stable_topn_mask(score, n_sel):
    nb = score.shape[-1]
    jcol = lax.broadcasted_iota(jnp.int32, score.shape, 1)
    rank = jnp.zeros(score.shape, F32)
    for k in range(nb):
        ck = score[:, k:k + 1]
        beats = (ck > score) | ((ck == score) & (jcol > k))
        rank = rank + jnp.where(beats, 1.0, 0.0)
    return jnp.where(rank < n_sel, 1.0, 0.0)


MASK_BIAS = -(2.0 ** 100)
SCORE_C = SCALE * 1.4426950408889634


def _softmax_pv(parts):
    m = jnp.max(parts[0][0], axis=-1, keepdims=True)
    for s, _ in parts[1:]:
        m = jnp.maximum(m, jnp.max(s, axis=-1, keepdims=True))
    den, acc = 0.0, 0.0
    for s, v in parts:
        e = jnp.exp2(s - m)
        den = den + jnp.sum(e, axis=-1, keepdims=True)
        acc = acc + jnp.dot(e.astype(BF16), v, preferred_element_type=F32)
    return acc / den


def _nsa_prompt_kernel(q_ref, sm_ref, kcb_ref, vcb_ref, att_ref, o_ref, os_ref, *, n_sel, wl, n_cls):
    i = pl.program_id(2)
    t_all = att_ref.shape[2]
    nb = kcb_ref.shape[2]
    rows = NSA_HPG * BLOCK
    q = q_ref[...]
    qs = jnp.concatenate([q[:, h * HEAD_DIM:(h + 1) * HEAD_DIM] for h in range(NSA_HPG)], axis=0)
    q0 = i * BLOCK
    tq = lax.broadcasted_iota(jnp.int32, (rows, 1), 0) & (BLOCK - 1)
    qpos = q0 + tq

    jc = lax.broadcasted_iota(jnp.int32, (rows, nb), 1)
    cmask = (jc * BLOCK + (BLOCK - 1)) <= qpos
    p_c = _masked_softmax(_dot_nt(qs, kcb_ref[0, 0]) * SCALE, cmask)
    o_c = _dot(p_c, vcb_ref[0, 0])

    imp = p_c[0:BLOCK]
    for h in range(1, NSA_HPG):
        imp = imp + p_c[h * BLOCK:(h + 1) * BLOCK]
    jb = lax.broadcasted_iota(jnp.int32, (BLOCK, nb), 1)
    forced = (jb == 0) | (jb == i) | (jb == i - 1)
    score = jnp.where(forced, jnp.inf, jnp.where(jb > i, NEG_INF, imp))
    sel = _stable_topn_mask(score, n_sel)
    blk_bias = jnp.where((jb < i) & (sel > 0.5), 0.0, MASK_BIAS)
    blk_bias = jnp.concatenate([blk_bias, jnp.zeros((BLOCK, HEAD_DIM - nb), F32)], axis=1).astype(BF16)
    q_aug = jnp.concatenate([qs, jnp.concatenate([blk_bias] * NSA_HPG, axis=0)], axis=1)
    r0 = pl.multiple_of(q0, BLOCK)
    tl = lax.broadcasted_iota(jnp.int32, (rows, BLOCK), 1)
    s_d = (lax.dot_general(qs, att_ref[0, 0, pl.ds(r0, BLOCK), 0:HEAD_DIM], (((1,), (1,)), ((), ())),
                           preferred_element_type=F32) * SCORE_C + jnp.where(tl <= tq, 0.0, MASK_BIAS))
    v_d = att_ref[0, 0, pl.ds(r0, BLOCK), 2 * HEAD_DIM:3 * HEAD_DIM]
    per_cls = nb // n_cls
    for c in range(n_cls):
        nk = (c + 1) * per_cls * BLOCK

        @pl.when((i >= c * per_cls) & (i < (c + 1) * per_cls))
        def _():
            s_b = lax.dot_general(q_aug, att_ref[0, 0, 0:nk, 0:2 * HEAD_DIM], (((1,), (1,)), ((), ())),
                                  preferred_element_type=F32) * SCORE_C
            os_ref[...] = _softmax_pv([(s_b, att_ref[0, 0, 0:nk, 2 * HEAD_DIM:3 * HEAD_DIM]), (s_d, v_d)])

    o_s = os_ref[...]

    start = pl.multiple_of(jnp.clip(q0 - WINDOW, 0, t_all - wl), BLOCK)
    wpos = start + lax.broadcasted_iota(jnp.int32, (rows, wl), 1)
    wbias = jnp.where((wpos <= qpos) & (wpos >= qpos - WINDOW), 0.0, MASK_BIAS)
    s_w = lax.dot_general(qs, att_ref[0, 0, pl.ds(start, wl), 3 * HEAD_DIM:4 * HEAD_DIM], (((1,), (1,)), ((), ())),
                          preferred_element_type=F32) * SCORE_C + wbias
    o_w = _softmax_pv([(s_w, att_ref[0, 0, pl.ds(start, wl), 4 * HEAD_DIM:5 * HEAD_DIM])])

    gt = _sigmoid(sm_ref[...])
    for h in range(NSA_HPG):
        r = slice(h * BLOCK, (h + 1) * BLOCK)
        o = (gt[:, 3 * h:3 * h + 1] * o_c[r] + gt[:, 3 * h + 1:3 * h + 2] * o_s[r]
             + gt[:, 3 * h + 2:3 * h + 3] * o_w[r])
        o_ref[:, h * HEAD_DIM:(h + 1) * HEAD_DIM] = o.astype(BF16)


def _nsa_prompt(q_rot, small, kcb, vcb, att, b, t):
    nq = t // BLOCK
    nb = kcb.shape[2]
    assert nb <= HEAD_DIM
    gw = NSA_HPG * HEAD_DIM
    wl = min(WINDOW + BLOCK, t)
    cb_spec = pl.BlockSpec((1, 1, nb, HEAD_DIM), lambda bi, g, i: (bi, g, 0, 0))
    return pl.pallas_call(
        functools.partial(_nsa_prompt_kernel, n_sel=min(TOPN, nb), wl=wl, n_cls=math.gcd(nb, 4)),
        out_shape=jax.ShapeDtypeStruct((b * t, NSA_Q_W), BF16),
        grid=(b, NSA_KV, nq),
        in_specs=[pl.BlockSpec((BLOCK, gw), lambda bi, g, i: (bi * nq + i, g)),
                  pl.BlockSpec((BLOCK, HEAD_DIM), lambda bi, g, i: (bi * nq + i, g)),
                  cb_spec, cb_spec,
                  pl.BlockSpec((1, 1, t, ATT_W), lambda bi, g, i: (bi, g, 0, 0))],
        out_specs=pl.BlockSpec((BLOCK, gw), lambda bi, g, i: (bi * nq + i, g)),
        scratch_shapes=[pltpu.VMEM((NSA_HPG * BLOCK, HEAD_DIM), F32)],
        compiler_params=_cparams(("arbitrary", "arbitrary", "arbitrary")),
        name="nsa_prompt",
    )(q_rot, small, kcb, vcb, att)


def _softplus(x):
    return jnp.maximum(x, 0.0) + jnp.log(1.0 + jnp.exp(-jnp.abs(x)))


def _unit_lower_solve(l_strict, rhs):
    n = l_strict[0].shape[0]
    sub = 8
    nt = n // sub
    tiles = [[r[t * sub:(t + 1) * sub, :] for t in range(nt)] for r in rhs]
    for i in range(n - 1):
        t0, s = divmod(i, sub)
        for p, lm in enumerate(l_strict):
            xi = tiles[p][t0][s:s + 1, :]
            col = lm[:, i:i + 1]
            for t in range(t0, nt):
                tiles[p][t] = tiles[p][t] - col[t * sub:(t + 1) * sub, :] * xi
    return [jnp.concatenate(tp, axis=0) for tp in tiles]


def _gdn_prompt_kernel(x_ref, z_ref, sm_ref, abr_ref, cw_ref, prow_ref, pcol_ref, nw_ref,
                       o_ref, s_out_ref, conv_out_ref, s_ref, xe_ref):
    n = pl.program_id(1)
    c = GDN_CHUNK
    pad = 8

    @pl.when(n == 0)
    def _():
        s_ref[...] = jnp.zeros_like(s_ref)
        xe_ref[0:pad, :] = jnp.zeros((pad, CONV_CH), F32)

    xe_ref[pad:pad + c, :] = x_ref[...]
    y = xe_ref[pl.ds(pad, c), :] * cw_ref[CONV_W - 1:CONV_W, :]
    for w in range(CONV_W - 1):
        y = y + xe_ref[pl.ds(pad - (CONV_W - 1) + w, c), :] * cw_ref[w:w + 1, :]
    y = _silu(y)

    ri = lax.broadcasted_iota(jnp.int32, (c, c), 0)
    ci = lax.broadcasted_iota(jnp.int32, (c, c), 1)
    tri = ri >= ci
    strict = ri > ci
    tri_f = jnp.where(tri, 1.0, 0.0)

    a_col = sm_ref[:, A_COL:A_COL + GDN_HEADS]
    b_col = sm_ref[:, B_COL:B_COL + GDN_HEADS]
    g_col = -jnp.exp(prow_ref[0:1, :]) * _softplus(a_col + prow_ref[1:2, :])
    g_row = -jnp.exp(pcol_ref[:, 0:1]) * _softplus(abr_ref[0, 0, 0:GDN_HEADS, :] + pcol_ref[:, 1:2])
    gc_col = _dot_exact(tri_f, g_col)
    gc_row = _dot_nt_exact(g_row, tri_f)
    beta_col = _sigmoid(b_col)

    pre = []
    for h in range(GDN_HEADS):
        sl = slice(h * HEAD_DIM, (h + 1) * HEAD_DIM)
        yq, yk, v = y[:, sl], y[:, GDN_W + h * HEAD_DIM:GDN_W + (h + 1) * HEAD_DIM], \
            y[:, 2 * GDN_W + h * HEAD_DIM:2 * GDN_W + (h + 1) * HEAD_DIM]
        q = yq * lax.rsqrt(jnp.sum(yq * yq, axis=-1, keepdims=True) + EPS) * SCALE
        k = yk * lax.rsqrt(jnp.sum(yk * yk, axis=-1, keepdims=True) + EPS)
        gcc = gc_col[:, h:h + 1]
        beta = beta_col[:, h:h + 1]
        decay = jnp.exp(jnp.where(tri, gcc - gc_row[h:h + 1, :], NEG_INF))
        kb = k * beta
        lmat = jnp.where(strict, _dot_nt(kb, k) * decay, 0.0)
        rhs = jnp.concatenate([v * beta, kb * jnp.exp(gcc)], axis=1)
        pre.append((q, k, gcc, _dot_nt(q, k) * decay, lmat, rhs))

    sols = _unit_lower_solve([p[4] for p in pre], [p[5] for p in pre])

    s_out, o_out = [], []
    for h in range(GDN_HEADS):
        q, k, gcc, qk, _, _ = pre[h]
        u, w = sols[h][:, :HEAD_DIM], sols[h][:, HEAD_DIM:]
        s = s_ref[h]
        v_new = u - _dot(w, s)
        o = _dot(q * jnp.exp(gcc), s) + _dot(qk, v_new)
        g_last = gcc[c - 1:c, :]
        k_dec = k * jnp.exp(g_last - gcc)
        s_out.append(s * jnp.exp(g_last) + lax.dot_general(
            k_dec.astype(BF16), v_new.astype(BF16), (((0,), (0,)), ((), ())), preferred_element_type=F32))
        o_out.append((_rms(o, nw_ref[...]) * _silu(z_ref[:, h * HEAD_DIM:(h + 1) * HEAD_DIM])).astype(BF16))
    for h in range(GDN_HEADS):
        s_ref[h] = s_out[h]
        o_ref[:, h * HEAD_DIM:(h + 1) * HEAD_DIM] = o_out[h]
    xe_ref[0:pad, :] = xe_ref[c:c + pad, :]

    @pl.when(n == pl.num_programs(1) - 1)
    def _():
        s_out_ref[0] = s_ref[...]
        conv_out_ref[0] = xe_ref[pad - (CONV_W - 1):pad, :]


def _gdn_prompt(proj, small, conv_w, a_log, dt_bias, norm_w, b, t):
    c = GDN_CHUNK
    nc = t // c
    ab_row = small.reshape(b, nc, c, SMALL_W)[..., A_COL:A_COL + 2 * GDN_HEADS].transpose(0, 1, 3, 2)
    prow = jnp.stack([a_log, dt_bias], axis=0)
    pcol = prow.T
    return pl.pallas_call(
        _gdn_prompt_kernel,
        out_shape=(jax.ShapeDtypeStruct((b * t, GDN_W), BF16),
                   jax.ShapeDtypeStruct((b, GDN_HEADS, HEAD_DIM, HEAD_DIM), F32),
                   jax.ShapeDtypeStruct((b, CONV_W - 1, CONV_CH), F32)),
        grid=(b, nc),
        in_specs=[pl.BlockSpec((c, CONV_CH), lambda bi, n: (bi * nc + n, OFF_CONV // CONV_CH)),
                  pl.BlockSpec((c, GDN_W), lambda bi, n: (bi * nc + n, OFF_Z // GDN_W)),
                  pl.BlockSpec((c, HEAD_DIM), lambda bi, n: (bi * nc + n, 0)),
                  pl.BlockSpec((1, 1, 2 * GDN_HEADS, c), lambda bi, n: (bi, n, 0, 0)),
                  pl.BlockSpec((CONV_W, CONV_CH), lambda bi, n: (0, 0)),
                  pl.BlockSpec((2, GDN_HEADS), lambda bi, n: (0, 0)),
                  pl.BlockSpec((GDN_HEADS, 2), lambda bi, n: (0, 0)),
                  pl.BlockSpec((1, HEAD_DIM), lambda bi, n: (0, 0))],
        out_specs=(pl.BlockSpec((c, GDN_W), lambda bi, n: (bi * nc + n, 0)),
                   pl.BlockSpec((1, GDN_HEADS, HEAD_DIM, HEAD_DIM), lambda bi, n: (bi, 0, 0, 0)),
                   pl.BlockSpec((1, CONV_W - 1, CONV_CH), lambda bi, n: (bi, 0, 0))),
        scratch_shapes=[pltpu.VMEM((GDN_HEADS, HEAD_DIM, HEAD_DIM), F32),
                        pltpu.VMEM((c + 16, CONV_CH), F32)],
        compiler_params=_cparams(("arbitrary", "arbitrary")),
        name="gdn_prompt",
    )(proj, proj, small, ab_row, conv_w, prow, pcol, norm_w.reshape(1, HEAD_DIM))


def _outproj_kernel(on_ref, og_ref, w_ref, x_ref, ga_ref, g_ref, o_ref):
    o = jnp.concatenate([on_ref[...], og_ref[...]], axis=1)
    mix = jnp.dot(o, w_ref[0], preferred_element_type=F32)
    o_ref[...] = x_ref[...] + ga_ref[0] * _rms(mix, g_ref[...])


def _out_proj(o_nsa, o_gdn, w_out, x, ga, g, l, rows_per_mod, tm):
    m, d = x.shape
    kw = o_nsa.shape[1]
    return pl.pallas_call(
        _outproj_kernel,
        out_shape=jax.ShapeDtypeStruct((m, d), F32),
        grid=(m // tm,),
        in_specs=[pl.BlockSpec((tm, kw), lambda i: (i, 0)),
                  pl.BlockSpec((tm, kw), lambda i: (i, 0)),
                  pl.BlockSpec((1, 2 * kw, d), lambda i: (l, 0, 0)),
                  pl.BlockSpec((tm, d), lambda i: (i, 0)),
                  _mod_spec(ga, rows_per_mod, tm),
                  pl.BlockSpec((1, d), lambda i: (0, 0))],
        out_specs=pl.BlockSpec((tm, d), lambda i: (i, 0)),
        compiler_params=_cparams(("arbitrary",)),
        name="out_proj",
    )(o_nsa, o_gdn, w_out, x, ga, g)


def _ffn_kernel(x_ref, g1_ref, sc_ref, sh_ref, wg_ref, wu_ref, wd_ref, ga_ref, g2_ref, o_ref, h_ref, acc_ref):
    j = pl.program_id(1)

    @pl.when(j == 0)
    def _():
        h = _rms(x_ref[...], g1_ref[...]) * (1.0 + sc_ref[0]) + sh_ref[0]
        h_ref[...] = h.astype(BF16)
        acc_ref[...] = jnp.zeros_like(acc_ref)

    h = h_ref[...]
    a = jnp.dot(h, wg_ref[0], preferred_element_type=F32)
    u = jnp.dot(h, wu_ref[0], preferred_element_type=F32)
    acc_ref[...] += _dot(_silu(a) * u, wd_ref[0])

    @pl.when(j == pl.num_programs(1) - 1)
    def _():
        o_ref[...] = x_ref[...] + ga_ref[0] * _rms(acc_ref[...], g2_ref[...])


def _ffn(x, g1, sc, sh, w_gate, w_up, w_down, ga, g2, l, rows_per_mod, tm, tf=512):
    m, d = x.shape
    f = w_gate.shape[2]
    vec = pl.BlockSpec((1, d), lambda i, j: (0, 0))
    return pl.pallas_call(
        _ffn_kernel,
        out_shape=jax.ShapeDtypeStruct((m, d), F32),
        grid=(m // tm, f // tf),
        in_specs=[pl.BlockSpec((tm, d), lambda i, j: (i, 0)), vec,
                  _mod_spec(sc, rows_per_mod, tm), _mod_spec(sh, rows_per_mod, tm),
                  pl.BlockSpec((1, d, tf), lambda i, j: (l, 0, j)),
                  pl.BlockSpec((1, d, tf), lambda i, j: (l, 0, j)),
                  pl.BlockSpec((1, tf, d), lambda i, j: (l, j, 0)),
                  _mod_spec(ga, rows_per_mod, tm), vec],
        out_specs=pl.BlockSpec((tm, d), lambda i, j: (i, 0)),
        scratch_shapes=[pltpu.VMEM((tm, d), BF16), pltpu.VMEM((tm, d), F32)],
        compiler_params=_cparams(("arbitrary", "arbitrary")),
        name="ffn",
    )(x, g1, sc, sh, w_gate, w_up, w_down, ga, g2)


def _stack_heads(q, h0):
    rows = [q[:, (h0 + h) * HEAD_DIM:(h0 + h + 1) * HEAD_DIM] for h in range(NSA_HPG)]
    return jnp.concatenate(rows + [jnp.zeros((NSA_HPG, HEAD_DIM), F32)], axis=0)


def _nsa_sample_cmp_kernel(q_ref, kcb_ref, vcb_ref, oc_ref, idx_ref, *, n_sel, q_pos):
    nbp = kcb_ref.shape[2]
    cur = q_pos // BLOCK
    q = q_ref[0].astype(F32)
    for g in range(NSA_KV):
        qs = _stack_heads(q, g * NSA_HPG)
        jc = lax.broadcasted_iota(jnp.int32, (qs.shape[0], nbp), 1)
        cmask = (jc * BLOCK + (BLOCK - 1)) <= q_pos
        p_c = _masked_softmax(_dot_nt(qs, kcb_ref[0, g]) * SCALE, cmask)
        o_c = _dot(p_c, vcb_ref[0, g])
        for h in range(NSA_HPG):
            c0 = (g * NSA_HPG + h) * HEAD_DIM
            oc_ref[0, :, c0:c0 + HEAD_DIM] = o_c[h:h + 1]
        imp = p_c[0:1]
        for h in range(1, NSA_HPG):
            imp = imp + p_c[h:h + 1]
        jb = jc[0:1]
        forced = (jb == 0) | (jb == cur) | (jb == cur - 1)
        score = jnp.where(forced, jnp.inf, jnp.where(jb > cur, NEG_INF, imp))
        left = jb >= 0
        for r in range(n_sel):
            best = jnp.max(jnp.where(left, score, NEG_INF), axis=1, keepdims=True)
            pick = jnp.min(jnp.where(left & (score == best), jb, nbp), axis=1, keepdims=True)
            idx_ref[0, g, r:r + 1, :] = jnp.broadcast_to(pick, (1, HEAD_DIM))
            left = left & (jb != pick)


def _nsa_sample_cmp(q_rot, kcb, vcb, q_pos, n_blocks):
    b = q_rot.shape[0]
    nbp = kcb.shape[2]
    n_sel = min(TOPN, n_blocks)
    cb_spec = pl.BlockSpec((1, NSA_KV, nbp, HEAD_DIM), lambda bi: (bi, 0, 0, 0))
    return pl.pallas_call(
        functools.partial(_nsa_sample_cmp_kernel, n_sel=n_sel, q_pos=q_pos),
        out_shape=(jax.ShapeDtypeStruct((b, 1, NSA_Q_W), F32),
                   jax.ShapeDtypeStruct((b, NSA_KV, n_sel, HEAD_DIM), jnp.int32)),
        grid=(b,),
        in_specs=[pl.BlockSpec((1, 1, NSA_Q_W), lambda bi: (bi, 0, 0)), cb_spec, cb_spec],
        out_specs=(pl.BlockSpec((1, 1, NSA_Q_W), lambda bi: (bi, 0, 0)),
                   pl.BlockSpec((1, NSA_KV, n_sel, HEAD_DIM), lambda bi: (bi, 0, 0, 0))),
        compiler_params=_cparams(("arbitrary",)),
        name="nsa_sample_cmp",
    )(q_rot.reshape(b, 1, NSA_Q_W), kcb, vcb)


def _nsa_sample_attn_kernel(idx_ref, pt_ref, q_ref, sm_ref, oc_ref, cache_ref, newsel_ref, win_ref, newwin_ref,
                            o_ref, wino_ref, blk_buf, sem, *, l, npb, n_sel):
    bi, g = pl.program_id(0), pl.program_id(1)
    bpp = cache_ref.shape[3]
    copies, picks = [], []
    for n in range(n_sel):
        pick = idx_ref[bi, g, n]
        jp = jnp.minimum(pick, npb - 1)
        cp = pltpu.make_async_copy(cache_ref.at[pt_ref[bi, jp // bpp], l, g, jp % bpp],
                                   blk_buf.at[pl.ds(n * BLK_ROWS, BLK_ROWS), :], sem)
        cp.start()
        copies.append(cp)
        picks.append(pick)

    qs = _stack_heads(q_ref[0].astype(F32), 0)

    def row_scores(k):
        return jnp.sum(qs * k.astype(BF16).astype(F32), axis=-1, keepdims=True) * SCALE

    wb = win_ref.shape[3] // 2
    s_buf = _dot_nt(qs, win_ref[0, 0, 0, pl.ds(0, wb, stride=2), :]) * SCALE
    s_new = row_scores(newwin_ref[0, 0, 0:1, :])
    m = jnp.maximum(jnp.max(s_buf, axis=-1, keepdims=True), s_new)
    e_buf, e_new = jnp.exp(s_buf - m), jnp.exp(s_new - m)
    den = jnp.maximum(jnp.sum(e_buf, axis=-1, keepdims=True) + e_new, 1e-30)
    o_w = (_dot(e_buf, win_ref[0, 0, 0, pl.ds(1, wb, stride=2), :])
           + e_new * newwin_ref[0, 0, 1:2, :].astype(BF16).astype(F32)) / den
    wino_ref[0, 0, 0:2 * wb - 2, :] = win_ref[0, 0, 0, 2:2 * wb, :]
    wino_ref[0, 0, 2 * wb - 2:2 * wb, :] = newwin_ref[0, 0]

    for cp in copies:
        cp.wait()
    nk = n_sel * BLOCK
    slot = lax.broadcasted_iota(jnp.int32, (1, nk), 1) // BLOCK
    bias = jnp.zeros((1, nk), F32)
    has_new = picks[0] >= npb
    for n in range(n_sel):
        bias = jnp.where((slot == n) & (picks[n] >= npb), NEG_INF, bias)
        has_new = has_new | (picks[n] >= npb)
    s_blk = _dot_nt(qs, blk_buf[pl.ds(0, nk, stride=2), :]) * SCALE + bias
    s_row = jnp.where(has_new, row_scores(newsel_ref[0, 0, 0:1, :]), NEG_INF)
    m = jnp.maximum(jnp.max(s_blk, axis=-1, keepdims=True), s_row)
    m = jnp.where(m == NEG_INF, 0.0, m)
    e_blk, e_row = jnp.exp(s_blk - m), jnp.exp(s_row - m)
    den = jnp.maximum(jnp.sum(e_blk, axis=-1, keepdims=True) + e_row, 1e-30)
    o_s = (_dot(e_blk, blk_buf[pl.ds(1, nk, stride=2), :])
           + e_row * newsel_ref[0, 0, 1:2, :].astype(BF16).astype(F32)) / den

    gt = _sigmoid(sm_ref[0])
    for h in range(NSA_HPG):
        c0 = h * HEAD_DIM
        o = (gt[:, 3 * h:3 * h + 1] * oc_ref[0, :, c0:c0 + HEAD_DIM] + gt[:, 3 * h + 1:3 * h + 2] * o_s[h:h + 1]
             + gt[:, 3 * h + 2:3 * h + 3] * o_w[h:h + 1])
        o_ref[0, :, c0:c0 + HEAD_DIM] = o.astype(BF16)


def _nsa_sample_attn(idx, page_table, q_rot, small, o_c, cache_sel, new_sel, cache_win, new_win, l, npb):
    b = q_rot.shape[0]
    n_sel = idx.shape[2]
    bpp = cache_sel.shape[3]
    wb2 = cache_win.shape[3]
    gw = NSA_HPG * HEAD_DIM

    new_spec = pl.BlockSpec((1, 1, 2, HEAD_DIM), lambda bi, g, *_: (bi, g, 0, 0))
    grid_spec = pltpu.PrefetchScalarGridSpec(
        num_scalar_prefetch=2,
        grid=(b, NSA_KV),
        in_specs=[pl.BlockSpec((1, 1, gw), lambda bi, g, *_: (bi, 0, g)),
                  pl.BlockSpec((1, 1, HEAD_DIM), lambda bi, g, *_: (bi, 0, g)),
                  pl.BlockSpec((1, 1, gw), lambda bi, g, *_: (bi, 0, g)),
                  pl.BlockSpec(memory_space=pl.ANY),
                  new_spec,
                  pl.BlockSpec((1, 1, 1, wb2, HEAD_DIM), lambda bi, g, *_: (bi, l, g, 0, 0)),
                  new_spec],
        out_specs=(pl.BlockSpec((1, 1, gw), lambda bi, g, *_: (bi, 0, g)),
                   pl.BlockSpec((1, 1, wb2, HEAD_DIM), lambda bi, g, *_: (bi, g, 0, 0))),
        scratch_shapes=[pltpu.VMEM((n_sel * BLK_ROWS, HEAD_DIM), F32), pltpu.SemaphoreType.DMA(())],
    )
    return pl.pallas_call(
        functools.partial(_nsa_sample_attn_kernel, l=l, npb=npb, n_sel=n_sel),
        out_shape=(jax.ShapeDtypeStruct((b, 1, NSA_Q_W), BF16),
                   jax.ShapeDtypeStruct((b, NSA_KV, wb2, HEAD_DIM), F32)),
        grid_spec=grid_spec,
        compiler_params=_cparams(("arbitrary", "arbitrary")),
        name="nsa_sample_attn",
    )(idx, page_table, q_rot.reshape(b, 1, NSA_Q_W), small.reshape(b, 1, SMALL_W), o_c,
      cache_sel, new_sel, cache_win, new_win)


def _gdn_sample_kernel(x_ref, z_ref, sm_ref, buf_ref, s0_ref, cw_ref, prow_ref, nw_ref, eye_ref,
                       o_ref, s_out_ref, conv_out_ref):
    x = x_ref[0]
    buf = buf_ref[0, 0]
    y = x * cw_ref[CONV_W - 1:CONV_W, :]
    for w in range(CONV_W - 1):
        y = y + buf[w:w + 1, :] * cw_ref[w:w + 1, :]
    y = _silu(y)
    conv_out_ref[0, 0:CONV_W - 2, :] = buf[1:CONV_W - 1, :]
    conv_out_ref[0, CONV_W - 2:CONV_W - 1, :] = x

    def heads(off):
        return jnp.concatenate([y[:, off + h * HEAD_DIM:off + (h + 1) * HEAD_DIM] for h in range(GDN_HEADS)], axis=0)

    yq, yk, v = heads(0), heads(GDN_W), heads(2 * GDN_W)
    q = yq * lax.rsqrt(jnp.sum(yq * yq, axis=-1, keepdims=True) + EPS) * SCALE
    k = yk * lax.rsqrt(jnp.sum(yk * yk, axis=-1, keepdims=True) + EPS)
    sm = sm_ref[0]
    g_row = -jnp.exp(prow_ref[0:1, :]) * _softplus(sm[:, A_COL:A_COL + GDN_HEADS] + prow_ref[1:2, :])
    beta_row = _sigmoid(sm[:, B_COL:B_COL + GDN_HEADS])
    eg_row = jnp.exp(g_row)
    k_t = _dot_nt_exact(eye_ref[...], k)
    q_t = _dot_nt_exact(eye_ref[...], q)
    z = z_ref[0]
    for h in range(GDN_HEADS):
        s0 = s0_ref[0, 0, h]
        eg = eg_row[:, h:h + 1]
        kc = k_t[:, h:h + 1].astype(BF16).astype(F32)
        qc = q_t[:, h:h + 1].astype(BF16).astype(F32)
        s0b = s0.astype(BF16).astype(F32)
        ks = jnp.sum(kc * beta_row[:, h:h + 1] * eg * s0b, axis=0, keepdims=True)
        v_new = v[h:h + 1] * beta_row[:, h:h + 1] - ks
        vb = v_new.astype(BF16).astype(F32)
        s1 = s0 * eg + kc * vb
        qe = (q_t[:, h:h + 1] * eg).astype(BF16).astype(F32)
        o = jnp.sum(qe * s0b, axis=0, keepdims=True) + jnp.sum(qc * kc, axis=0, keepdims=True) * vb
        s_out_ref[0, h] = s1
        sl = slice(h * HEAD_DIM, (h + 1) * HEAD_DIM)
        o = _rms(o, nw_ref[...]) * _silu(z[:, sl])
        o_ref[0, :, sl] = o.astype(BF16)


def _gdn_sample(proj, small, state_conv, state_gdn, conv_w, a_log, dt_bias, norm_w, l, b):
    prow = jnp.stack([a_log, dt_bias], axis=0)
    rows = proj.shape[0]
    return pl.pallas_call(
        _gdn_sample_kernel,
        out_shape=(jax.ShapeDtypeStruct((b, 1, GDN_W), BF16),
                   jax.ShapeDtypeStruct((b, GDN_HEADS, HEAD_DIM, HEAD_DIM), F32),
                   jax.ShapeDtypeStruct((b, CONV_W - 1, CONV_CH), F32)),
        grid=(b,),
        in_specs=[pl.BlockSpec((1, 1, CONV_CH), lambda bi: (bi, 0, OFF_CONV // CONV_CH)),
                  pl.BlockSpec((1, 1, GDN_W), lambda bi: (bi, 0, OFF_Z // GDN_W)),
                  pl.BlockSpec((1, 1, SMALL_W), lambda bi: (bi, 0, 0)),
                  pl.BlockSpec((1, 1, CONV_W - 1, CONV_CH), lambda bi: (bi, l, 0, 0)),
                  pl.BlockSpec((1, 1, GDN_HEADS, HEAD_DIM, HEAD_DIM), lambda bi: (bi, l, 0, 0, 0)),
                  pl.BlockSpec((CONV_W, CONV_CH), lambda bi: (0, 0)),
                  pl.BlockSpec((2, GDN_HEADS), lambda bi: (0, 0)),
                  pl.BlockSpec((1, HEAD_DIM), lambda bi: (0, 0)),
                  pl.BlockSpec((HEAD_DIM, HEAD_DIM), lambda bi: (0, 0))],
        out_specs=(pl.BlockSpec((1, 1, GDN_W), lambda bi: (bi, 0, 0)),
                   pl.BlockSpec((1, GDN_HEADS, HEAD_DIM, HEAD_DIM), lambda bi: (bi, 0, 0, 0)),
                   pl.BlockSpec((1, CONV_W - 1, CONV_CH), lambda bi: (bi, 0, 0))),
        compiler_params=_cparams(("arbitrary",)),
        name="gdn_sample",
    )(proj.reshape(rows, 1, MAIN_W), proj.reshape(rows, 1, MAIN_W), small.reshape(rows, 1, SMALL_W),
      state_conv, state_gdn, conv_w, prow, norm_w.reshape(1, HEAD_DIM), jnp.eye(HEAD_DIM, dtype=F32))


def _repack_w_in(w_in):
    kvw = NSA_KV * HEAD_DIM
    o_q = 0
    o_kv = [NSA_Q_W + i * kvw for i in range(6)]
    o_gl = NSA_Q_W + 6 * kvw
    o_conv = o_gl + 3 * NSA_HEADS
    o_a = o_conv + CONV_CH
    o_b = o_a + GDN_HEADS
    o_z = o_b + GDN_HEADS

    def cols(a, n):
        return w_in[:, :, a:a + n]

    pieces = [cols(o_conv, CONV_CH), cols(o_z, GDN_W), cols(o_q, NSA_Q_W)]
    for pair in range(3):
        for g in range(NSA_KV):
            pieces.append(cols(o_kv[2 * pair] + g * HEAD_DIM, HEAD_DIM))
            pieces.append(cols(o_kv[2 * pair + 1] + g * HEAD_DIM, HEAD_DIM))
    main = jnp.concatenate(pieces, axis=-1).astype(BF16)
    depth, d, _ = w_in.shape
    ngl = 3 * NSA_HPG
    small = jnp.concatenate([
        cols(o_gl, ngl), cols(o_a, GDN_HEADS), cols(o_b, GDN_HEADS),
        jnp.zeros((depth, d, HEAD_DIM - ngl - 2 * GDN_HEADS), w_in.dtype),
        cols(o_gl + ngl, ngl), jnp.zeros((depth, d, HEAD_DIM - ngl), w_in.dtype)], axis=-1).astype(BF16)
    return main, small


def _mods(mod_l, lo, hi, broadcast):
    d = mod_l.shape[1] // 6
    out = []
    for k in range(6):
        m = mod_l[lo:hi, k * d:(k + 1) * d]
        out.append(m[:, None, :] if broadcast else m[None])
    return out


def kernel(x_prompt, x_sample, cache_cmp_kv, cache_sel_kv, cache_win_kv, state_gdn, state_conv, page_table,
           c_prompt, c_sample, w_ada, b_ada, g_pre_mix, w_in, cmp_pe, cmp_w1, cmp_w2, conv_w, gdn_a_log,
           gdn_dt_bias, gdn_norm, w_out, g_post_mix, g_pre_ffn, w_gate, w_up, w_down, g_post_ffn):
    bp, t, d = x_prompt.shape
    bs = x_sample.shape[0]
    depth = w_in.shape[0]
    n_pool, _, _, page, _, _ = cache_cmp_kv.shape
    past = page_table.shape[1] * page
    npb = past // BLOCK
    bpp = page // BLOCK
    wb = cache_win_kv.shape[3]
    ts = 16
    assert x_sample.shape[1] == 1 and bs <= ts and t % BLOCK == 0 and page % BLOCK == 0

    c_all = jnp.concatenate([c_sample, jnp.zeros((ts - bs, d), F32), c_prompt,
                             jnp.zeros((-bp % 8, d), F32)], axis=0)
    mod = _ada_mod(c_all, w_ada, b_ada)

    w_main, w_small = _repack_w_in(w_in)
    w_out_b, w_gate_b, w_up_b, w_down_b = (w.astype(BF16) for w in (w_out, w_gate, w_up, w_down))
    rope_p = _rope_tables(jnp.arange(t))
    rope_s = _rope_tables(jnp.full((ts,), past))

    tm = min(512, t)
    xp = x_prompt.reshape(bp * t, d)
    xs = jnp.concatenate([x_sample.reshape(bs, d), jnp.zeros((ts - bs, d), F32)], axis=0)
    cache_cmp_blocks = cache_cmp_kv.reshape(n_pool, depth, NSA_KV * bpp, BLK_ROWS, HEAD_DIM)
    cache_sel_blocks = cache_sel_kv.reshape(n_pool, depth, NSA_KV, bpp, BLK_ROWS, HEAD_DIM)
    cache_win_rows = cache_win_kv.reshape(bs, depth, NSA_KV, 2 * wb, HEAD_DIM)
    nbp = -(-(npb + 1) // HEAD_DIM) * HEAD_DIM

    p_states, s_states = [], []
    for l in range(depth):
        gpm, gpo, gpf, gpof = (g[l][None] for g in (g_pre_mix, g_post_mix, g_pre_ffn, g_post_ffn))
        cw = _compress_weights(cmp_pe, cmp_w1, cmp_w2, l)

        sh1, sc1, ga1, sh2, sc2, ga2 = _mods(mod[l], ts, ts + bp, True)
        proj, small = _in_proj(xp, gpm, sc1, sh1, w_main, w_small, l, t, tm)
        q_rot, cmp_st, sel_st, win_st, att = _post_proj(proj, rope_p, bp, t, min(256, t))
        nb = t // BLOCK
        kcb, vcb = _compress(cmp_st.reshape(bp * NSA_KV * 2 * t, HEAD_DIM), cw, math.gcd(bp * NSA_KV * nb, 128))
        kcb, vcb = (a.reshape(bp, NSA_KV, nb, HEAD_DIM) for a in (kcb, vcb))
        o_nsa = _nsa_prompt(q_rot, small, kcb, vcb, att, bp, t)
        o_gdn, s_fin, conv_fin = _gdn_prompt(proj, small, conv_w[l], gdn_a_log[l], gdn_dt_bias[l], gdn_norm[l], bp, t)
        xp = _out_proj(o_nsa, o_gdn, w_out_b, xp, ga1, gpo, l, t, tm)
        xp = _ffn(xp, gpf, sc2, sh2, w_gate_b, w_up_b, w_down_b, ga2, gpof, l, t, tm)
        wl = min(WINDOW, t)
        p_states.append((cmp_st.reshape(bp, NSA_KV, t, 2, HEAD_DIM), sel_st.reshape(bp, NSA_KV, t, 2, HEAD_DIM),
                         win_st[:, :, 2 * (t - wl):].reshape(bp, NSA_KV, wl, 2, HEAD_DIM), s_fin, conv_fin))

        sh1, sc1, ga1, sh2, sc2, ga2 = _mods(mod[l], 0, ts, False)
        proj, small = _in_proj(xs, gpm, sc1, sh1, w_main, w_small, l, ts, ts)
        q_rot, cmp_new, sel_new, win_new, _ = _post_proj(proj, rope_s, 1, ts, ts)
        cmp_new, sel_new, win_new = (a.reshape(NSA_KV, ts, 2, HEAD_DIM)[:, :bs].transpose(1, 0, 2, 3)
                                     for a in (cmp_new, sel_new, win_new))
        kc_past, vc_past = _compress_paged(cache_cmp_blocks, page_table, cw, l)
        new_blk = jnp.pad(cmp_new.reshape(bs * NSA_KV, 2, HEAD_DIM), ((0, 0), (0, BLK_ROWS - 2), (0, 0)))
        kc_new, vc_new = _compress(new_blk.reshape(bs * NSA_KV * BLK_ROWS, HEAD_DIM), cw, bs * NSA_KV)

        def summaries(past_rows, new_rows):
            a = past_rows.reshape(bs, past // page, NSA_KV, bpp, HEAD_DIM).transpose(0, 2, 1, 3, 4)
            a = a.reshape(bs, NSA_KV, npb, HEAD_DIM)
            a = jnp.concatenate([a, new_rows.reshape(bs, NSA_KV, 1, HEAD_DIM)], axis=2)
            return jnp.pad(a, ((0, 0), (0, 0), (0, nbp - npb - 1), (0, 0)))

        kcb, vcb = summaries(kc_past, kc_new), summaries(vc_past, vc_new)
        o_c, idx = _nsa_sample_cmp(q_rot[:bs], kcb, vcb, past, npb + 1)
        o_nsa, win_out = _nsa_sample_attn(idx[..., 0], page_table, q_rot[:bs], small[:bs], o_c, cache_sel_blocks,
                                          sel_new, cache_win_rows, win_new, l, npb)
        o_gdn, s_fin, conv_fin = _gdn_sample(proj, small, state_conv, state_gdn, conv_w[l], gdn_a_log[l],
                                             gdn_dt_bias[l], gdn_norm[l], l, bs)
        pad_rows = ((0, ts - bs), (0, 0))
        o_nsa = jnp.pad(o_nsa.reshape(bs, NSA_Q_W), pad_rows)
        o_gdn = jnp.pad(o_gdn.reshape(bs, GDN_W), pad_rows)
        xs = _out_proj(o_nsa, o_gdn, w_out_b, xs, ga1, gpo, l, ts, ts)
        xs = _ffn(xs, gpf, sc2, sh2, w_gate_b, w_up_b, w_down_b, ga2, gpof, l, ts, ts)
        s_states.append((cmp_new.reshape(bs, NSA_KV, 1, 2, HEAD_DIM), sel_new.reshape(bs, NSA_KV, 1, 2, HEAD_DIM),
                         win_out.reshape(bs, NSA_KV, wb, 2, HEAD_DIM), s_fin, conv_fin))

    p_st = [jnp.stack(s, axis=1) for s in zip(*p_states)]
    s_st = [jnp.stack(s, axis=1) for s in zip(*s_states)]
    return (xp.reshape(bp, t, d), xs[:bs].reshape(bs, 1, d), *p_st, *s_st)
```

```python
import functools
import math

import jax
import jax.numpy as jnp
from jax import lax
from jax.experimental import pallas as pl
from jax.experimental.pallas import tpu as pltpu

F32 = jnp.float32
BF16 = jnp.bfloat16

HEAD_DIM = 128
NSA_HEADS = 8
NSA_KV = 2
NSA_HPG = NSA_HEADS // NSA_KV
GDN_HEADS = 8
ROT_DIM = HEAD_DIM // 4
ROPE_THETA = 500000.0
BLOCK = 64
TOPN = 16
WINDOW = 512
CMP_HID = 256
CONV_W = 4
GDN_CHUNK = 64
EPS = 1e-6
KV_ROW = 2 * HEAD_DIM
BLK_ROWS = 2 * BLOCK
BLK_PITCH = BLK_ROWS + 8
NSA_Q_W = NSA_HEADS * HEAD_DIM
GDN_W = GDN_HEADS * HEAD_DIM
CONV_CH = 3 * GDN_W
SMALL_W = 2 * HEAD_DIM
A_COL = 12
B_COL = 20
OFF_CONV = 0
OFF_Z = CONV_CH
OFF_Q = OFF_Z + GDN_W
OFF_CMP = OFF_Q + NSA_Q_W
OFF_SEL = OFF_CMP + NSA_KV * KV_ROW
OFF_WIN = OFF_SEL + NSA_KV * KV_ROW
MAIN_W = OFF_WIN + NSA_KV * KV_ROW
VMEM_LIMIT = 48 * 1024 * 1024
NEG_INF = float("-inf")
SCALE = HEAD_DIM ** -0.5


def _cparams(sem):
    return pltpu.CompilerParams(dimension_semantics=sem, vmem_limit_bytes=VMEM_LIMIT)


def _dot(a, b):
    return jnp.dot(a.astype(BF16), b.astype(BF16), preferred_element_type=F32)


def _dot_nt(a, b):
    return lax.dot_general(a.astype(BF16), b.astype(BF16), (((1,), (1,)), ((), ())),
                           preferred_element_type=F32)


def _dot_exact(a, b):
    return jnp.dot(a, b, preferred_element_type=F32, precision=lax.Precision.HIGHEST)


def _dot_nt_exact(a, b):
    return lax.dot_general(a, b, (((1,), (1,)), ((), ())), preferred_element_type=F32,
                           precision=lax.Precision.HIGHEST)


def _sigmoid(x):
    return 1.0 / (1.0 + jnp.exp(-x))


def _silu(x):
    return x * _sigmoid(x)


def _rms(x, g):
    return x * lax.rsqrt(jnp.mean(x * x, axis=-1, keepdims=True) + EPS) * g


def _masked_softmax(s, mask):
    s = jnp.where(mask, s, NEG_INF)
    m = jnp.max(s, axis=-1, keepdims=True)
    m = jnp.where(m == NEG_INF, 0.0, m)
    e = jnp.where(mask, jnp.exp(s - m), 0.0)
    return e / jnp.maximum(jnp.sum(e, axis=-1, keepdims=True), 1e-30)


def _ada_kernel(c_ref, w_ref, b_ref, o_ref):
    o_ref[0] = _dot(_silu(c_ref[...]), w_ref[0]) + b_ref[0]


def _ada_mod(c_all, w_ada, b_ada, tn=1024):
    depth, d, n = w_ada.shape
    r = c_all.shape[0]
    return pl.pallas_call(
        _ada_kernel,
        out_shape=jax.ShapeDtypeStruct((depth, r, n), F32),
        grid=(depth, n // tn),
        in_specs=[pl.BlockSpec((r, d), lambda l, j: (0, 0)),
                  pl.BlockSpec((1, d, tn), lambda l, j: (l, 0, j)),
                  pl.BlockSpec((1, 1, tn), lambda l, j: (l, 0, j))],
        out_specs=pl.BlockSpec((1, r, tn), lambda l, j: (l, 0, j)),
        compiler_params=_cparams(("arbitrary", "arbitrary")),
        name="ada_mod",
    )(c_all, w_ada, b_ada.reshape(depth, 1, n))


def _inproj_kernel(x_ref, g_ref, sc_ref, sh_ref, w_ref, ws_ref, o_ref, os_ref, h_ref):
    @pl.when(pl.program_id(1) == 0)
    def _():
        h = _rms(x_ref[...], g_ref[...]) * (1.0 + sc_ref[0]) + sh_ref[0]
        hb = h.astype(BF16)
        h_ref[...] = hb
        os_ref[...] = jnp.dot(hb, ws_ref[0], preferred_element_type=F32)

    o_ref[...] = jnp.dot(h_ref[...], w_ref[0], preferred_element_type=F32)


def _mod_spec(mod, rows_per_mod, tm):
    d = mod.shape[-1]
    if mod.shape[1] == 1:
        return pl.BlockSpec((1, 1, d), lambda i, *_: ((i * tm) // rows_per_mod, 0, 0))
    return pl.BlockSpec((1, tm, d), lambda i, *_: (0, i, 0))


def _in_proj(x, g, sc, sh, w_main, w_small, l, rows_per_mod, tm, tn=MAIN_W // 4):
    m, d = x.shape
    return pl.pallas_call(
        _inproj_kernel,
        out_shape=(jax.ShapeDtypeStruct((m, MAIN_W), F32), jax.ShapeDtypeStruct((m, SMALL_W), F32)),
        grid=(m // tm, MAIN_W // tn),
        in_specs=[pl.BlockSpec((tm, d), lambda i, j: (i, 0)),
                  pl.BlockSpec((1, d), lambda i, j: (0, 0)),
                  _mod_spec(sc, rows_per_mod, tm), _mod_spec(sh, rows_per_mod, tm),
                  pl.BlockSpec((1, d, tn), lambda i, j: (l, 0, j)),
                  pl.BlockSpec((1, d, SMALL_W), lambda i, j: (l, 0, 0))],
        out_specs=(pl.BlockSpec((tm, tn), lambda i, j: (i, j)),
                   pl.BlockSpec((tm, SMALL_W), lambda i, j: (i, 0))),
        scratch_shapes=[pltpu.VMEM((tm, d), BF16)],
        compiler_params=_cparams(("arbitrary", "arbitrary")),
        name="in_proj",
    )(x, g, sc, sh, w_main, w_small)


def _rope(x, c, s1, s2):
    half = ROT_DIM // 2
    return x * c + pltpu.roll(x, HEAD_DIM - half, 1) * s1 + pltpu.roll(x, half, 1) * s2


ATT_W = 5 * HEAD_DIM


def _postproj_kernel(q_ref, cmp_ref, sel_ref, win_ref, c_ref, s1_ref, s2_ref,
                     qo_ref, cmpo_ref, selo_ref, wino_ref, att_ref):
    c, s1, s2 = c_ref[...], s1_ref[...], s2_ref[...]
    for h in range(NSA_HEADS):
        sl = slice(h * HEAD_DIM, (h + 1) * HEAD_DIM)
        qo_ref[:, sl] = _rope(q_ref[:, sl], c, s1, s2).astype(BF16)
    tt = c.shape[0]
    blk = (pl.program_id(1) * tt + lax.broadcasted_iota(jnp.int32, (tt, HEAD_DIM), 0)) // BLOCK
    onehot = jnp.where(lax.broadcasted_iota(jnp.int32, (tt, HEAD_DIM), 1) == blk, 1.0, 0.0).astype(BF16)
    for src, dst, col in ((cmp_ref, cmpo_ref, None), (sel_ref, selo_ref, 0), (win_ref, wino_ref, 3 * HEAD_DIM)):
        for g in range(NSA_KV):
            base = g * KV_ROW
            k = _rope(src[:, base:base + HEAD_DIM], c, s1, s2)
            v = src[:, base + HEAD_DIM:base + KV_ROW]
            dst[0, g, pl.ds(0, tt, stride=2), :] = k
            dst[0, g, pl.ds(1, tt, stride=2), :] = v
            if col == 0:
                att_ref[0, g, :, 0:HEAD_DIM] = k.astype(BF16)
                att_ref[0, g, :, HEAD_DIM:2 * HEAD_DIM] = onehot
                att_ref[0, g, :, 2 * HEAD_DIM:3 * HEAD_DIM] = v.astype(BF16)
            elif col is not None:
                att_ref[0, g, :, col:col + HEAD_DIM] = k.astype(BF16)
                att_ref[0, g, :, col + HEAD_DIM:col + 2 * HEAD_DIM] = v.astype(BF16)


def _post_proj(proj, rope_tabs, b, t, tt):
    nt = t // tt
    state = jax.ShapeDtypeStruct((b, NSA_KV, 2 * t, HEAD_DIM), F32)
    att = jax.ShapeDtypeStruct((b, NSA_KV, t, ATT_W), BF16)
    kvw = NSA_KV * KV_ROW
    tab_spec = pl.BlockSpec((tt, HEAD_DIM), lambda bi, ti: (ti, 0))
    st_spec = pl.BlockSpec((1, NSA_KV, 2 * tt, HEAD_DIM), lambda bi, ti: (bi, 0, ti, 0))
    att_spec = pl.BlockSpec((1, NSA_KV, tt, ATT_W), lambda bi, ti: (bi, 0, ti, 0))
    return pl.pallas_call(
        _postproj_kernel,
        out_shape=(jax.ShapeDtypeStruct((b * t, NSA_Q_W), BF16), state, state, state, att),
        grid=(b, nt),
        in_specs=[pl.BlockSpec((tt, NSA_Q_W), lambda bi, ti: (bi * nt + ti, OFF_Q // NSA_Q_W)),
                  pl.BlockSpec((tt, kvw), lambda bi, ti: (bi * nt + ti, OFF_CMP // kvw)),
                  pl.BlockSpec((tt, kvw), lambda bi, ti: (bi * nt + ti, OFF_SEL // kvw)),
                  pl.BlockSpec((tt, kvw), lambda bi, ti: (bi * nt + ti, OFF_WIN // kvw)),
                  tab_spec, tab_spec, tab_spec],
        out_specs=(pl.BlockSpec((tt, NSA_Q_W), lambda bi, ti: (bi * nt + ti, 0)),
                   st_spec, st_spec, st_spec, att_spec),
        compiler_params=_cparams(("arbitrary", "arbitrary")),
        name="post_proj",
    )(proj, proj, proj, proj, *rope_tabs)


def _rope_tables(pos):
    half = ROT_DIM // 2
    inv = ROPE_THETA ** (-jnp.arange(half, dtype=F32) * 2.0 / ROT_DIM)
    ang = pos.astype(F32)[:, None] * inv
    cos, sin = jnp.cos(ang), jnp.sin(ang)
    n = pos.shape[0]
    rest = HEAD_DIM - ROT_DIM
    c = jnp.concatenate([cos, cos, jnp.ones((n, rest), F32)], axis=1)
    s1 = jnp.concatenate([-sin, jnp.zeros((n, HEAD_DIM - half), F32)], axis=1)
    s2 = jnp.concatenate([jnp.zeros((n, half), F32), sin, jnp.zeros((n, rest), F32)], axis=1)
    return c, s1, s2


def _gelu_tanh(x):
    return 0.5 * x * (1.0 + jnp.tanh(math.sqrt(2.0 / math.pi) * (x + 0.044715 * (x * x * x))))


def _compress_blocks(x_ref, base, pitch, nblk, pe_ref, w1k_ref, w1v_ref, w2k_ref, w2v_ref, ko_ref, vo_ref):
    hk = jnp.zeros((nblk, CMP_HID), F32)
    hv = jnp.zeros((nblk, CMP_HID), F32)
    for r in range(BLOCK):
        xk = x_ref[pl.ds(base + 2 * r, nblk, stride=pitch), :] + pe_ref[0, r:r + 1, :]
        xv = x_ref[pl.ds(base + 2 * r + 1, nblk, stride=pitch), :] + pe_ref[1, r:r + 1, :]
        hk = hk + _dot(xk, w1k_ref[r])
        hv = hv + _dot(xv, w1v_ref[r])
    ko_ref[...] = _dot(_gelu_tanh(hk), w2k_ref[...])
    vo_ref[...] = _dot(_gelu_tanh(hv), w2v_ref[...])


def _compress_kernel(x_ref, *rest):
    _compress_blocks(x_ref, 0, BLK_ROWS, x_ref.shape[0] // BLK_ROWS, *rest)


def _compress_weights(cmp_pe, cmp_w1, cmp_w2, l):
    w1 = cmp_w1[l].astype(BF16).reshape(2, BLOCK, HEAD_DIM, CMP_HID)
    w2 = cmp_w2[l].astype(BF16)
    return cmp_pe[l], w1[0], w1[1], w2[0], w2[1]


def _cw_specs(nargs):
    z = (0,) * 3
    return [pl.BlockSpec((2, BLOCK, HEAD_DIM), lambda *a: z),
            pl.BlockSpec((BLOCK, HEAD_DIM, CMP_HID), lambda *a: z),
            pl.BlockSpec((BLOCK, HEAD_DIM, CMP_HID), lambda *a: z),
            pl.BlockSpec((CMP_HID, HEAD_DIM), lambda *a: (0, 0)),
            pl.BlockSpec((CMP_HID, HEAD_DIM), lambda *a: (0, 0))]


def _compress(x_rows, cw, tr):
    nblocks = x_rows.shape[0] // BLK_ROWS
    out = jax.ShapeDtypeStruct((nblocks, HEAD_DIM), F32)
    return pl.pallas_call(
        _compress_kernel,
        out_shape=(out, out),
        grid=(nblocks // tr,),
        in_specs=[pl.BlockSpec((tr * BLK_ROWS, HEAD_DIM), lambda i: (i, 0))] + _cw_specs(1),
        out_specs=(pl.BlockSpec((tr, HEAD_DIM), lambda i: (i, 0)),) * 2,
        compiler_params=_cparams(("arbitrary",)),
        name="compress",
    )(x_rows, *cw)


MAX_PAGES_PER_STEP = 32


def _compress_paged_kernel(pt_ref, cache_ref, pe_ref, w1k_ref, w1v_ref, w2k_ref, w2v_ref, ko_ref, vo_ref,
                           xbuf, sem, *, l):
    bpp = cache_ref.shape[2]
    nblk = ko_ref.shape[0]
    pages_per_step = nblk // bpp
    ns = pl.num_programs(1)
    step = pl.program_id(0) * ns + pl.program_id(1)
    total = pl.num_programs(0) * ns

    def copies(st, slot):
        bi, s = st // ns, st % ns
        out = []
        for k in range(pages_per_step):
            pg = pt_ref[bi, s * pages_per_step + k]
            for j in range(bpp):
                row0 = pl.multiple_of(slot * (nblk * BLK_PITCH) + (k * bpp + j) * BLK_PITCH, 8)
                out.append(pltpu.make_async_copy(cache_ref.at[pg, l, j], xbuf.at[pl.ds(row0, BLK_ROWS), :],
                                                 sem.at[slot]))
        return out

    slot = step % 2

    @pl.when(step == 0)
    def _():
        for c in copies(step, slot):
            c.start()

    @pl.when(step + 1 < total)
    def _():
        for c in copies(step + 1, 1 - slot):
            c.start()

    for c in copies(step, slot):
        c.wait()
    _compress_blocks(xbuf, slot * (nblk * BLK_PITCH), BLK_PITCH, nblk,
                     pe_ref, w1k_ref, w1v_ref, w2k_ref, w2v_ref, ko_ref, vo_ref)


def _compress_paged(cache, page_table, cw, l):
    b, n_pages = page_table.shape
    bpp = cache.shape[2]
    pages_per_step = math.gcd(n_pages, MAX_PAGES_PER_STEP)
    steps = n_pages // pages_per_step
    tr = pages_per_step * bpp
    out = jax.ShapeDtypeStruct((b * steps * tr, HEAD_DIM), F32)
    grid_spec = pltpu.PrefetchScalarGridSpec(
        num_scalar_prefetch=1,
        grid=(b, steps),
        in_specs=[pl.BlockSpec(memory_space=pl.ANY)] + _cw_specs(3),
        out_specs=(pl.BlockSpec((tr, HEAD_DIM), lambda bi, s, pt: (bi * steps + s, 0)),) * 2,
        scratch_shapes=[pltpu.VMEM((2 * tr * BLK_PITCH, HEAD_DIM), F32), pltpu.SemaphoreType.DMA((2,))],
    )
    return pl.pallas_call(
        functools.partial(_compress_paged_kernel, l=l),
        out_shape=(out, out),
        grid_spec=grid_spec,
        compiler_params=_cparams(("arbitrary", "arbitrary")),
        name="compress_paged",
    )(page_table, cache, *cw)


def _stable_topn_mask(score, n_sel):
    rows, nb = score.shape
    tn = (((0,), (0,)), ((), ()))
    eye_r = jnp.where(lax.broadcasted_iota(jnp.int32, (rows, rows), 0)
                      == lax.broadcasted_iota(jnp.int32, (rows, rows), 1), 1.0, 0.0)
    st = lax.dot_general(score, eye_r, tn, preferred_element_type=F32, precision=lax.Precision.HIGHEST)
    jrow = lax.broadcasted_iota(jnp.int32, (nb, rows), 0)
    rank = jnp.zeros((nb, rows), F32)
    for k in range(nb):
        ck = st[k:k + 1, :]
        beats = (ck > st) | ((ck == st) & (jrow > k))
        rank = rank + jnp.where(beats, 1.0, 0.0)
    sel_t = jnp.where(rank < n_sel, 1.0, 0.0).astype(BF16)
    eye_n = jnp.where(lax.broadcasted_iota(jnp.int32, (nb, nb), 0)
                      == lax.broadcasted_iota(jnp.int32, (nb, nb), 1), 1.0, 0.0).astype(BF16)
    return lax.dot_general(sel_t, eye_n, tn, preferred_element_type=F32)


MASK_BIAS = -(2.0 ** 100)
SCORE_C = SCALE * 1.4426950408889634


def _softmax_pv(parts):
    m = jnp.max(parts[0][0], axis=-1, keepdims=True)
    for s, _ in parts[1:]:
        m = jnp.maximum(m, jnp.max(s, axis=-1, keepdims=True))
    den, acc = 0.0, 0.0
    for s, v in parts:
        e = jnp.exp2(s - m)
        den = den + jnp.sum(e, axis=-1, keepdims=True)
        acc = acc + jnp.dot(e.astype(BF16), v, preferred_element_type=F32)
    return acc / den


def _nsa_prompt_kernel(q_ref, sm_ref, kcb_ref, vcb_ref, att_ref, o_ref, os_ref, *, n_sel, wl, n_cls):
    i = pl.program_id(2)
    t_all = att_ref.shape[2]
    nb = kcb_ref.shape[2]
    rows = NSA_HPG * BLOCK
    q = q_ref[...]
    qs = jnp.concatenate([q[:, h * HEAD_DIM:(h + 1) * HEAD_DIM] for h in range(NSA_HPG)], axis=0)
    q0 = i * BLOCK
    tq = lax.broadcasted_iota(jnp.int32, (rows, 1), 0) & (BLOCK - 1)
    qpos = q0 + tq

    start = pl.multiple_of(jnp.clip(q0 - WINDOW, 0, t_all - wl), BLOCK)
    wpos = start + lax.broadcasted_iota(jnp.int32, (rows, wl), 1)
    wbias = jnp.where((wpos <= qpos) & (wpos >= qpos - WINDOW), 0.0, MASK_BIAS)
    s_w = lax.dot_general(qs, att_ref[0, 0, pl.ds(start, wl), 3 * HEAD_DIM:4 * HEAD_DIM], (((1,), (1,)), ((), ())),
                          preferred_element_type=F32) * SCORE_C + wbias
    o_w = _softmax_pv([(s_w, att_ref[0, 0, pl.ds(start, wl), 4 * HEAD_DIM:5 * HEAD_DIM])])

    jc = lax.broadcasted_iota(jnp.int32, (rows, nb), 1)
    cmask = (jc * BLOCK + (BLOCK - 1)) <= qpos
    p_c = _masked_softmax(_dot_nt(qs, kcb_ref[0, 0]) * SCALE, cmask)
    o_c = _dot(p_c, vcb_ref[0, 0])

    imp = p_c[0:BLOCK]
    for h in range(1, NSA_HPG):
        imp = imp + p_c[h * BLOCK:(h + 1) * BLOCK]
    jb = lax.broadcasted_iota(jnp.int32, (BLOCK, nb), 1)
    forced = (jb == 0) | (jb == i) | (jb == i - 1)
    score = jnp.where(forced, 2.0 * NSA_HPG, jnp.where(jb > i, -1.0, imp))
    sel = _stable_topn_mask(score, n_sel)
    blk_bias = jnp.where((jb < i) & (sel > 0.5), 0.0, MASK_BIAS)
    blk_bias = jnp.concatenate([blk_bias, jnp.zeros((BLOCK, HEAD_DIM - nb), F32)], axis=1).astype(BF16)
    q_aug = jnp.concatenate([qs, jnp.concatenate([blk_bias] * NSA_HPG, axis=0)], axis=1)
    r0 = pl.multiple_of(q0, BLOCK)
    tl = lax.broadcasted_iota(jnp.int32, (rows, BLOCK), 1)
    s_d = (lax.dot_general(qs, att_ref[0, 0, pl.ds(r0, BLOCK), 0:HEAD_DIM], (((1,), (1,)), ((), ())),
                           preferred_element_type=F32) * SCORE_C + jnp.where(tl <= tq, 0.0, MASK_BIAS))
    v_d = att_ref[0, 0, pl.ds(r0, BLOCK), 2 * HEAD_DIM:3 * HEAD_DIM]
    per_cls = nb // n_cls
    for c in range(n_cls):
        nk = (c + 1) * per_cls * BLOCK

        @pl.when((i >= c * per_cls) & (i < (c + 1) * per_cls))
        def _():
            s_b = lax.dot_general(q_aug, att_ref[0, 0, 0:nk, 0:2 * HEAD_DIM], (((1,), (1,)), ((), ())),
                                  preferred_element_type=F32) * SCORE_C
            os_ref[...] = _softmax_pv([(s_b, att_ref[0, 0, 0:nk, 2 * HEAD_DIM:3 * HEAD_DIM]), (s_d, v_d)])

    o_s = os_ref[...]

    gt = _sigmoid(sm_ref[...])
    for h in range(NSA_HPG):
        r = slice(h * BLOCK, (h + 1) * BLOCK)
        o = (gt[:, 3 * h:3 * h + 1] * o_c[r] + gt[:, 3 * h + 1:3 * h + 2] * o_s[r]
             + gt[:, 3 * h + 2:3 * h + 3] * o_w[r])
        o_ref[:, h * HEAD_DIM:(h + 1) * HEAD_DIM] = o.astype(BF16)


def _nsa_prompt(q_rot, small, kcb, vcb, att, b, t):
    nq = t // BLOCK
    nb = kcb.shape[2]
    assert nb <= HEAD_DIM
    gw = NSA_HPG * HEAD_DIM
    wl = min(WINDOW + BLOCK, t)
    cb_spec = pl.BlockSpec((1, 1, nb, HEAD_DIM), lambda bi, g, i: (bi, g, 0, 0))
    return pl.pallas_call(
        functools.partial(_nsa_prompt_kernel, n_sel=min(TOPN, nb), wl=wl, n_cls=math.gcd(nb, 8)),
        out_shape=jax.ShapeDtypeStruct((b * t, NSA_Q_W), BF16),
        grid=(b, NSA_KV, nq),
        in_specs=[pl.BlockSpec((BLOCK, gw), lambda bi, g, i: (bi * nq + i, g)),
                  pl.BlockSpec((BLOCK, HEAD_DIM), lambda bi, g, i: (bi * nq + i, g)),
                  cb_spec, cb_spec,
                  pl.BlockSpec((1, 1, t, ATT_W), lambda bi, g, i: (bi, g, 0, 0))],
        out_specs=pl.BlockSpec((BLOCK, gw), lambda bi, g, i: (bi * nq + i, g)),
        scratch_shapes=[pltpu.VMEM((NSA_HPG * BLOCK, HEAD_DIM), F32)],
        compiler_params=_cparams(("arbitrary", "arbitrary", "arbitrary")),
        name="nsa_prompt",
    )(q_rot, small, kcb, vcb, att)


GDN_SOLVE_BLOCKS = 4


def _softplus(x):
    return jnp.maximum(x, 0.0) + jnp.log(1.0 + jnp.exp(-jnp.abs(x)))


def _unit_lower_solve(l_strict, rhs):
    n = l_strict[0].shape[0]
    sub = 8
    nt = n // sub
    tiles = [[r[t * sub:(t + 1) * sub, :] for t in range(nt)] for r in rhs]
    for i in range(n - 1):
        t0, s = divmod(i, sub)
        for p, lm in enumerate(l_strict):
            xi = tiles[p][t0][s:s + 1, :]
            col = lm[:, i:i + 1]
            for t in range(t0, nt):
                tiles[p][t] = tiles[p][t] - col[t * sub:(t + 1) * sub, :] * xi
    return [jnp.concatenate(tp, axis=0) for tp in tiles]


def _gdn_prompt_kernel(x_ref, z_ref, sm_ref, abr_ref, cw_ref, prow_ref, pcol_ref, nw_ref,
                       o_ref, s_out_ref, conv_out_ref, s_ref, xe_ref):
    n = pl.program_id(1)
    c = GDN_CHUNK
    pad = 8

    @pl.when(n == 0)
    def _():
        s_ref[...] = jnp.zeros_like(s_ref)
        xe_ref[0:pad, :] = jnp.zeros((pad, CONV_CH), F32)

    xe_ref[pad:pad + c, :] = x_ref[...]
    y = xe_ref[pl.ds(pad, c), :] * cw_ref[CONV_W - 1:CONV_W, :]
    for w in range(CONV_W - 1):
        y = y + xe_ref[pl.ds(pad - (CONV_W - 1) + w, c), :] * cw_ref[w:w + 1, :]
    y = _silu(y)

    ri = lax.broadcasted_iota(jnp.int32, (c, c), 0)
    ci = lax.broadcasted_iota(jnp.int32, (c, c), 1)
    tri = ri >= ci
    strict = ri > ci
    tri_f = jnp.where(tri, 1.0, 0.0)

    a_col = sm_ref[:, A_COL:A_COL + GDN_HEADS]
    b_col = sm_ref[:, B_COL:B_COL + GDN_HEADS]
    g_col = -jnp.exp(prow_ref[0:1, :]) * _softplus(a_col + prow_ref[1:2, :])
    g_row = -jnp.exp(pcol_ref[:, 0:1]) * _softplus(abr_ref[0, 0, 0:GDN_HEADS, :] + pcol_ref[:, 1:2])
    gc_col = _dot_exact(tri_f, g_col)
    gc_row = _dot_nt_exact(g_row, tri_f)
    beta_col = _sigmoid(b_col)

    pre = []
    for h in range(GDN_HEADS):
        sl = slice(h * HEAD_DIM, (h + 1) * HEAD_DIM)
        yq, yk, v = y[:, sl], y[:, GDN_W + h * HEAD_DIM:GDN_W + (h + 1) * HEAD_DIM], \
            y[:, 2 * GDN_W + h * HEAD_DIM:2 * GDN_W + (h + 1) * HEAD_DIM]
        q = yq * lax.rsqrt(jnp.sum(yq * yq, axis=-1, keepdims=True) + EPS) * SCALE
        k = yk * lax.rsqrt(jnp.sum(yk * yk, axis=-1, keepdims=True) + EPS)
        gcc = gc_col[:, h:h + 1]
        beta = beta_col[:, h:h + 1]
        decay = jnp.exp(jnp.where(tri, gcc - gc_row[h:h + 1, :], NEG_INF))
        kb = k * beta
        lmat = jnp.where(strict, _dot_nt(kb, k) * decay, 0.0)
        rhs = jnp.concatenate([v * beta, kb * jnp.exp(gcc)], axis=1)
        pre.append((q, k, gcc, _dot_nt(q, k) * decay, lmat, rhs))

    nblk = GDN_SOLVE_BLOCKS
    rb = c // nblk
    l_list, r_list = [], []
    for p in pre:
        lmat, rhs = p[4], p[5]
        for b in range(nblk):
            r0 = b * rb
            l_list.append(lmat[r0:r0 + rb, r0:r0 + rb])
            if b == 0:
                r_list.append(rhs[0:rb, :])
            else:
                r_list.append(jnp.concatenate([rhs[r0:r0 + rb, :], lmat[r0:r0 + rb, 0:r0],
                                               jnp.zeros((rb, HEAD_DIM - r0), F32)], axis=1))
    part = _unit_lower_solve(l_list, r_list)
    xs = [[part[h * nblk][:, 0:2 * HEAD_DIM]] for h in range(GDN_HEADS)]
    for b in range(1, nblk):
        for h in range(GDN_HEADS):
            y = part[h * nblk + b]
            xs[h].append(y[:, 0:2 * HEAD_DIM]
                         - _dot_exact(y[:, 2 * HEAD_DIM:2 * HEAD_DIM + b * rb], jnp.concatenate(xs[h], axis=0)))
    sols = [jnp.concatenate(x, axis=0) for x in xs]

    s_out, o_out = [], []
    for h in range(GDN_HEADS):
        q, k, gcc, qk, _, _ = pre[h]
        u, w = sols[h][:, :HEAD_DIM], sols[h][:, HEAD_DIM:]
        s = s_ref[h]
        v_new = u - _dot(w, s)
        o = _dot(q * jnp.exp(gcc), s) + _dot(qk, v_new)
        g_last = gcc[c - 1:c, :]
        k_dec = k * jnp.exp(g_last - gcc)
        s_out.append(s * jnp.exp(g_last) + lax.dot_general(
            k_dec.astype(BF16), v_new.astype(BF16), (((0,), (0,)), ((), ())), preferred_element_type=F32))
        o_out.append((_rms(o, nw_ref[...]) * _silu(z_ref[:, h * HEAD_DIM:(h + 1) * HEAD_DIM])).astype(BF16))
    for h in range(GDN_HEADS):
        s_ref[h] = s_out[h]
        o_ref[:, h * HEAD_DIM:(h + 1) * HEAD_DIM] = o_out[h]
    xe_ref[0:pad, :] = xe_ref[c:c + pad, :]

    @pl.when(n == pl.num_programs(1) - 1)
    def _():
        s_out_ref[0] = s_ref[...]
        conv_out_ref[0] = xe_ref[pad - (CONV_W - 1):pad, :]


def _gdn_prompt(proj, small, conv_w, a_log, dt_bias, norm_w, b, t):
    c = GDN_CHUNK
    nc = t // c
    ab_row = small.reshape(b, nc, c, SMALL_W)[..., A_COL:A_COL + 2 * GDN_HEADS].transpose(0, 1, 3, 2)
    prow = jnp.stack([a_log, dt_bias], axis=0)
    pcol = prow.T
    return pl.pallas_call(
        _gdn_prompt_kernel,
        out_shape=(jax.ShapeDtypeStruct((b * t, GDN_W), BF16),
                   jax.ShapeDtypeStruct((b, GDN_HEADS, HEAD_DIM, HEAD_DIM), F32),
                   jax.ShapeDtypeStruct((b, CONV_W - 1, CONV_CH), F32)),
        grid=(b, nc),
        in_specs=[pl.BlockSpec((c, CONV_CH), lambda bi, n: (bi * nc + n, OFF_CONV // CONV_CH)),
                  pl.BlockSpec((c, GDN_W), lambda bi, n: (bi * nc + n, OFF_Z // GDN_W)),
                  pl.BlockSpec((c, HEAD_DIM), lambda bi, n: (bi * nc + n, 0)),
                  pl.BlockSpec((1, 1, 2 * GDN_HEADS, c), lambda bi, n: (bi, n, 0, 0)),
                  pl.BlockSpec((CONV_W, CONV_CH), lambda bi, n: (0, 0)),
                  pl.BlockSpec((2, GDN_HEADS), lambda bi, n: (0, 0)),
                  pl.BlockSpec((GDN_HEADS, 2), lambda bi, n: (0, 0)),
                  pl.BlockSpec((1, HEAD_DIM), lambda bi, n: (0, 0))],
        out_specs=(pl.BlockSpec((c, GDN_W), lambda bi, n: (bi * nc + n, 0)),
                   pl.BlockSpec((1, GDN_HEADS, HEAD_DIM, HEAD_DIM), lambda bi, n: (bi, 0, 0, 0)),
                   pl.BlockSpec((1, CONV_W - 1, CONV_CH), lambda bi, n: (bi, 0, 0))),
        scratch_shapes=[pltpu.VMEM((GDN_HEADS, HEAD_DIM, HEAD_DIM), F32),
                        pltpu.VMEM((c + 16, CONV_CH), F32)],
        compiler_params=_cparams(("arbitrary", "arbitrary")),
        name="gdn_prompt",
    )(proj, proj, small, ab_row, conv_w, prow, pcol, norm_w.reshape(1, HEAD_DIM))


def _outproj_kernel(on_ref, og_ref, w_ref, x_ref, ga_ref, g_ref, o_ref):
    o = jnp.concatenate([on_ref[...], og_ref[...]], axis=1)
    mix = jnp.dot(o, w_ref[0], preferred_element_type=F32)
    o_ref[...] = x_ref[...] + ga_ref[0] * _rms(mix, g_ref[...])


def _out_proj(o_nsa, o_gdn, w_out, x, ga, g, l, rows_per_mod, tm):
    m, d = x.shape
    kw = o_nsa.shape[1]
    return pl.pallas_call(
        _outproj_kernel,
        out_shape=jax.ShapeDtypeStruct((m, d), F32),
        grid=(m // tm,),
        in_specs=[pl.BlockSpec((tm, kw), lambda i: (i, 0)),
                  pl.BlockSpec((tm, kw), lambda i: (i, 0)),
                  pl.BlockSpec((1, 2 * kw, d), lambda i: (l, 0, 0)),
                  pl.BlockSpec((tm, d), lambda i: (i, 0)),
                  _mod_spec(ga, rows_per_mod, tm),
                  pl.BlockSpec((1, d), lambda i: (0, 0))],
        out_specs=pl.BlockSpec((tm, d), lambda i: (i, 0)),
        compiler_params=_cparams(("arbitrary",)),
        name="out_proj",
    )(o_nsa, o_gdn, w_out, x, ga, g)


def _ffn_kernel(x_ref, g1_ref, sc_ref, sh_ref, wg_ref, wu_ref, wd_ref, ga_ref, g2_ref, o_ref, h_ref, acc_ref):
    j = pl.program_id(1)

    @pl.when(j == 0)
    def _():
        h = _rms(x_ref[...], g1_ref[...]) * (1.0 + sc_ref[0]) + sh_ref[0]
        h_ref[...] = h.astype(BF16)
        acc_ref[...] = jnp.zeros_like(acc_ref)

    h = h_ref[...]
    a = jnp.dot(h, wg_ref[0], preferred_element_type=F32)
    u = jnp.dot(h, wu_ref[0], preferred_element_type=F32)
    acc_ref[...] += _dot(_silu(a) * u, wd_ref[0])

    @pl.when(j == pl.num_programs(1) - 1)
    def _():
        o_ref[...] = x_ref[...] + ga_ref[0] * _rms(acc_ref[...], g2_ref[...])


def _ffn(x, g1, sc, sh, w_gate, w_up, w_down, ga, g2, l, rows_per_mod, tm, tf=512):
    m, d = x.shape
    f = w_gate.shape[2]
    vec = pl.BlockSpec((1, d), lambda i, j: (0, 0))
    return pl.pallas_call(
        _ffn_kernel,
        out_shape=jax.ShapeDtypeStruct((m, d), F32),
        grid=(m // tm, f // tf),
        in_specs=[pl.BlockSpec((tm, d), lambda i, j: (i, 0)), vec,
                  _mod_spec(sc, rows_per_mod, tm), _mod_spec(sh, rows_per_mod, tm),
                  pl.BlockSpec((1, d, tf), lambda i, j: (l, 0, j)),
                  pl.BlockSpec((1, d, tf), lambda i, j: (l, 0, j)),
                  pl.BlockSpec((1, tf, d), lambda i, j: (l, j, 0)),
                  _mod_spec(ga, rows_per_mod, tm), vec],
        out_specs=pl.BlockSpec((tm, d), lambda i, j: (i, 0)),
        scratch_shapes=[pltpu.VMEM((tm, d), BF16), pltpu.VMEM((tm, d), F32)],
        compiler_params=_cparams(("arbitrary", "arbitrary")),
        name="ffn",
    )(x, g1, sc, sh, w_gate, w_up, w_down, ga, g2)


def _stack_heads(q, h0):
    rows = [q[:, (h0 + h) * HEAD_DIM:(h0 + h + 1) * HEAD_DIM] for h in range(NSA_HPG)]
    return jnp.concatenate(rows + [jnp.zeros((NSA_HPG, HEAD_DIM), F32)], axis=0)


def _nsa_sample_cmp_kernel(q_ref, kcb_ref, vcb_ref, oc_ref, idx_ref, *, n_sel, q_pos):
    nbp = kcb_ref.shape[2]
    cur = q_pos // BLOCK
    q = q_ref[0].astype(F32)
    for g in range(NSA_KV):
        qs = _stack_heads(q, g * NSA_HPG)
        jc = lax.broadcasted_iota(jnp.int32, (qs.shape[0], nbp), 1)
        cmask = (jc * BLOCK + (BLOCK - 1)) <= q_pos
        p_c = _masked_softmax(_dot_nt(qs, kcb_ref[0, g]) * SCALE, cmask)
        o_c = _dot(p_c, vcb_ref[0, g])
        for h in range(NSA_HPG):
            c0 = (g * NSA_HPG + h) * HEAD_DIM
            oc_ref[0, :, c0:c0 + HEAD_DIM] = o_c[h:h + 1]
        imp = p_c[0:1]
        for h in range(1, NSA_HPG):
            imp = imp + p_c[h:h + 1]
        jb = jc[0:1]
        forced = (jb == 0) | (jb == cur) | (jb == cur - 1)
        score = jnp.where(forced, jnp.inf, jnp.where(jb > cur, NEG_INF, imp))
        left = jb >= 0
        for r in range(n_sel):
            best = jnp.max(jnp.where(left, score, NEG_INF), axis=1, keepdims=True)
            pick = jnp.min(jnp.where(left & (score == best), jb, nbp), axis=1, keepdims=True)
            idx_ref[0, g, r:r + 1, :] = jnp.broadcast_to(pick, (1, HEAD_DIM))
            left = left & (jb != pick)


def _nsa_sample_cmp(q_rot, kcb, vcb, q_pos, n_blocks):
    b = q_rot.shape[0]
    nbp = kcb.shape[2]
    n_sel = min(TOPN, n_blocks)
    cb_spec = pl.BlockSpec((1, NSA_KV, nbp, HEAD_DIM), lambda bi: (bi, 0, 0, 0))
    return pl.pallas_call(
        functools.partial(_nsa_sample_cmp_kernel, n_sel=n_sel, q_pos=q_pos),
        out_shape=(jax.ShapeDtypeStruct((b, 1, NSA_Q_W), F32),
                   jax.ShapeDtypeStruct((b, NSA_KV, n_sel, HEAD_DIM), jnp.int32)),
        grid=(b,),
        in_specs=[pl.BlockSpec((1, 1, NSA_Q_W), lambda bi: (bi, 0, 0)), cb_spec, cb_spec],
        out_specs=(pl.BlockSpec((1, 1, NSA_Q_W), lambda bi: (bi, 0, 0)),
                   pl.BlockSpec((1, NSA_KV, n_sel, HEAD_DIM), lambda bi: (bi, 0, 0, 0))),
        compiler_params=_cparams(("arbitrary",)),
        name="nsa_sample_cmp",
    )(q_rot.reshape(b, 1, NSA_Q_W), kcb, vcb)


def _nsa_sample_attn_kernel(idx_ref, pt_ref, q_ref, sm_ref, oc_ref, cache_ref, newsel_ref, win_ref, newwin_ref,
                            o_ref, wino_ref, blk_buf, sem, *, l, npb, n_sel):
    bi, g = pl.program_id(0), pl.program_id(1)
    bpp = cache_ref.shape[3]
    copies, picks = [], []
    for n in range(n_sel):
        pick = idx_ref[bi, g, n]
        jp = jnp.minimum(pick, npb - 1)
        cp = pltpu.make_async_copy(cache_ref.at[pt_ref[bi, jp // bpp], l, g, jp % bpp],
                                   blk_buf.at[pl.ds(n * BLK_ROWS, BLK_ROWS), :], sem)
        cp.start()
        copies.append(cp)
        picks.append(pick)

    qs = _stack_heads(q_ref[0].astype(F32), 0)

    def row_scores(k):
        return jnp.sum(qs * k.astype(BF16).astype(F32), axis=-1, keepdims=True) * SCALE

    wb = win_ref.shape[3] // 2
    s_buf = _dot_nt(qs, win_ref[0, 0, 0, pl.ds(0, wb, stride=2), :]) * SCALE
    s_new = row_scores(newwin_ref[0, 0, 0:1, :])
    m = jnp.maximum(jnp.max(s_buf, axis=-1, keepdims=True), s_new)
    e_buf, e_new = jnp.exp(s_buf - m), jnp.exp(s_new - m)
    den = jnp.maximum(jnp.sum(e_buf, axis=-1, keepdims=True) + e_new, 1e-30)
    o_w = (_dot(e_buf, win_ref[0, 0, 0, pl.ds(1, wb, stride=2), :])
           + e_new * newwin_ref[0, 0, 1:2, :].astype(BF16).astype(F32)) / den
    wino_ref[0, 0, 0:2 * wb - 2, :] = win_ref[0, 0, 0, 2:2 * wb, :]
    wino_ref[0, 0, 2 * wb - 2:2 * wb, :] = newwin_ref[0, 0]

    for cp in copies:
        cp.wait()
    nk = n_sel * BLOCK
    slot = lax.broadcasted_iota(jnp.int32, (1, nk), 1) // BLOCK
    bias = jnp.zeros((1, nk), F32)
    has_new = picks[0] >= npb
    for n in range(n_sel):
        bias = jnp.where((slot == n) & (picks[n] >= npb), NEG_INF, bias)
        has_new = has_new | (picks[n] >= npb)
    s_blk = _dot_nt(qs, blk_buf[pl.ds(0, nk, stride=2), :]) * SCALE + bias
    s_row = jnp.where(has_new, row_scores(newsel_ref[0, 0, 0:1, :]), NEG_INF)
    m = jnp.maximum(jnp.max(s_blk, axis=-1, keepdims=True), s_row)
    m = jnp.where(m == NEG_INF, 0.0, m)
    e_blk, e_row = jnp.exp(s_blk - m), jnp.exp(s_row - m)
    den = jnp.maximum(jnp.sum(e_blk, axis=-1, keepdims=True) + e_row, 1e-30)
    o_s = (_dot(e_blk, blk_buf[pl.ds(1, nk, stride=2), :])
           + e_row * newsel_ref[0, 0, 1:2, :].astype(BF16).astype(F32)) / den

    gt = _sigmoid(sm_ref[0])
    for h in range(NSA_HPG):
        c0 = h * HEAD_DIM
        o = (gt[:, 3 * h:3 * h + 1] * oc_ref[0, :, c0:c0 + HEAD_DIM] + gt[:, 3 * h + 1:3 * h + 2] * o_s[h:h + 1]
             + gt[:, 3 * h + 2:3 * h + 3] * o_w[h:h + 1])
        o_ref[0, :, c0:c0 + HEAD_DIM] = o.astype(BF16)


def _nsa_sample_attn(idx, page_table, q_rot, small, o_c, cache_sel, new_sel, cache_win, new_win, l, npb):
    b = q_rot.shape[0]
    n_sel = idx.shape[2]
    bpp = cache_sel.shape[3]
    wb2 = cache_win.shape[3]
    gw = NSA_HPG * HEAD_DIM

    new_spec = pl.BlockSpec((1, 1, 2, HEAD_DIM), lambda bi, g, *_: (bi, g, 0, 0))
    grid_spec = pltpu.PrefetchScalarGridSpec(
        num_scalar_prefetch=2,
        grid=(b, NSA_KV),
        in_specs=[pl.BlockSpec((1, 1, gw), lambda bi, g, *_: (bi, 0, g)),
                  pl.BlockSpec((1, 1, HEAD_DIM), lambda bi, g, *_: (bi, 0, g)),
                  pl.BlockSpec((1, 1, gw), lambda bi, g, *_: (bi, 0, g)),
                  pl.BlockSpec(memory_space=pl.ANY),
                  new_spec,
                  pl.BlockSpec((1, 1, 1, wb2, HEAD_DIM), lambda bi, g, *_: (bi, l, g, 0, 0)),
                  new_spec],
        out_specs=(pl.BlockSpec((1, 1, gw), lambda bi, g, *_: (bi, 0, g)),
                   pl.BlockSpec((1, 1, wb2, HEAD_DIM), lambda bi, g, *_: (bi, g, 0, 0))),
        scratch_shapes=[pltpu.VMEM((n_sel * BLK_ROWS, HEAD_DIM), F32), pltpu.SemaphoreType.DMA(())],
    )
    return pl.pallas_call(
        functools.partial(_nsa_sample_attn_kernel, l=l, npb=npb, n_sel=n_sel),
        out_shape=(jax.ShapeDtypeStruct((b, 1, NSA_Q_W), BF16),
                   jax.ShapeDtypeStruct((b, NSA_KV, wb2, HEAD_DIM), F32)),
        grid_spec=grid_spec,
        compiler_params=_cparams(("arbitrary", "arbitrary")),
        name="nsa_sample_attn",
    )(idx, page_table, q_rot.reshape(b, 1, NSA_Q_W), small.reshape(b, 1, SMALL_W), o_c,
      cache_sel, new_sel, cache_win, new_win)


def _gdn_sample_kernel(x_ref, z_ref, sm_ref, buf_ref, s0_ref, cw_ref, prow_ref, nw_ref, eye_ref,
                       o_ref, s_out_ref, conv_out_ref):
    x = x_ref[0]
    buf = buf_ref[0, 0]
    y = x * cw_ref[CONV_W - 1:CONV_W, :]
    for w in range(CONV_W - 1):
        y = y + buf[w:w + 1, :] * cw_ref[w:w + 1, :]
    y = _silu(y)
    conv_out_ref[0, 0:CONV_W - 2, :] = buf[1:CONV_W - 1, :]
    conv_out_ref[0, CONV_W - 2:CONV_W - 1, :] = x

    def heads(off):
        return jnp.concatenate([y[:, off + h * HEAD_DIM:off + (h + 1) * HEAD_DIM] for h in range(GDN_HEADS)], axis=0)

    yq, yk, v = heads(0), heads(GDN_W), heads(2 * GDN_W)
    q = yq * lax.rsqrt(jnp.sum(yq * yq, axis=-1, keepdims=True) + EPS) * SCALE
    k = yk * lax.rsqrt(jnp.sum(yk * yk, axis=-1, keepdims=True) + EPS)
    sm = sm_ref[0]
    g_row = -jnp.exp(prow_ref[0:1, :]) * _softplus(sm[:, A_COL:A_COL + GDN_HEADS] + prow_ref[1:2, :])
    beta_row = _sigmoid(sm[:, B_COL:B_COL + GDN_HEADS])
    eg_row = jnp.exp(g_row)
    k_t = _dot_nt_exact(eye_ref[...], k)
    q_t = _dot_nt_exact(eye_ref[...], q)
    z = z_ref[0]
    for h in range(GDN_HEADS):
        s0 = s0_ref[0, 0, h]
        eg = eg_row[:, h:h + 1]
        kc = k_t[:, h:h + 1].astype(BF16).astype(F32)
        qc = q_t[:, h:h + 1].astype(BF16).astype(F32)
        s0b = s0.astype(BF16).astype(F32)
        ks = jnp.sum(kc * beta_row[:, h:h + 1] * eg * s0b, axis=0, keepdims=True)
        v_new = v[h:h + 1] * beta_row[:, h:h + 1] - ks
        vb = v_new.astype(BF16).astype(F32)
        s1 = s0 * eg + kc * vb
        qe = (q_t[:, h:h + 1] * eg).astype(BF16).astype(F32)
        o = jnp.sum(qe * s0b, axis=0, keepdims=True) + jnp.sum(qc * kc, axis=0, keepdims=True) * vb
        s_out_ref[0, h] = s1
        sl = slice(h * HEAD_DIM, (h + 1) * HEAD_DIM)
        o = _rms(o, nw_ref[...]) * _silu(z[:, sl])
        o_ref[0, :, sl] = o.astype(BF16)


def _gdn_sample(proj, small, state_conv, state_gdn, conv_w, a_log, dt_bias, norm_w, l, b):
    prow = jnp.stack([a_log, dt_bias], axis=0)
    rows = proj.shape[0]
    return pl.pallas_call(
        _gdn_sample_kernel,
        out_shape=(jax.ShapeDtypeStruct((b, 1, GDN_W), BF16),
                   jax.ShapeDtypeStruct((b, GDN_HEADS, HEAD_DIM, HEAD_DIM), F32),
                   jax.ShapeDtypeStruct((b, CONV_W - 1, CONV_CH), F32)),
        grid=(b,),
        in_specs=[pl.BlockSpec((1, 1, CONV_CH), lambda bi: (bi, 0, OFF_CONV // CONV_CH)),
                  pl.BlockSpec((1, 1, GDN_W), lambda bi: (bi, 0, OFF_Z // GDN_W)),
                  pl.BlockSpec((1, 1, SMALL_W), lambda bi: (bi, 0, 0)),
                  pl.BlockSpec((1, 1, CONV_W - 1, CONV_CH), lambda bi: (bi, l, 0, 0)),
                  pl.BlockSpec((1, 1, GDN_HEADS, HEAD_DIM, HEAD_DIM), lambda bi: (bi, l, 0, 0, 0)),
                  pl.BlockSpec((CONV_W, CONV_CH), lambda bi: (0, 0)),
                  pl.BlockSpec((2, GDN_HEADS), lambda bi: (0, 0)),
                  pl.BlockSpec((1, HEAD_DIM), lambda bi: (0, 0)),
                  pl.BlockSpec((HEAD_DIM, HEAD_DIM), lambda bi: (0, 0))],
        out_specs=(pl.BlockSpec((1, 1, GDN_W), lambda bi: (bi, 0, 0)),
                   pl.BlockSpec((1, GDN_HEADS, HEAD_DIM, HEAD_DIM), lambda bi: (bi, 0, 0, 0)),
                   pl.BlockSpec((1, CONV_W - 1, CONV_CH), lambda bi: (bi, 0, 0))),
        compiler_params=_cparams(("arbitrary",)),
        name="gdn_sample",
    )(proj.reshape(rows, 1, MAIN_W), proj.reshape(rows, 1, MAIN_W), small.reshape(rows, 1, SMALL_W),
      state_conv, state_gdn, conv_w, prow, norm_w.reshape(1, HEAD_DIM), jnp.eye(HEAD_DIM, dtype=F32))


def _repack_w_in(w_in):
    kvw = NSA_KV * HEAD_DIM
    o_q = 0
    o_kv = [NSA_Q_W + i * kvw for i in range(6)]
    o_gl = NSA_Q_W + 6 * kvw
    o_conv = o_gl + 3 * NSA_HEADS
    o_a = o_conv + CONV_CH
    o_b = o_a + GDN_HEADS
    o_z = o_b + GDN_HEADS

    def cols(a, n):
        return w_in[:, :, a:a + n]

    pieces = [cols(o_conv, CONV_CH), cols(o_z, GDN_W), cols(o_q, NSA_Q_W)]
    for pair in range(3):
        for g in range(NSA_KV):
            pieces.append(cols(o_kv[2 * pair] + g * HEAD_DIM, HEAD_DIM))
            pieces.append(cols(o_kv[2 * pair + 1] + g * HEAD_DIM, HEAD_DIM))
    main = jnp.concatenate(pieces, axis=-1).astype(BF16)
    depth, d, _ = w_in.shape
    ngl = 3 * NSA_HPG
    small = jnp.concatenate([
        cols(o_gl, ngl), cols(o_a, GDN_HEADS), cols(o_b, GDN_HEADS),
        jnp.zeros((depth, d, HEAD_DIM - ngl - 2 * GDN_HEADS), w_in.dtype),
        cols(o_gl + ngl, ngl), jnp.zeros((depth, d, HEAD_DIM - ngl), w_in.dtype)], axis=-1).astype(BF16)
    return main, small


def _mods(mod_l, lo, hi, broadcast):
    d = mod_l.shape[1] // 6
    out = []
    for k in range(6):
        m = mod_l[lo:hi, k * d:(k + 1) * d]
        out.append(m[:, None, :] if broadcast else m[None])
    return out


def kernel(x_prompt, x_sample, cache_cmp_kv, cache_sel_kv, cache_win_kv, state_gdn, state_conv, page_table,
           c_prompt, c_sample, w_ada, b_ada, g_pre_mix, w_in, cmp_pe, cmp_w1, cmp_w2, conv_w, gdn_a_log,
           gdn_dt_bias, gdn_norm, w_out, g_post_mix, g_pre_ffn, w_gate, w_up, w_down, g_post_ffn):
    bp, t, d = x_prompt.shape
    bs = x_sample.shape[0]
    depth = w_in.shape[0]
    n_pool, _, _, page, _, _ = cache_cmp_kv.shape
    past = page_table.shape[1] * page
    npb = past // BLOCK
    bpp = page // BLOCK
    wb = cache_win_kv.shape[3]
    ts = 16
    assert x_sample.shape[1] == 1 and bs <= ts and t % BLOCK == 0 and page % BLOCK == 0

    c_all = jnp.concatenate([c_sample, jnp.zeros((ts - bs, d), F32), c_prompt,
                             jnp.zeros((-bp % 8, d), F32)], axis=0)
    mod = _ada_mod(c_all, w_ada, b_ada)

    w_main, w_small = _repack_w_in(w_in)
    w_out_b, w_gate_b, w_up_b, w_down_b = (w.astype(BF16) for w in (w_out, w_gate, w_up, w_down))
    rope_p = _rope_tables(jnp.arange(t))
    rope_s = _rope_tables(jnp.full((ts,), past))

    tm = min(512, t)
    xp = x_prompt.reshape(bp * t, d)
    xs = jnp.concatenate([x_sample.reshape(bs, d), jnp.zeros((ts - bs, d), F32)], axis=0)
    cache_cmp_blocks = cache_cmp_kv.reshape(n_pool, depth, NSA_KV * bpp, BLK_ROWS, HEAD_DIM)
    cache_sel_blocks = cache_sel_kv.reshape(n_pool, depth, NSA_KV, bpp, BLK_ROWS, HEAD_DIM)
    cache_win_rows = cache_win_kv.reshape(bs, depth, NSA_KV, 2 * wb, HEAD_DIM)
    nbp = -(-(npb + 1) // HEAD_DIM) * HEAD_DIM

    p_states, s_states = [], []
    for l in range(depth):
        gpm, gpo, gpf, gpof = (g[l][None] for g in (g_pre_mix, g_post_mix, g_pre_ffn, g_post_ffn))
        cw = _compress_weights(cmp_pe, cmp_w1, cmp_w2, l)

        sh1, sc1, ga1, sh2, sc2, ga2 = _mods(mod[l], ts, ts + bp, True)
        proj, small = _in_proj(xp, gpm, sc1, sh1, w_main, w_small, l, t, tm)
        q_rot, cmp_st, sel_st, win_st, att = _post_proj(proj, rope_p, bp, t, min(256, t))
        nb = t // BLOCK
        kcb, vcb = _compress(cmp_st.reshape(bp * NSA_KV * 2 * t, HEAD_DIM), cw, math.gcd(bp * NSA_KV * nb, 128))
        kcb, vcb = (a.reshape(bp, NSA_KV, nb, HEAD_DIM) for a in (kcb, vcb))
        o_nsa = _nsa_prompt(q_rot, small, kcb, vcb, att, bp, t)
        o_gdn, s_fin, conv_fin = _gdn_prompt(proj, small, conv_w[l], gdn_a_log[l], gdn_dt_bias[l], gdn_norm[l], bp, t)
        xp = _out_proj(o_nsa, o_gdn, w_out_b, xp, ga1, gpo, l, t, tm)
        xp = _ffn(xp, gpf, sc2, sh2, w_gate_b, w_up_b, w_down_b, ga2, gpof, l, t, tm)
        wl = min(WINDOW, t)
        p_states.append((cmp_st.reshape(bp, NSA_KV, t, 2, HEAD_DIM), sel_st.reshape(bp, NSA_KV, t, 2, HEAD_DIM),
                         win_st[:, :, 2 * (t - wl):].reshape(bp, NSA_KV, wl, 2, HEAD_DIM), s_fin, conv_fin))

        sh1, sc1, ga1, sh2, sc2, ga2 = _mods(mod[l], 0, ts, False)
        proj, small = _in_proj(xs, gpm, sc1, sh1, w_main, w_small, l, ts, ts)
        q_rot, cmp_new, sel_new, win_new, _ = _post_proj(proj, rope_s, 1, ts, ts)
        cmp_new, sel_new, win_new = (a.reshape(NSA_KV, ts, 2, HEAD_DIM)[:, :bs].transpose(1, 0, 2, 3)
                                     for a in (cmp_new, sel_new, win_new))
        kc_past, vc_past = _compress_paged(cache_cmp_blocks, page_table, cw, l)
        new_blk = jnp.pad(cmp_new.reshape(bs * NSA_KV, 2, HEAD_DIM), ((0, 0), (0, BLK_ROWS - 2), (0, 0)))
        kc_new, vc_new = _compress(new_blk.reshape(bs * NSA_KV * BLK_ROWS, HEAD_DIM), cw, bs * NSA_KV)

        def summaries(past_rows, new_rows):
            a = past_rows.reshape(bs, past // page, NSA_KV, bpp, HEAD_DIM).transpose(0, 2, 1, 3, 4)
            a = a.reshape(bs, NSA_KV, npb, HEAD_DIM)
            a = jnp.concatenate([a, new_rows.reshape(bs, NSA_KV, 1, HEAD_DIM)], axis=2)
            return jnp.pad(a, ((0, 0), (0, 0), (0, nbp - npb - 1), (0, 0)))

        kcb, vcb = summaries(kc_past, kc_new), summaries(vc_past, vc_new)
        o_c, idx = _nsa_sample_cmp(q_rot[:bs], kcb, vcb, past, npb + 1)
        o_nsa, win_out = _nsa_sample_attn(idx[..., 0], page_table, q_rot[:bs], small[:bs], o_c, cache_sel_blocks,
                                          sel_new, cache_win_rows, win_new, l, npb)
        o_gdn, s_fin, conv_fin = _gdn_sample(proj, small, state_conv, state_gdn, conv_w[l], gdn_a_log[l],
                                             gdn_dt_bias[l], gdn_norm[l], l, bs)
        pad_rows = ((0, ts - bs), (0, 0))
        o_nsa = jnp.pad(o_nsa.reshape(bs, NSA_Q_W), pad_rows)
        o_gdn = jnp.pad(o_gdn.reshape(bs, GDN_W), pad_rows)
        xs = _out_proj(o_nsa, o_gdn, w_out_b, xs, ga1, gpo, l, ts, ts)
        xs = _ffn(xs, gpf, sc2, sh2, w_gate_b, w_up_b, w_down_b, ga2, gpof, l, ts, ts)
        s_states.append((cmp_new.reshape(bs, NSA_KV, 1, 2, HEAD_DIM), sel_new.reshape(bs, NSA_KV, 1, 2, HEAD_DIM),
                         win_out.reshape(bs, NSA_KV, wb, 2, HEAD_DIM), s_fin, conv_fin))

    p_st = [jnp.stack(s, axis=1) for s in zip(*p_states)]
    s_st = [jnp.stack(s, axis=1) for s in zip(*s_states)]
    return (xp.reshape(bp, t, d), xs[:bs].reshape(bs, 1, d), *p_st, *s_st)
```

```python
import functools
import math

import jax
import jax.numpy as jnp
from jax import lax
from jax.experimental import pallas as pl
from jax.experimental.pallas import tpu as pltpu

F32 = jnp.float32
BF16 = jnp.bfloat16

HEAD_DIM = 128
NSA_HEADS = 8
NSA_KV = 2
NSA_HPG = NSA_HEADS // NSA_KV
GDN_HEADS = 8
ROT_DIM = HEAD_DIM // 4
ROPE_THETA = 500000.0
BLOCK = 64
TOPN = 16
WINDOW = 512
CMP_HID = 256
CONV_W = 4
GDN_CHUNK = 64
EPS = 1e-6
KV_ROW = 2 * HEAD_DIM
BLK_ROWS = 2 * BLOCK
BLK_PITCH = BLK_ROWS + 8
NSA_Q_W = NSA_HEADS * HEAD_DIM
GDN_W = GDN_HEADS * HEAD_DIM
CONV_CH = 3 * GDN_W
SMALL_W = 2 * HEAD_DIM
A_COL = 12
B_COL = 20
OFF_CONV = 0
OFF_Z = CONV_CH
OFF_Q = OFF_Z + GDN_W
OFF_CMP = OFF_Q + NSA_Q_W
OFF_SEL = OFF_CMP + NSA_KV * KV_ROW
OFF_WIN = OFF_SEL + NSA_KV * KV_ROW
MAIN_W = OFF_WIN + NSA_KV * KV_ROW
VMEM_LIMIT = 48 * 1024 * 1024
NEG_INF = float("-inf")
SCALE = HEAD_DIM ** -0.5


def _cparams(sem):
    return pltpu.CompilerParams(dimension_semantics=sem, vmem_limit_bytes=VMEM_LIMIT)


def _dot(a, b):
    return jnp.dot(a.astype(BF16), b.astype(BF16), preferred_element_type=F32)


def _dot_nt(a, b):
    return lax.dot_general(a.astype(BF16), b.astype(BF16), (((1,), (1,)), ((), ())),
                           preferred_element_type=F32)


def _dot_exact(a, b):
    return jnp.dot(a, b, preferred_element_type=F32, precision=lax.Precision.HIGHEST)


def _dot_nt_exact(a, b):
    return lax.dot_general(a, b, (((1,), (1,)), ((), ())), preferred_element_type=F32,
                           precision=lax.Precision.HIGHEST)


def _sigmoid(x):
    return 1.0 / (1.0 + jnp.exp(-x))


def _silu(x):
    return x * _sigmoid(x)


def _rms(x, g):
    return x * lax.rsqrt(jnp.mean(x * x, axis=-1, keepdims=True) + EPS) * g


def _masked_softmax(s, mask):
    s = jnp.where(mask, s, NEG_INF)
    m = jnp.max(s, axis=-1, keepdims=True)
    m = jnp.where(m == NEG_INF, 0.0, m)
    e = jnp.where(mask, jnp.exp(s - m), 0.0)
    return e / jnp.maximum(jnp.sum(e, axis=-1, keepdims=True), 1e-30)


def _ada_kernel(c_ref, w_ref, b_ref, o_ref):
    o_ref[0] = _dot(_silu(c_ref[...]), w_ref[0]) + b_ref[0]


def _ada_mod(c_all, w_ada, b_ada, tn=1024):
    depth, d, n = w_ada.shape
    r = c_all.shape[0]
    return pl.pallas_call(
        _ada_kernel,
        out_shape=jax.ShapeDtypeStruct((depth, r, n), F32),
        grid=(depth, n // tn),
        in_specs=[pl.BlockSpec((r, d), lambda l, j: (0, 0)),
                  pl.BlockSpec((1, d, tn), lambda l, j: (l, 0, j)),
                  pl.BlockSpec((1, 1, tn), lambda l, j: (l, 0, j))],
        out_specs=pl.BlockSpec((1, r, tn), lambda l, j: (l, 0, j)),
        compiler_params=_cparams(("arbitrary", "arbitrary")),
        name="ada_mod",
    )(c_all, w_ada, b_ada.reshape(depth, 1, n))


def _inproj_kernel(x_ref, g_ref, sc_ref, sh_ref, w_ref, ws_ref, o_ref, os_ref, h_ref):
    @pl.when(pl.program_id(1) == 0)
    def _():
        h = _rms(x_ref[...], g_ref[...]) * (1.0 + sc_ref[0]) + sh_ref[0]
        hb = h.astype(BF16)
        h_ref[...] = hb
        os_ref[...] = jnp.dot(hb, ws_ref[0], preferred_element_type=F32)

    o_ref[...] = jnp.dot(h_ref[...], w_ref[0], preferred_element_type=F32)


def _mod_spec(mod, rows_per_mod, tm):
    d = mod.shape[-1]
    if mod.shape[1] == 1:
        return pl.BlockSpec((1, 1, d), lambda i, *_: ((i * tm) // rows_per_mod, 0, 0))
    return pl.BlockSpec((1, tm, d), lambda i, *_: (0, i, 0))


def _in_proj(x, g, sc, sh, w_main, w_small, l, rows_per_mod, tm, tn=MAIN_W // 4):
    m, d = x.shape
    return pl.pallas_call(
        _inproj_kernel,
        out_shape=(jax.ShapeDtypeStruct((m, MAIN_W), F32), jax.ShapeDtypeStruct((m, SMALL_W), F32)),
        grid=(m // tm, MAIN_W // tn),
        in_specs=[pl.BlockSpec((tm, d), lambda i, j: (i, 0)),
                  pl.BlockSpec((1, d), lambda i, j: (0, 0)),
                  _mod_spec(sc, rows_per_mod, tm), _mod_spec(sh, rows_per_mod, tm),
                  pl.BlockSpec((1, d, tn), lambda i, j: (l, 0, j)),
                  pl.BlockSpec((1, d, SMALL_W), lambda i, j: (l, 0, 0))],
        out_specs=(pl.BlockSpec((tm, tn), lambda i, j: (i, j)),
                   pl.BlockSpec((tm, SMALL_W), lambda i, j: (i, 0))),
        scratch_shapes=[pltpu.VMEM((tm, d), BF16)],
        compiler_params=_cparams(("arbitrary", "arbitrary")),
        name="in_proj",
    )(x, g, sc, sh, w_main, w_small)


def _rope(x, c, s1, s2):
    half = ROT_DIM // 2
    return x * c + pltpu.roll(x, HEAD_DIM - half, 1) * s1 + pltpu.roll(x, half, 1) * s2


ATT_W = 5 * HEAD_DIM


def _postproj_kernel(q_ref, cmp_ref, sel_ref, win_ref, c_ref, s1_ref, s2_ref,
                     qo_ref, cmpo_ref, selo_ref, wino_ref, att_ref):
    c, s1, s2 = c_ref[...], s1_ref[...], s2_ref[...]
    for h in range(NSA_HEADS):
        sl = slice(h * HEAD_DIM, (h + 1) * HEAD_DIM)
        qo_ref[:, sl] = _rope(q_ref[:, sl], c, s1, s2).astype(BF16)
    tt = c.shape[0]
    blk = (pl.program_id(1) * tt + lax.broadcasted_iota(jnp.int32, (tt, HEAD_DIM), 0)) // BLOCK
    onehot = jnp.where(lax.broadcasted_iota(jnp.int32, (tt, HEAD_DIM), 1) == blk, 1.0, 0.0).astype(BF16)
    for src, dst, col in ((cmp_ref, cmpo_ref, None), (sel_ref, selo_ref, 0), (win_ref, wino_ref, 3 * HEAD_DIM)):
        for g in range(NSA_KV):
            base = g * KV_ROW
            k = _rope(src[:, base:base + HEAD_DIM], c, s1, s2)
            v = src[:, base + HEAD_DIM:base + KV_ROW]
            dst[0, g, pl.ds(0, tt, stride=2), :] = k
            dst[0, g, pl.ds(1, tt, stride=2), :] = v
            if col == 0:
                att_ref[0, g, :, 0:HEAD_DIM] = k.astype(BF16)
                att_ref[0, g, :, HEAD_DIM:2 * HEAD_DIM] = onehot
                att_ref[0, g, :, 2 * HEAD_DIM:3 * HEAD_DIM] = v.astype(BF16)
            elif col is not None:
                att_ref[0, g, :, col:col + HEAD_DIM] = k.astype(BF16)
                att_ref[0, g, :, col + HEAD_DIM:col + 2 * HEAD_DIM] = v.astype(BF16)


def _post_proj(proj, rope_tabs, b, t, tt):
    nt = t // tt
    state = jax.ShapeDtypeStruct((b, NSA_KV, 2 * t, HEAD_DIM), F32)
    att = jax.ShapeDtypeStruct((b, NSA_KV, t, ATT_W), BF16)
    kvw = NSA_KV * KV_ROW
    tab_spec = pl.BlockSpec((tt, HEAD_DIM), lambda bi, ti: (ti, 0))
    st_spec = pl.BlockSpec((1, NSA_KV, 2 * tt, HEAD_DIM), lambda bi, ti: (bi, 0, ti, 0))
    att_spec = pl.BlockSpec((1, NSA_KV, tt, ATT_W), lambda bi, ti: (bi, 0, ti, 0))
    return pl.pallas_call(
        _postproj_kernel,
        out_shape=(jax.ShapeDtypeStruct((b * t, NSA_Q_W), BF16), state, state, state, att),
        grid=(b, nt),
        in_specs=[pl.BlockSpec((tt, NSA_Q_W), lambda bi, ti: (bi * nt + ti, OFF_Q // NSA_Q_W)),
                  pl.BlockSpec((tt, kvw), lambda bi, ti: (bi * nt + ti, OFF_CMP // kvw)),
                  pl.BlockSpec((tt, kvw), lambda bi, ti: (bi * nt + ti, OFF_SEL // kvw)),
                  pl.BlockSpec((tt, kvw), lambda bi, ti: (bi * nt + ti, OFF_WIN // kvw)),
                  tab_spec, tab_spec, tab_spec],
        out_specs=(pl.BlockSpec((tt, NSA_Q_W), lambda bi, ti: (bi * nt + ti, 0)),
                   st_spec, st_spec, st_spec, att_spec),
        compiler_params=_cparams(("arbitrary", "arbitrary")),
        name="post_proj",
    )(proj, proj, proj, proj, *rope_tabs)


def _rope_tables(pos):
    half = ROT_DIM // 2
    inv = ROPE_THETA ** (-jnp.arange(half, dtype=F32) * 2.0 / ROT_DIM)
    ang = pos.astype(F32)[:, None] * inv
    cos, sin = jnp.cos(ang), jnp.sin(ang)
    n = pos.shape[0]
    rest = HEAD_DIM - ROT_DIM
    c = jnp.concatenate([cos, cos, jnp.ones((n, rest), F32)], axis=1)
    s1 = jnp.concatenate([-sin, jnp.zeros((n, HEAD_DIM - half), F32)], axis=1)
    s2 = jnp.concatenate([jnp.zeros((n, half), F32), sin, jnp.zeros((n, rest), F32)], axis=1)
    return c, s1, s2


def _gelu_tanh(x):
    return 0.5 * x * (1.0 + jnp.tanh(math.sqrt(2.0 / math.pi) * (x + 0.044715 * (x * x * x))))


def _compress_blocks(x_ref, base, pitch, nblk, pe_ref, w1k_ref, w1v_ref, w2k_ref, w2v_ref, ko_ref, vo_ref):
    hk = jnp.zeros((nblk, CMP_HID), F32)
    hv = jnp.zeros((nblk, CMP_HID), F32)
    for r in range(BLOCK):
        xk = x_ref[pl.ds(base + 2 * r, nblk, stride=pitch), :] + pe_ref[0, r:r + 1, :]
        xv = x_ref[pl.ds(base + 2 * r + 1, nblk, stride=pitch), :] + pe_ref[1, r:r + 1, :]
        hk = hk + _dot(xk, w1k_ref[r])
        hv = hv + _dot(xv, w1v_ref[r])
    ko_ref[...] = _dot(_gelu_tanh(hk), w2k_ref[...])
    vo_ref[...] = _dot(_gelu_tanh(hv), w2v_ref[...])


def _compress_kernel(x_ref, *rest):
    _compress_blocks(x_ref, 0, BLK_ROWS, x_ref.shape[0] // BLK_ROWS, *rest)


def _compress_weights(cmp_pe, cmp_w1, cmp_w2, l):
    w1 = cmp_w1[l].astype(BF16).reshape(2, BLOCK, HEAD_DIM, CMP_HID)
    w2 = cmp_w2[l].astype(BF16)
    return cmp_pe[l], w1[0], w1[1], w2[0], w2[1]


def _cw_specs(nargs):
    z = (0,) * 3
    return [pl.BlockSpec((2, BLOCK, HEAD_DIM), lambda *a: z),
            pl.BlockSpec((BLOCK, HEAD_DIM, CMP_HID), lambda *a: z),
            pl.BlockSpec((BLOCK, HEAD_DIM, CMP_HID), lambda *a: z),
            pl.BlockSpec((CMP_HID, HEAD_DIM), lambda *a: (0, 0)),
            pl.BlockSpec((CMP_HID, HEAD_DIM), lambda *a: (0, 0))]


def _compress(x_rows, cw, tr):
    nblocks = x_rows.shape[0] // BLK_ROWS
    out = jax.ShapeDtypeStruct((nblocks, HEAD_DIM), F32)
    return pl.pallas_call(
        _compress_kernel,
        out_shape=(out, out),
        grid=(nblocks // tr,),
        in_specs=[pl.BlockSpec((tr * BLK_ROWS, HEAD_DIM), lambda i: (i, 0))] + _cw_specs(1),
        out_specs=(pl.BlockSpec((tr, HEAD_DIM), lambda i: (i, 0)),) * 2,
        compiler_params=_cparams(("arbitrary",)),
        name="compress",
    )(x_rows, *cw)


MAX_PAGES_PER_STEP = 32


def _compress_paged_kernel(pt_ref, cache_ref, pe_ref, w1k_ref, w1v_ref, w2k_ref, w2v_ref, ko_ref, vo_ref,
                           xbuf, sem, *, l):
    bpp = cache_ref.shape[2]
    nblk = ko_ref.shape[0]
    pages_per_step = nblk // bpp
    ns = pl.num_programs(1)
    step = pl.program_id(0) * ns + pl.program_id(1)
    total = pl.num_programs(0) * ns

    def copies(st, slot):
        bi, s = st // ns, st % ns
        out = []
        for k in range(pages_per_step):
            pg = pt_ref[bi, s * pages_per_step + k]
            for j in range(bpp):
                row0 = pl.multiple_of(slot * (nblk * BLK_PITCH) + (k * bpp + j) * BLK_PITCH, 8)
                out.append(pltpu.make_async_copy(cache_ref.at[pg, l, j], xbuf.at[pl.ds(row0, BLK_ROWS), :],
                                                 sem.at[slot]))
        return out

    slot = step % 2

    @pl.when(step == 0)
    def _():
        for c in copies(step, slot):
            c.start()

    @pl.when(step + 1 < total)
    def _():
        for c in copies(step + 1, 1 - slot):
            c.start()

    for c in copies(step, slot):
        c.wait()
    _compress_blocks(xbuf, slot * (nblk * BLK_PITCH), BLK_PITCH, nblk,
                     pe_ref, w1k_ref, w1v_ref, w2k_ref, w2v_ref, ko_ref, vo_ref)


def _compress_paged(cache, page_table, cw, l):
    b, n_pages = page_table.shape
    bpp = cache.shape[2]
    pages_per_step = math.gcd(n_pages, MAX_PAGES_PER_STEP)
    steps = n_pages // pages_per_step
    tr = pages_per_step * bpp
    out = jax.ShapeDtypeStruct((b * steps * tr, HEAD_DIM), F32)
    grid_spec = pltpu.PrefetchScalarGridSpec(
        num_scalar_prefetch=1,
        grid=(b, steps),
        in_specs=[pl.BlockSpec(memory_space=pl.ANY)] + _cw_specs(3),
        out_specs=(pl.BlockSpec((tr, HEAD_DIM), lambda bi, s, pt: (bi * steps + s, 0)),) * 2,
        scratch_shapes=[pltpu.VMEM((2 * tr * BLK_PITCH, HEAD_DIM), F32), pltpu.SemaphoreType.DMA((2,))],
    )
    return pl.pallas_call(
        functools.partial(_compress_paged_kernel, l=l),
        out_shape=(out, out),
        grid_spec=grid_spec,
        compiler_params=_cparams(("arbitrary", "arbitrary")),
        name="compress_paged",
    )(page_table, cache, *cw)


def _stable_topn_masks(scores, n_sel):
    rows, nb = scores[0].shape
    tn = (((0,), (0,)), ((), ()))
    eye_r = jnp.where(lax.broadcasted_iota(jnp.int32, (rows, rows), 0)
                      == lax.broadcasted_iota(jnp.int32, (rows, rows), 1), 1.0, 0.0)
    sts = [lax.dot_general(sc, eye_r, tn, preferred_element_type=F32, precision=lax.Precision.HIGHEST)
           for sc in scores]
    jrow = lax.broadcasted_iota(jnp.int32, (nb, rows), 0)
    ranks = [jnp.zeros((nb, rows), F32) for _ in scores]
    for k in range(nb):
        for p, st in enumerate(sts):
            ck = st[k:k + 1, :]
            beats = (ck > st) | ((ck == st) & (jrow > k))
            ranks[p] = ranks[p] + jnp.where(beats, 1.0, 0.0)
    eye_n = jnp.where(lax.broadcasted_iota(jnp.int32, (nb, nb), 0)
                      == lax.broadcasted_iota(jnp.int32, (nb, nb), 1), 1.0, 0.0).astype(BF16)
    return [lax.dot_general(jnp.where(r < n_sel, 1.0, 0.0).astype(BF16), eye_n, tn, preferred_element_type=F32)
            for r in ranks]


MASK_BIAS = -(2.0 ** 100)
SCORE_C = SCALE * 1.4426950408889634


def _softmax_pv(problems):
    ms = []
    for parts in problems:
        m = jnp.max(parts[0][0], axis=-1, keepdims=True)
        for s, _ in parts[1:]:
            m = jnp.maximum(m, jnp.max(s, axis=-1, keepdims=True))
        ms.append(m)
    dens = [0.0 for _ in problems]
    accs = [0.0 for _ in problems]
    for j in range(len(problems[0])):
        for p, parts in enumerate(problems):
            s, v = parts[j]
            e = jnp.exp2(s - ms[p])
            dens[p] = dens[p] + jnp.sum(e, axis=-1, keepdims=True)
            accs[p] = accs[p] + jnp.dot(e.astype(BF16), v, preferred_element_type=F32)
    return [a / d for a, d in zip(accs, dens)]


def _nsa_prompt_kernel(q_ref, sm_ref, kcb_ref, vcb_ref, att_ref, o_ref, os_ref, *, n_sel, wl, n_cls):
    i = pl.program_id(1)
    groups = range(NSA_KV)
    t_all = att_ref.shape[2]
    nb = kcb_ref.shape[2]
    rows = NSA_HPG * BLOCK
    nt = (((1,), (1,)), ((), ()))
    q = q_ref[...]
    qs = [jnp.concatenate([q[:, (g * NSA_HPG + h) * HEAD_DIM:(g * NSA_HPG + h + 1) * HEAD_DIM]
                           for h in range(NSA_HPG)], axis=0) for g in groups]
    q0 = i * BLOCK
    tq = lax.broadcasted_iota(jnp.int32, (rows, 1), 0) & (BLOCK - 1)
    qpos = q0 + tq

    start = pl.multiple_of(jnp.clip(q0 - WINDOW, 0, t_all - wl), BLOCK)
    wpos = start + lax.broadcasted_iota(jnp.int32, (rows, wl), 1)
    wbias = jnp.where((wpos <= qpos) & (wpos >= qpos - WINDOW), 0.0, MASK_BIAS)
    s_w = [lax.dot_general(qs[g], att_ref[0, g, pl.ds(start, wl), 3 * HEAD_DIM:4 * HEAD_DIM], nt,
                           preferred_element_type=F32) * SCORE_C + wbias for g in groups]
    o_w = _softmax_pv([[(s_w[g], att_ref[0, g, pl.ds(start, wl), 4 * HEAD_DIM:5 * HEAD_DIM])] for g in groups])

    jc = lax.broadcasted_iota(jnp.int32, (rows, nb), 1)
    cmask = (jc * BLOCK + (BLOCK - 1)) <= qpos
    s_c = [jnp.where(cmask, _dot_nt(qs[g], kcb_ref[0, g]) * SCALE, NEG_INF) for g in groups]
    m_c = [jnp.max(s, axis=-1, keepdims=True) for s in s_c]
    m_c = [jnp.where(m == NEG_INF, 0.0, m) for m in m_c]
    e_c = [jnp.where(cmask, jnp.exp(s - m), 0.0) for s, m in zip(s_c, m_c)]
    p_c = [e / jnp.maximum(jnp.sum(e, axis=-1, keepdims=True), 1e-30) for e in e_c]
    o_c = [_dot(p_c[g], vcb_ref[0, g]) for g in groups]

    jb = lax.broadcasted_iota(jnp.int32, (BLOCK, nb), 1)
    forced = (jb == 0) | (jb == i) | (jb == i - 1)
    scores = []
    for g in groups:
        imp = p_c[g][0:BLOCK]
        for h in range(1, NSA_HPG):
            imp = imp + p_c[g][h * BLOCK:(h + 1) * BLOCK]
        scores.append(jnp.where(forced, 2.0 * NSA_HPG, jnp.where(jb > i, -1.0, imp)))
    sel = _stable_topn_masks(scores, n_sel)
    r0 = pl.multiple_of(q0, BLOCK)
    tl = lax.broadcasted_iota(jnp.int32, (rows, BLOCK), 1)
    dbias = jnp.where(tl <= tq, 0.0, MASK_BIAS)
    q_aug, s_d, v_d = [], [], []
    for g in groups:
        blk_bias = jnp.where((jb < i) & (sel[g] > 0.5), 0.0, MASK_BIAS)
        blk_bias = jnp.concatenate([blk_bias, jnp.zeros((BLOCK, HEAD_DIM - nb), F32)], axis=1).astype(BF16)
        q_aug.append(jnp.concatenate([qs[g], jnp.concatenate([blk_bias] * NSA_HPG, axis=0)], axis=1))
        s_d.append(lax.dot_general(qs[g], att_ref[0, g, pl.ds(r0, BLOCK), 0:HEAD_DIM], nt,
                                   preferred_element_type=F32) * SCORE_C + dbias)
        v_d.append(att_ref[0, g, pl.ds(r0, BLOCK), 2 * HEAD_DIM:3 * HEAD_DIM])
    per_cls = nb // n_cls
    for c in range(n_cls):
        nk = (c + 1) * per_cls * BLOCK

        @pl.when((i >= c * per_cls) & (i < (c + 1) * per_cls))
        def _():
            s_b = [lax.dot_general(q_aug[g], att_ref[0, g, 0:nk, 0:2 * HEAD_DIM], nt,
                                   preferred_element_type=F32) * SCORE_C for g in groups]
            o = _softmax_pv([[(s_b[g], att_ref[0, g, 0:nk, 2 * HEAD_DIM:3 * HEAD_DIM]), (s_d[g], v_d[g])]
                             for g in groups])
            for g in groups:
                os_ref[g] = o[g]

    for g in groups:
        o_s = os_ref[g]
        gt = _sigmoid(sm_ref[:, g * HEAD_DIM:(g + 1) * HEAD_DIM])
        for h in range(NSA_HPG):
            r = slice(h * BLOCK, (h + 1) * BLOCK)
            o = (gt[:, 3 * h:3 * h + 1] * o_c[g][r] + gt[:, 3 * h + 1:3 * h + 2] * o_s[r]
                 + gt[:, 3 * h + 2:3 * h + 3] * o_w[g][r])
            c0 = (g * NSA_HPG + h) * HEAD_DIM
            o_ref[:, c0:c0 + HEAD_DIM] = o.astype(BF16)


def _nsa_prompt(q_rot, small, kcb, vcb, att, b, t):
    nq = t // BLOCK
    nb = kcb.shape[2]
    assert nb <= HEAD_DIM
    wl = min(WINDOW + BLOCK, t)
    cb_spec = pl.BlockSpec((1, NSA_KV, nb, HEAD_DIM), lambda bi, i: (bi, 0, 0, 0))
    return pl.pallas_call(
        functools.partial(_nsa_prompt_kernel, n_sel=min(TOPN, nb), wl=wl, n_cls=math.gcd(nb, 8)),
        out_shape=jax.ShapeDtypeStruct((b * t, NSA_Q_W), BF16),
        grid=(b, nq),
        in_specs=[pl.BlockSpec((BLOCK, NSA_Q_W), lambda bi, i: (bi * nq + i, 0)),
                  pl.BlockSpec((BLOCK, SMALL_W), lambda bi, i: (bi * nq + i, 0)),
                  cb_spec, cb_spec,
                  pl.BlockSpec((1, NSA_KV, t, ATT_W), lambda bi, i: (bi, 0, 0, 0))],
        out_specs=pl.BlockSpec((BLOCK, NSA_Q_W), lambda bi, i: (bi * nq + i, 0)),
        scratch_shapes=[pltpu.VMEM((NSA_KV, NSA_HPG * BLOCK, HEAD_DIM), F32)],
        compiler_params=_cparams(("arbitrary", "arbitrary")),
        name="nsa_prompt",
    )(q_rot, small, kcb, vcb, att)


GDN_SOLVE_BLOCKS = 4


def _softplus(x):
    return jnp.maximum(x, 0.0) + jnp.log(1.0 + jnp.exp(-jnp.abs(x)))


def _unit_lower_solve(l_strict, rhs):
    n = l_strict[0].shape[0]
    sub = 8
    nt = n // sub
    tiles = [[r[t * sub:(t + 1) * sub, :] for t in range(nt)] for r in rhs]
    for i in range(n - 1):
        t0, s = divmod(i, sub)
        for p, lm in enumerate(l_strict):
            xi = tiles[p][t0][s:s + 1, :]
            col = lm[:, i:i + 1]
            for t in range(t0, nt):
                tiles[p][t] = tiles[p][t] - col[t * sub:(t + 1) * sub, :] * xi
    return [jnp.concatenate(tp, axis=0) for tp in tiles]


def _gdn_prompt_kernel(x_ref, z_ref, sm_ref, abr_ref, cw_ref, prow_ref, pcol_ref, nw_ref,
                       o_ref, s_out_ref, conv_out_ref, s_ref, xe_ref):
    n = pl.program_id(1)
    c = GDN_CHUNK
    pad = 8

    @pl.when(n == 0)
    def _():
        s_ref[...] = jnp.zeros_like(s_ref)
        xe_ref[0:pad, :] = jnp.zeros((pad, CONV_CH), F32)

    xe_ref[pad:pad + c, :] = x_ref[...]
    y = xe_ref[pl.ds(pad, c), :] * cw_ref[CONV_W - 1:CONV_W, :]
    for w in range(CONV_W - 1):
        y = y + xe_ref[pl.ds(pad - (CONV_W - 1) + w, c), :] * cw_ref[w:w + 1, :]
    y = _silu(y)

    ri = lax.broadcasted_iota(jnp.int32, (c, c), 0)
    ci = lax.broadcasted_iota(jnp.int32, (c, c), 1)
    tri = ri >= ci
    strict = ri > ci
    tri_f = jnp.where(tri, 1.0, 0.0)

    a_col = sm_ref[:, A_COL:A_COL + GDN_HEADS]
    b_col = sm_ref[:, B_COL:B_COL + GDN_HEADS]
    g_col = -jnp.exp(prow_ref[0:1, :]) * _softplus(a_col + prow_ref[1:2, :])
    g_row = -jnp.exp(pcol_ref[:, 0:1]) * _softplus(abr_ref[0, 0, 0:GDN_HEADS, :] + pcol_ref[:, 1:2])
    gc_col = _dot_exact(tri_f, g_col)
    gc_row = _dot_nt_exact(g_row, tri_f)
    beta_col = _sigmoid(b_col)

    pre = []
    for h in range(GDN_HEADS):
        sl = slice(h * HEAD_DIM, (h + 1) * HEAD_DIM)
        yq, yk, v = y[:, sl], y[:, GDN_W + h * HEAD_DIM:GDN_W + (h + 1) * HEAD_DIM], \
            y[:, 2 * GDN_W + h * HEAD_DIM:2 * GDN_W + (h + 1) * HEAD_DIM]
        q = yq * lax.rsqrt(jnp.sum(yq * yq, axis=-1, keepdims=True) + EPS) * SCALE
        k = yk * lax.rsqrt(jnp.sum(yk * yk, axis=-1, keepdims=True) + EPS)
        gcc = gc_col[:, h:h + 1]
        beta = beta_col[:, h:h + 1]
        decay = jnp.exp(jnp.where(tri, gcc - gc_row[h:h + 1, :], NEG_INF))
        kb = k * beta
        lmat = jnp.where(strict, _dot_nt(kb, k) * decay, 0.0)
        rhs = jnp.concatenate([v * beta, kb * jnp.exp(gcc)], axis=1)
        pre.append((q, k, gcc, _dot_nt(q, k) * decay, lmat, rhs))

    nblk = GDN_SOLVE_BLOCKS
    rb = c // nblk
    l_list, r_list = [], []
    for p in pre:
        lmat, rhs = p[4], p[5]
        for b in range(nblk):
            r0 = b * rb
            l_list.append(lmat[r0:r0 + rb, r0:r0 + rb])
            if b == 0:
                r_list.append(rhs[0:rb, :])
            else:
                r_list.append(jnp.concatenate([rhs[r0:r0 + rb, :], lmat[r0:r0 + rb, 0:r0],
                                               jnp.zeros((rb, HEAD_DIM - r0), F32)], axis=1))
    part = _unit_lower_solve(l_list, r_list)
    xs = [[part[h * nblk][:, 0:2 * HEAD_DIM]] for h in range(GDN_HEADS)]
    for b in range(1, nblk):
        for h in range(GDN_HEADS):
            y = part[h * nblk + b]
            xs[h].append(y[:, 0:2 * HEAD_DIM]
                         - _dot_exact(y[:, 2 * HEAD_DIM:2 * HEAD_DIM + b * rb], jnp.concatenate(xs[h], axis=0)))
    sols = [jnp.concatenate(x, axis=0) for x in xs]

    s_out, o_out = [], []
    for h in range(GDN_HEADS):
        q, k, gcc, qk, _, _ = pre[h]
        u, w = sols[h][:, :HEAD_DIM], sols[h][:, HEAD_DIM:]
        s = s_ref[h]
        v_new = u - _dot(w, s)
        o = _dot(q * jnp.exp(gcc), s) + _dot(qk, v_new)
        g_last = gcc[c - 1:c, :]
        k_dec = k * jnp.exp(g_last - gcc)
        s_out.append(s * jnp.exp(g_last) + lax.dot_general(
            k_dec.astype(BF16), v_new.astype(BF16), (((0,), (0,)), ((), ())), preferred_element_type=F32))
        o_out.append((_rms(o, nw_ref[...]) * _silu(z_ref[:, h * HEAD_DIM:(h + 1) * HEAD_DIM])).astype(BF16))
    for h in range(GDN_HEADS):
        s_ref[h] = s_out[h]
        o_ref[:, h * HEAD_DIM:(h + 1) * HEAD_DIM] = o_out[h]
    xe_ref[0:pad, :] = xe_ref[c:c + pad, :]

    @pl.when(n == pl.num_programs(1) - 1)
    def _():
        s_out_ref[0] = s_ref[...]
        conv_out_ref[0] = xe_ref[pad - (CONV_W - 1):pad, :]


def _gdn_prompt(proj, small, conv_w, a_log, dt_bias, norm_w, b, t):
    c = GDN_CHUNK
    nc = t // c
    ab_row = small.reshape(b, nc, c, SMALL_W)[..., A_COL:A_COL + 2 * GDN_HEADS].transpose(0, 1, 3, 2)
    prow = jnp.stack([a_log, dt_bias], axis=0)
    pcol = prow.T
    return pl.pallas_call(
        _gdn_prompt_kernel,
        out_shape=(jax.ShapeDtypeStruct((b * t, GDN_W), BF16),
                   jax.ShapeDtypeStruct((b, GDN_HEADS, HEAD_DIM, HEAD_DIM), F32),
                   jax.ShapeDtypeStruct((b, CONV_W - 1, CONV_CH), F32)),
        grid=(b, nc),
        in_specs=[pl.BlockSpec((c, CONV_CH), lambda bi, n: (bi * nc + n, OFF_CONV // CONV_CH)),
                  pl.BlockSpec((c, GDN_W), lambda bi, n: (bi * nc + n, OFF_Z // GDN_W)),
                  pl.BlockSpec((c, HEAD_DIM), lambda bi, n: (bi * nc + n, 0)),
                  pl.BlockSpec((1, 1, 2 * GDN_HEADS, c), lambda bi, n: (bi, n, 0, 0)),
                  pl.BlockSpec((CONV_W, CONV_CH), lambda bi, n: (0, 0)),
                  pl.BlockSpec((2, GDN_HEADS), lambda bi, n: (0, 0)),
                  pl.BlockSpec((GDN_HEADS, 2), lambda bi, n: (0, 0)),
                  pl.BlockSpec((1, HEAD_DIM), lambda bi, n: (0, 0))],
        out_specs=(pl.BlockSpec((c, GDN_W), lambda bi, n: (bi * nc + n, 0)),
                   pl.BlockSpec((1, GDN_HEADS, HEAD_DIM, HEAD_DIM), lambda bi, n: (bi, 0, 0, 0)),
                   pl.BlockSpec((1, CONV_W - 1, CONV_CH), lambda bi, n: (bi, 0, 0))),
        scratch_shapes=[pltpu.VMEM((GDN_HEADS, HEAD_DIM, HEAD_DIM), F32),
                        pltpu.VMEM((c + 16, CONV_CH), F32)],
        compiler_params=_cparams(("arbitrary", "arbitrary")),
        name="gdn_prompt",
    )(proj, proj, small, ab_row, conv_w, prow, pcol, norm_w.reshape(1, HEAD_DIM))


def _outproj_kernel(on_ref, og_ref, w_ref, x_ref, ga_ref, g_ref, o_ref):
    o = jnp.concatenate([on_ref[...], og_ref[...]], axis=1)
    mix = jnp.dot(o, w_ref[0], preferred_element_type=F32)
    o_ref[...] = x_ref[...] + ga_ref[0] * _rms(mix, g_ref[...])


def _out_proj(o_nsa, o_gdn, w_out, x, ga, g, l, rows_per_mod, tm):
    m, d = x.shape
    kw = o_nsa.shape[1]
    return pl.pallas_call(
        _outproj_kernel,
        out_shape=jax.ShapeDtypeStruct((m, d), F32),
        grid=(m // tm,),
        in_specs=[pl.BlockSpec((tm, kw), lambda i: (i, 0)),
                  pl.BlockSpec((tm, kw), lambda i: (i, 0)),
                  pl.BlockSpec((1, 2 * kw, d), lambda i: (l, 0, 0)),
                  pl.BlockSpec((tm, d), lambda i: (i, 0)),
                  _mod_spec(ga, rows_per_mod, tm),
                  pl.BlockSpec((1, d), lambda i: (0, 0))],
        out_specs=pl.BlockSpec((tm, d), lambda i: (i, 0)),
        compiler_params=_cparams(("arbitrary",)),
        name="out_proj",
    )(o_nsa, o_gdn, w_out, x, ga, g)


def _ffn_kernel(x_ref, g1_ref, sc_ref, sh_ref, wg_ref, wu_ref, wd_ref, ga_ref, g2_ref, o_ref, h_ref, acc_ref):
    j = pl.program_id(1)

    @pl.when(j == 0)
    def _():
        h = _rms(x_ref[...], g1_ref[...]) * (1.0 + sc_ref[0]) + sh_ref[0]
        h_ref[...] = h.astype(BF16)
        acc_ref[...] = jnp.zeros_like(acc_ref)

    h = h_ref[...]
    a = jnp.dot(h, wg_ref[0], preferred_element_type=F32)
    u = jnp.dot(h, wu_ref[0], preferred_element_type=F32)
    acc_ref[...] += _dot(_silu(a) * u, wd_ref[0])

    @pl.when(j == pl.num_programs(1) - 1)
    def _():
        o_ref[...] = x_ref[...] + ga_ref[0] * _rms(acc_ref[...], g2_ref[...])


def _ffn(x, g1, sc, sh, w_gate, w_up, w_down, ga, g2, l, rows_per_mod, tm, tf=512):
    m, d = x.shape
    f = w_gate.shape[2]
    vec = pl.BlockSpec((1, d), lambda i, j: (0, 0))
    return pl.pallas_call(
        _ffn_kernel,
        out_shape=jax.ShapeDtypeStruct((m, d), F32),
        grid=(m // tm, f // tf),
        in_specs=[pl.BlockSpec((tm, d), lambda i, j: (i, 0)), vec,
                  _mod_spec(sc, rows_per_mod, tm), _mod_spec(sh, rows_per_mod, tm),
                  pl.BlockSpec((1, d, tf), lambda i, j: (l, 0, j)),
                  pl.BlockSpec((1, d, tf), lambda i, j: (l, 0, j)),
                  pl.BlockSpec((1, tf, d), lambda i, j: (l, j, 0)),
                  _mod_spec(ga, rows_per_mod, tm), vec],
        out_specs=pl.BlockSpec((tm, d), lambda i, j: (i, 0)),
        scratch_shapes=[pltpu.VMEM((tm, d), BF16), pltpu.VMEM((tm, d), F32)],
        compiler_params=_cparams(("arbitrary", "arbitrary")),
        name="ffn",
    )(x, g1, sc, sh, w_gate, w_up, w_down, ga, g2)


def _stack_heads(q, h0):
    rows = [q[:, (h0 + h) * HEAD_DIM:(h0 + h + 1) * HEAD_DIM] for h in range(NSA_HPG)]
    return jnp.concatenate(rows + [jnp.zeros((NSA_HPG, HEAD_DIM), F32)], axis=0)


def _nsa_sample_cmp_kernel(q_ref, kcb_ref, vcb_ref, oc_ref, idx_ref, *, n_sel, q_pos):
    nbp = kcb_ref.shape[2]
    cur = q_pos // BLOCK
    q = q_ref[0].astype(F32)
    for g in range(NSA_KV):
        qs = _stack_heads(q, g * NSA_HPG)
        jc = lax.broadcasted_iota(jnp.int32, (qs.shape[0], nbp), 1)
        cmask = (jc * BLOCK + (BLOCK - 1)) <= q_pos
        p_c = _masked_softmax(_dot_nt(qs, kcb_ref[0, g]) * SCALE, cmask)
        o_c = _dot(p_c, vcb_ref[0, g])
        for h in range(NSA_HPG):
            c0 = (g * NSA_HPG + h) * HEAD_DIM
            oc_ref[0, :, c0:c0 + HEAD_DIM] = o_c[h:h + 1]
        imp = p_c[0:1]
        for h in range(1, NSA_HPG):
            imp = imp + p_c[h:h + 1]
        jb = jc[0:1]
        forced = (jb == 0) | (jb == cur) | (jb == cur - 1)
        score = jnp.where(forced, jnp.inf, jnp.where(jb > cur, NEG_INF, imp))
        left = jb >= 0
        for r in range(n_sel):
            best = jnp.max(jnp.where(left, score, NEG_INF), axis=1, keepdims=True)
            pick = jnp.min(jnp.where(left & (score == best), jb, nbp), axis=1, keepdims=True)
            idx_ref[0, g, r:r + 1, :] = jnp.broadcast_to(pick, (1, HEAD_DIM))
            left = left & (jb != pick)


def _nsa_sample_cmp(q_rot, kcb, vcb, q_pos, n_blocks):
    b = q_rot.shape[0]
    nbp = kcb.shape[2]
    n_sel = min(TOPN, n_blocks)
    cb_spec = pl.BlockSpec((1, NSA_KV, nbp, HEAD_DIM), lambda bi: (bi, 0, 0, 0))
    return pl.pallas_call(
        functools.partial(_nsa_sample_cmp_kernel, n_sel=n_sel, q_pos=q_pos),
        out_shape=(jax.ShapeDtypeStruct((b, 1, NSA_Q_W), F32),
                   jax.ShapeDtypeStruct((b, NSA_KV, n_sel, HEAD_DIM), jnp.int32)),
        grid=(b,),
        in_specs=[pl.BlockSpec((1, 1, NSA_Q_W), lambda bi: (bi, 0, 0)), cb_spec, cb_spec],
        out_specs=(pl.BlockSpec((1, 1, NSA_Q_W), lambda bi: (bi, 0, 0)),
                   pl.BlockSpec((1, NSA_KV, n_sel, HEAD_DIM), lambda bi: (bi, 0, 0, 0))),
        compiler_params=_cparams(("arbitrary",)),
        name="nsa_sample_cmp",
    )(q_rot.reshape(b, 1, NSA_Q_W), kcb, vcb)


def _nsa_sample_attn_kernel(idx_ref, pt_ref, q_ref, sm_ref, oc_ref, cache_ref, newsel_ref, win_ref, newwin_ref,
                            o_ref, wino_ref, blk_buf, sem, *, l, npb, n_sel):
    bi, g = pl.program_id(0), pl.program_id(1)
    bpp = cache_ref.shape[3]
    copies, picks = [], []
    for n in range(n_sel):
        pick = idx_ref[bi, g, n]
        jp = jnp.minimum(pick, npb - 1)
        cp = pltpu.make_async_copy(cache_ref.at[pt_ref[bi, jp // bpp], l, g, jp % bpp],
                                   blk_buf.at[pl.ds(n * BLK_ROWS, BLK_ROWS), :], sem)
        cp.start()
        copies.append(cp)
        picks.append(pick)

    qs = _stack_heads(q_ref[0].astype(F32), 0)

    def row_scores(k):
        return jnp.sum(qs * k.astype(BF16).astype(F32), axis=-1, keepdims=True) * SCALE

    wb = win_ref.shape[3] // 2
    s_buf = _dot_nt(qs, win_ref[0, 0, 0, pl.ds(0, wb, stride=2), :]) * SCALE
    s_new = row_scores(newwin_ref[0, 0, 0:1, :])
    m = jnp.maximum(jnp.max(s_buf, axis=-1, keepdims=True), s_new)
    e_buf, e_new = jnp.exp(s_buf - m), jnp.exp(s_new - m)
    den = jnp.maximum(jnp.sum(e_buf, axis=-1, keepdims=True) + e_new, 1e-30)
    o_w = (_dot(e_buf, win_ref[0, 0, 0, pl.ds(1, wb, stride=2), :])
           + e_new * newwin_ref[0, 0, 1:2, :].astype(BF16).astype(F32)) / den
    wino_ref[0, 0, 0:2 * wb - 2, :] = win_ref[0, 0, 0, 2:2 * wb, :]
    wino_ref[0, 0, 2 * wb - 2:2 * wb, :] = newwin_ref[0, 0]

    for cp in copies:
        cp.wait()
    nk = n_sel * BLOCK
    slot = lax.broadcasted_iota(jnp.int32, (1, nk), 1) // BLOCK
    bias = jnp.zeros((1, nk), F32)
    has_new = picks[0] >= npb
    for n in range(n_sel):
        bias = jnp.where((slot == n) & (picks[n] >= npb), NEG_INF, bias)
        has_new = has_new | (picks[n] >= npb)
    s_blk = _dot_nt(qs, blk_buf[pl.ds(0, nk, stride=2), :]) * SCALE + bias
    s_row = jnp.where(has_new, row_scores(newsel_ref[0, 0, 0:1, :]), NEG_INF)
    m = jnp.maximum(jnp.max(s_blk, axis=-1, keepdims=True), s_row)
    m = jnp.where(m == NEG_INF, 0.0, m)
    e_blk, e_row = jnp.exp(s_blk - m), jnp.exp(s_row - m)
    den = jnp.maximum(jnp.sum(e_blk, axis=-1, keepdims=True) + e_row, 1e-30)
    o_s = (_dot(e_blk, blk_buf[pl.ds(1, nk, stride=2), :])
           + e_row * newsel_ref[0, 0, 1:2, :].astype(BF16).astype(F32)) / den

    gt = _sigmoid(sm_ref[0])
    for h in range(NSA_HPG):
        c0 = h * HEAD_DIM
        o = (gt[:, 3 * h:3 * h + 1] * oc_ref[0, :, c0:c0 + HEAD_DIM] + gt[:, 3 * h + 1:3 * h + 2] * o_s[h:h + 1]
             + gt[:, 3 * h + 2:3 * h + 3] * o_w[h:h + 1])
        o_ref[0, :, c0:c0 + HEAD_DIM] = o.astype(BF16)


def _nsa_sample_attn(idx, page_table, q_rot, small, o_c, cache_sel, new_sel, cache_win, new_win, l, npb):
    b = q_rot.shape[0]
    n_sel = idx.shape[2]
    bpp = cache_sel.shape[3]
    wb2 = cache_win.shape[3]
    gw = NSA_HPG * HEAD_DIM

    new_spec = pl.BlockSpec((1, 1, 2, HEAD_DIM), lambda bi, g, *_: (bi, g, 0, 0))
    grid_spec = pltpu.PrefetchScalarGridSpec(
        num_scalar_prefetch=2,
        grid=(b, NSA_KV),
        in_specs=[pl.BlockSpec((1, 1, gw), lambda bi, g, *_: (bi, 0, g)),
                  pl.BlockSpec((1, 1, HEAD_DIM), lambda bi, g, *_: (bi, 0, g)),
                  pl.BlockSpec((1, 1, gw), lambda bi, g, *_: (bi, 0, g)),
                  pl.BlockSpec(memory_space=pl.ANY),
                  new_spec,
                  pl.BlockSpec((1, 1, 1, wb2, HEAD_DIM), lambda bi, g, *_: (bi, l, g, 0, 0)),
                  new_spec],
        out_specs=(pl.BlockSpec((1, 1, gw), lambda bi, g, *_: (bi, 0, g)),
                   pl.BlockSpec((1, 1, wb2, HEAD_DIM), lambda bi, g, *_: (bi, g, 0, 0))),
        scratch_shapes=[pltpu.VMEM((n_sel * BLK_ROWS, HEAD_DIM), F32), pltpu.SemaphoreType.DMA(())],
    )
    return pl.pallas_call(
        functools.partial(_nsa_sample_attn_kernel, l=l, npb=npb, n_sel=n_sel),
        out_shape=(jax.ShapeDtypeStruct((b, 1, NSA_Q_W), BF16),
                   jax.ShapeDtypeStruct((b, NSA_KV, wb2, HEAD_DIM), F32)),
        grid_spec=grid_spec,
        compiler_params=_cparams(("arbitrary", "arbitrary")),
        name="nsa_sample_attn",
    )(idx, page_table, q_rot.reshape(b, 1, NSA_Q_W), small.reshape(b, 1, SMALL_W), o_c,
      cache_sel, new_sel, cache_win, new_win)


def _gdn_sample_kernel(x_ref, z_ref, sm_ref, buf_ref, s0_ref, cw_ref, prow_ref, nw_ref, eye_ref,
                       o_ref, s_out_ref, conv_out_ref):
    x = x_ref[0]
    buf = buf_ref[0, 0]
    y = x * cw_ref[CONV_W - 1:CONV_W, :]
    for w in range(CONV_W - 1):
        y = y + buf[w:w + 1, :] * cw_ref[w:w + 1, :]
    y = _silu(y)
    conv_out_ref[0, 0:CONV_W - 2, :] = buf[1:CONV_W - 1, :]
    conv_out_ref[0, CONV_W - 2:CONV_W - 1, :] = x

    def heads(off):
        return jnp.concatenate([y[:, off + h * HEAD_DIM:off + (h + 1) * HEAD_DIM] for h in range(GDN_HEADS)], axis=0)

    yq, yk, v = heads(0), heads(GDN_W), heads(2 * GDN_W)
    q = yq * lax.rsqrt(jnp.sum(yq * yq, axis=-1, keepdims=True) + EPS) * SCALE
    k = yk * lax.rsqrt(jnp.sum(yk * yk, axis=-1, keepdims=True) + EPS)
    sm = sm_ref[0]
    g_row = -jnp.exp(prow_ref[0:1, :]) * _softplus(sm[:, A_COL:A_COL + GDN_HEADS] + prow_ref[1:2, :])
    beta_row = _sigmoid(sm[:, B_COL:B_COL + GDN_HEADS])
    eg_row = jnp.exp(g_row)
    k_t = _dot_nt_exact(eye_ref[...], k)
    q_t = _dot_nt_exact(eye_ref[...], q)
    z = z_ref[0]
    for h in range(GDN_HEADS):
        s0 = s0_ref[0, 0, h]
        eg = eg_row[:, h:h + 1]
        kc = k_t[:, h:h + 1].astype(BF16).astype(F32)
        qc = q_t[:, h:h + 1].astype(BF16).astype(F32)
        s0b = s0.astype(BF16).astype(F32)
        ks = jnp.sum(kc * beta_row[:, h:h + 1] * eg * s0b, axis=0, keepdims=True)
        v_new = v[h:h + 1] * beta_row[:, h:h + 1] - ks
        vb = v_new.astype(BF16).astype(F32)
        s1 = s0 * eg + kc * vb
        qe = (q_t[:, h:h + 1] * eg).astype(BF16).astype(F32)
        o = jnp.sum(qe * s0b, axis=0, keepdims=True) + jnp.sum(qc * kc, axis=0, keepdims=True) * vb
        s_out_ref[0, h] = s1
        sl = slice(h * HEAD_DIM, (h + 1) * HEAD_DIM)
        o = _rms(o, nw_ref[...]) * _silu(z[:, sl])
        o_ref[0, :, sl] = o.astype(BF16)


def _gdn_sample(proj, small, state_conv, state_gdn, conv_w, a_log, dt_bias, norm_w, l, b):
    prow = jnp.stack([a_log, dt_bias], axis=0)
    rows = proj.shape[0]
    return pl.pallas_call(
        _gdn_sample_kernel,
        out_shape=(jax.ShapeDtypeStruct((b, 1, GDN_W), BF16),
                   jax.ShapeDtypeStruct((b, GDN_HEADS, HEAD_DIM, HEAD_DIM), F32),
                   jax.ShapeDtypeStruct((b, CONV_W - 1, CONV_CH), F32)),
        grid=(b,),
        in_specs=[pl.BlockSpec((1, 1, CONV_CH), lambda bi: (bi, 0, OFF_CONV // CONV_CH)),
                  pl.BlockSpec((1, 1, GDN_W), lambda bi: (bi, 0, OFF_Z // GDN_W)),
                  pl.BlockSpec((1, 1, SMALL_W), lambda bi: (bi, 0, 0)),
                  pl.BlockSpec((1, 1, CONV_W - 1, CONV_CH), lambda bi: (bi, l, 0, 0)),
                  pl.BlockSpec((1, 1, GDN_HEADS, HEAD_DIM, HEAD_DIM), lambda bi: (bi, l, 0, 0, 0)),
                  pl.BlockSpec((CONV_W, CONV_CH), lambda bi: (0, 0)),
                  pl.BlockSpec((2, GDN_HEADS), lambda bi: (0, 0)),
                  pl.BlockSpec((1, HEAD_DIM), lambda bi: (0, 0)),
                  pl.BlockSpec((HEAD_DIM, HEAD_DIM), lambda bi: (0, 0))],
        out_specs=(pl.BlockSpec((1, 1, GDN_W), lambda bi: (bi, 0, 0)),
                   pl.BlockSpec((1, GDN_HEADS, HEAD_DIM, HEAD_DIM), lambda bi: (bi, 0, 0, 0)),
                   pl.BlockSpec((1, CONV_W - 1, CONV_CH), lambda bi: (bi, 0, 0))),
        compiler_params=_cparams(("arbitrary",)),
        name="gdn_sample",
    )(proj.reshape(rows, 1, MAIN_W), proj.reshape(rows, 1, MAIN_W), small.reshape(rows, 1, SMALL_W),
      state_conv, state_gdn, conv_w, prow, norm_w.reshape(1, HEAD_DIM), jnp.eye(HEAD_DIM, dtype=F32))


def _repack_w_in(w_in):
    kvw = NSA_KV * HEAD_DIM
    o_q = 0
    o_kv = [NSA_Q_W + i * kvw for i in range(6)]
    o_gl = NSA_Q_W + 6 * kvw
    o_conv = o_gl + 3 * NSA_HEADS
    o_a = o_conv + CONV_CH
    o_b = o_a + GDN_HEADS
    o_z = o_b + GDN_HEADS

    def cols(a, n):
        return w_in[:, :, a:a + n]

    pieces = [cols(o_conv, CONV_CH), cols(o_z, GDN_W), cols(o_q, NSA_Q_W)]
    for pair in range(3):
        for g in range(NSA_KV):
            pieces.append(cols(o_kv[2 * pair] + g * HEAD_DIM, HEAD_DIM))
            pieces.append(cols(o_kv[2 * pair + 1] + g * HEAD_DIM, HEAD_DIM))
    main = jnp.concatenate(pieces, axis=-1).astype(BF16)
    depth, d, _ = w_in.shape
    ngl = 3 * NSA_HPG
    small = jnp.concatenate([
        cols(o_gl, ngl), cols(o_a, GDN_HEADS), cols(o_b, GDN_HEADS),
        jnp.zeros((depth, d, HEAD_DIM - ngl - 2 * GDN_HEADS), w_in.dtype),
        cols(o_gl + ngl, ngl), jnp.zeros((depth, d, HEAD_DIM - ngl), w_in.dtype)], axis=-1).astype(BF16)
    return main, small


def _mods(mod_l, lo, hi, broadcast):
    d = mod_l.shape[1] // 6
    out = []
    for k in range(6):
        m = mod_l[lo:hi, k * d:(k + 1) * d]
        out.append(m[:, None, :] if broadcast else m[None])
    return out


def kernel(x_prompt, x_sample, cache_cmp_kv, cache_sel_kv, cache_win_kv, state_gdn, state_conv, page_table,
           c_prompt, c_sample, w_ada, b_ada, g_pre_mix, w_in, cmp_pe, cmp_w1, cmp_w2, conv_w, gdn_a_log,
           gdn_dt_bias, gdn_norm, w_out, g_post_mix, g_pre_ffn, w_gate, w_up, w_down, g_post_ffn):
    bp, t, d = x_prompt.shape
    bs = x_sample.shape[0]
    depth = w_in.shape[0]
    n_pool, _, _, page, _, _ = cache_cmp_kv.shape
    past = page_table.shape[1] * page
    npb = past // BLOCK
    bpp = page // BLOCK
    wb = cache_win_kv.shape[3]
    ts = 16
    assert x_sample.shape[1] == 1 and bs <= ts and t % BLOCK == 0 and page % BLOCK == 0

    c_all = jnp.concatenate([c_sample, jnp.zeros((ts - bs, d), F32), c_prompt,
                             jnp.zeros((-bp % 8, d), F32)], axis=0)
    mod = _ada_mod(c_all, w_ada, b_ada)

    w_main, w_small = _repack_w_in(w_in)
    w_out_b, w_gate_b, w_up_b, w_down_b = (w.astype(BF16) for w in (w_out, w_gate, w_up, w_down))
    rope_p = _rope_tables(jnp.arange(t))
    rope_s = _rope_tables(jnp.full((ts,), past))

    tm = min(512, t)
    xp = x_prompt.reshape(bp * t, d)
    xs = jnp.concatenate([x_sample.reshape(bs, d), jnp.zeros((ts - bs, d), F32)], axis=0)
    cache_cmp_blocks = cache_cmp_kv.reshape(n_pool, depth, NSA_KV * bpp, BLK_ROWS, HEAD_DIM)
    cache_sel_blocks = cache_sel_kv.reshape(n_pool, depth, NSA_KV, bpp, BLK_ROWS, HEAD_DIM)
    cache_win_rows = cache_win_kv.reshape(bs, depth, NSA_KV, 2 * wb, HEAD_DIM)
    nbp = -(-(npb + 1) // HEAD_DIM) * HEAD_DIM

    p_states, s_states = [], []
    for l in range(depth):
        gpm, gpo, gpf, gpof = (g[l][None] for g in (g_pre_mix, g_post_mix, g_pre_ffn, g_post_ffn))
        cw = _compress_weights(cmp_pe, cmp_w1, cmp_w2, l)

        sh1, sc1, ga1, sh2, sc2, ga2 = _mods(mod[l], ts, ts + bp, True)
        proj, small = _in_proj(xp, gpm, sc1, sh1, w_main, w_small, l, t, tm)
        q_rot, cmp_st, sel_st, win_st, att = _post_proj(proj, rope_p, bp, t, min(256, t))
        nb = t // BLOCK
        kcb, vcb = _compress(cmp_st.reshape(bp * NSA_KV * 2 * t, HEAD_DIM), cw, math.gcd(bp * NSA_KV * nb, 128))
        kcb, vcb = (a.reshape(bp, NSA_KV, nb, HEAD_DIM) for a in (kcb, vcb))
        o_nsa = _nsa_prompt(q_rot, small, kcb, vcb, att, bp, t)
        o_gdn, s_fin, conv_fin = _gdn_prompt(proj, small, conv_w[l], gdn_a_log[l], gdn_dt_bias[l], gdn_norm[l], bp, t)
        xp = _out_proj(o_nsa, o_gdn, w_out_b, xp, ga1, gpo, l, t, tm)
        xp = _ffn(xp, gpf, sc2, sh2, w_gate_b, w_up_b, w_down_b, ga2, gpof, l, t, tm)
        wl = min(WINDOW, t)
        p_states.append((cmp_st.reshape(bp, NSA_KV, t, 2, HEAD_DIM), sel_st.reshape(bp, NSA_KV, t, 2, HEAD_DIM),
                         win_st[:, :, 2 * (t - wl):].reshape(bp, NSA_KV, wl, 2, HEAD_DIM), s_fin, conv_fin))

        sh1, sc1, ga1, sh2, sc2, ga2 = _mods(mod[l], 0, ts, False)
        proj, small = _in_proj(xs, gpm, sc1, sh1, w_main, w_small, l, ts, ts)
        q_rot, cmp_new, sel_new, win_new, _ = _post_proj(proj, rope_s, 1, ts, ts)
        cmp_new, sel_new, win_new = (a.reshape(NSA_KV, ts, 2, HEAD_DIM)[:, :bs].transpose(1, 0, 2, 3)
                                     for a in (cmp_new, sel_new, win_new))
        kc_past, vc_past = _compress_paged(cache_cmp_blocks, page_table, cw, l)
        new_blk = jnp.pad(cmp_new.reshape(bs * NSA_KV, 2, HEAD_DIM), ((0, 0), (0, BLK_ROWS - 2), (0, 0)))
        kc_new, vc_new = _compress(new_blk.reshape(bs * NSA_KV * BLK_ROWS, HEAD_DIM), cw, bs * NSA_KV)

        def summaries(past_rows, new_rows):
            a = past_rows.reshape(bs, past // page, NSA_KV, bpp, HEAD_DIM).transpose(0, 2, 1, 3, 4)
            a = a.reshape(bs, NSA_KV, npb, HEAD_DIM)
            a = jnp.concatenate([a, new_rows.reshape(bs, NSA_KV, 1, HEAD_DIM)], axis=2)
            return jnp.pad(a, ((0, 0), (0, 0), (0, nbp - npb - 1), (0, 0)))

        kcb, vcb = summaries(kc_past, kc_new), summaries(vc_past, vc_new)
        o_c, idx = _nsa_sample_cmp(q_rot[:bs], kcb, vcb, past, npb + 1)
        o_nsa, win_out = _nsa_sample_attn(idx[..., 0], page_table, q_rot[:bs], small[:bs], o_c, cache_sel_blocks,
                                          sel_new, cache_win_rows, win_new, l, npb)
        o_gdn, s_fin, conv_fin = _gdn_sample(proj, small, state_conv, state_gdn, conv_w[l], gdn_a_log[l],
                                             gdn_dt_bias[l], gdn_norm[l], l, bs)
        pad_rows = ((0, ts - bs), (0, 0))
        o_nsa = jnp.pad(o_nsa.reshape(bs, NSA_Q_W), pad_rows)
        o_gdn = jnp.pad(o_gdn.reshape(bs, GDN_W), pad_rows)
        xs = _out_proj(o_nsa, o_gdn, w_out_b, xs, ga1, gpo, l, ts, ts)
        xs = _ffn(xs, gpf, sc2, sh2, w_gate_b, w_up_b, w_down_b, ga2, gpof, l, ts, ts)
        s_states.append((cmp_new.reshape(bs, NSA_KV, 1, 2, HEAD_DIM), sel_new.reshape(bs, NSA_KV, 1, 2, HEAD_DIM),
                         win_out.reshape(bs, NSA_KV, wb, 2, HEAD_DIM), s_fin, conv_fin))

    p_st = [jnp.stack(s, axis=1) for s in zip(*p_states)]
    s_st = [jnp.stack(s, axis=1) for s in zip(*s_states)]
    return (xp.reshape(bp, t, d), xs[:bs].reshape(bs, 1, d), *p_st, *s_st)
```

```python
import functools
import math

import jax
import jax.numpy as jnp
from jax import lax
from jax.experimental import pallas as pl
from jax.experimental.pallas import tpu as pltpu

F32 = jnp.float32
BF16 = jnp.bfloat16

HEAD_DIM = 128
NSA_HEADS = 8
NSA_KV = 2
NSA_HPG = NSA_HEADS // NSA_KV
GDN_HEADS = 8
ROT_DIM = HEAD_DIM // 4
ROPE_THETA = 500000.0
BLOCK = 64
TOPN = 16
WINDOW = 512
CMP_HID = 256
CONV_W = 4
GDN_CHUNK = 64
EPS = 1e-6
KV_ROW = 2 * HEAD_DIM
BLK_ROWS = 2 * BLOCK
BLK_PITCH = BLK_ROWS + 8
NSA_Q_W = NSA_HEADS * HEAD_DIM
GDN_W = GDN_HEADS * HEAD_DIM
CONV_CH = 3 * GDN_W
SMALL_W = 2 * HEAD_DIM
A_COL = 12
B_COL = 20
OFF_CONV = 0
OFF_Z = CONV_CH
OFF_Q = OFF_Z + GDN_W
OFF_CMP = OFF_Q + NSA_Q_W
OFF_SEL = OFF_CMP + NSA_KV * KV_ROW
OFF_WIN = OFF_SEL + NSA_KV * KV_ROW
MAIN_W = OFF_WIN + NSA_KV * KV_ROW
VMEM_LIMIT = 48 * 1024 * 1024
NEG_INF = float("-inf")
SCALE = HEAD_DIM ** -0.5


def _cparams(sem):
    return pltpu.CompilerParams(dimension_semantics=sem, vmem_limit_bytes=VMEM_LIMIT)


def _dot(a, b):
    return jnp.dot(a.astype(BF16), b.astype(BF16), preferred_element_type=F32)


def _dot_nt(a, b):
    return lax.dot_general(a.astype(BF16), b.astype(BF16), (((1,), (1,)), ((), ())),
                           preferred_element_type=F32)


def _dot_exact(a, b):
    return jnp.dot(a, b, preferred_element_type=F32, precision=lax.Precision.HIGHEST)


def _dot_nt_exact(a, b):
    return lax.dot_general(a, b, (((1,), (1,)), ((), ())), preferred_element_type=F32,
                           precision=lax.Precision.HIGHEST)


def _sigmoid(x):
    return 1.0 / (1.0 + jnp.exp(-x))


def _silu(x):
    return x * _sigmoid(x)


def _rms(x, g):
    return x * lax.rsqrt(jnp.mean(x * x, axis=-1, keepdims=True) + EPS) * g


def _masked_softmax(s, mask):
    s = jnp.where(mask, s, NEG_INF)
    m = jnp.max(s, axis=-1, keepdims=True)
    m = jnp.where(m == NEG_INF, 0.0, m)
    e = jnp.where(mask, jnp.exp(s - m), 0.0)
    return e / jnp.maximum(jnp.sum(e, axis=-1, keepdims=True), 1e-30)


def _ada_kernel(c_ref, w_ref, b_ref, o_ref):
    o_ref[0] = _dot(_silu(c_ref[...]), w_ref[0]) + b_ref[0]


def _ada_mod(c_all, w_ada, b_ada, tn=1024):
    depth, d, n = w_ada.shape
    r = c_all.shape[0]
    return pl.pallas_call(
        _ada_kernel,
        out_shape=jax.ShapeDtypeStruct((depth, r, n), F32),
        grid=(depth, n // tn),
        in_specs=[pl.BlockSpec((r, d), lambda l, j: (0, 0)),
                  pl.BlockSpec((1, d, tn), lambda l, j: (l, 0, j)),
                  pl.BlockSpec((1, 1, tn), lambda l, j: (l, 0, j))],
        out_specs=pl.BlockSpec((1, r, tn), lambda l, j: (l, 0, j)),
        compiler_params=_cparams(("arbitrary", "arbitrary")),
        name="ada_mod",
    )(c_all, w_ada, b_ada.reshape(depth, 1, n))


def _inproj_kernel(x_ref, g_ref, sc_ref, sh_ref, w_ref, ws_ref, o_ref, os_ref, h_ref):
    @pl.when(pl.program_id(1) == 0)
    def _():
        h = _rms(x_ref[...], g_ref[...]) * (1.0 + sc_ref[0]) + sh_ref[0]
        hb = h.astype(BF16)
        h_ref[...] = hb
        os_ref[...] = jnp.dot(hb, ws_ref[0], preferred_element_type=F32)

    o_ref[...] = jnp.dot(h_ref[...], w_ref[0], preferred_element_type=F32)


def _mod_spec(mod, rows_per_mod, tm):
    d = mod.shape[-1]
    if mod.shape[1] == 1:
        return pl.BlockSpec((1, 1, d), lambda i, *_: ((i * tm) // rows_per_mod, 0, 0))
    return pl.BlockSpec((1, tm, d), lambda i, *_: (0, i, 0))


def _in_proj(x, g, sc, sh, w_main, w_small, l, rows_per_mod, tm, tn=MAIN_W // 4):
    m, d = x.shape
    return pl.pallas_call(
        _inproj_kernel,
        out_shape=(jax.ShapeDtypeStruct((m, MAIN_W), F32), jax.ShapeDtypeStruct((m, SMALL_W), F32)),
        grid=(m // tm, MAIN_W // tn),
        in_specs=[pl.BlockSpec((tm, d), lambda i, j: (i, 0)),
                  pl.BlockSpec((1, d), lambda i, j: (0, 0)),
                  _mod_spec(sc, rows_per_mod, tm), _mod_spec(sh, rows_per_mod, tm),
                  pl.BlockSpec((1, d, tn), lambda i, j: (l, 0, j)),
                  pl.BlockSpec((1, d, SMALL_W), lambda i, j: (l, 0, 0))],
        out_specs=(pl.BlockSpec((tm, tn), lambda i, j: (i, j)),
                   pl.BlockSpec((tm, SMALL_W), lambda i, j: (i, 0))),
        scratch_shapes=[pltpu.VMEM((tm, d), BF16)],
        compiler_params=_cparams(("arbitrary", "arbitrary")),
        name="in_proj",
    )(x, g, sc, sh, w_main, w_small)


def _rope(x, c, s1, s2):
    half = ROT_DIM // 2
    return x * c + pltpu.roll(x, HEAD_DIM - half, 1) * s1 + pltpu.roll(x, half, 1) * s2


ATT_W = 5 * HEAD_DIM


def _postproj_kernel(q_ref, cmp_ref, sel_ref, win_ref, c_ref, s1_ref, s2_ref,
                     qo_ref, cmpo_ref, selo_ref, wino_ref, att_ref):
    c, s1, s2 = c_ref[...], s1_ref[...], s2_ref[...]
    for h in range(NSA_HEADS):
        sl = slice(h * HEAD_DIM, (h + 1) * HEAD_DIM)
        qo_ref[:, sl] = _rope(q_ref[:, sl], c, s1, s2).astype(BF16)
    tt = c.shape[0]
    blk = (pl.program_id(1) * tt + lax.broadcasted_iota(jnp.int32, (tt, HEAD_DIM), 0)) // BLOCK
    onehot = jnp.where(lax.broadcasted_iota(jnp.int32, (tt, HEAD_DIM), 1) == blk, 1.0, 0.0).astype(BF16)
    for src, dst, col in ((cmp_ref, cmpo_ref, None), (sel_ref, selo_ref, 0), (win_ref, wino_ref, 3 * HEAD_DIM)):
        for g in range(NSA_KV):
            base = g * KV_ROW
            k = _rope(src[:, base:base + HEAD_DIM], c, s1, s2)
            v = src[:, base + HEAD_DIM:base + KV_ROW]
            dst[0, g, pl.ds(0, tt, stride=2), :] = k
            dst[0, g, pl.ds(1, tt, stride=2), :] = v
            if col == 0:
                att_ref[0, g, :, 0:HEAD_DIM] = k.astype(BF16)
                att_ref[0, g, :, HEAD_DIM:2 * HEAD_DIM] = onehot
                att_ref[0, g, :, 2 * HEAD_DIM:3 * HEAD_DIM] = v.astype(BF16)
            elif col is not None:
                att_ref[0, g, :, col:col + HEAD_DIM] = k.astype(BF16)
                att_ref[0, g, :, col + HEAD_DIM:col + 2 * HEAD_DIM] = v.astype(BF16)


def _post_proj(proj, rope_tabs, b, t, tt):
    nt = t // tt
    state = jax.ShapeDtypeStruct((b, NSA_KV, 2 * t, HEAD_DIM), F32)
    att = jax.ShapeDtypeStruct((b, NSA_KV, t, ATT_W), BF16)
    kvw = NSA_KV * KV_ROW
    tab_spec = pl.BlockSpec((tt, HEAD_DIM), lambda bi, ti: (ti, 0))
    st_spec = pl.BlockSpec((1, NSA_KV, 2 * tt, HEAD_DIM), lambda bi, ti: (bi, 0, ti, 0))
    att_spec = pl.BlockSpec((1, NSA_KV, tt, ATT_W), lambda bi, ti: (bi, 0, ti, 0))
    return pl.pallas_call(
        _postproj_kernel,
        out_shape=(jax.ShapeDtypeStruct((b * t, NSA_Q_W), BF16), state, state, state, att),
        grid=(b, nt),
        in_specs=[pl.BlockSpec((tt, NSA_Q_W), lambda bi, ti: (bi * nt + ti, OFF_Q // NSA_Q_W)),
                  pl.BlockSpec((tt, kvw), lambda bi, ti: (bi * nt + ti, OFF_CMP // kvw)),
                  pl.BlockSpec((tt, kvw), lambda bi, ti: (bi * nt + ti, OFF_SEL // kvw)),
                  pl.BlockSpec((tt, kvw), lambda bi, ti: (bi * nt + ti, OFF_WIN // kvw)),
                  tab_spec, tab_spec, tab_spec],
        out_specs=(pl.BlockSpec((tt, NSA_Q_W), lambda bi, ti: (bi * nt + ti, 0)),
                   st_spec, st_spec, st_spec, att_spec),
        compiler_params=_cparams(("arbitrary", "arbitrary")),
        name="post_proj",
    )(proj, proj, proj, proj, *rope_tabs)


def _rope_tables(pos):
    half = ROT_DIM // 2
    inv = ROPE_THETA ** (-jnp.arange(half, dtype=F32) * 2.0 / ROT_DIM)
    ang = pos.astype(F32)[:, None] * inv
    cos, sin = jnp.cos(ang), jnp.sin(ang)
    n = pos.shape[0]
    rest = HEAD_DIM - ROT_DIM
    c = jnp.concatenate([cos, cos, jnp.ones((n, rest), F32)], axis=1)
    s1 = jnp.concatenate([-sin, jnp.zeros((n, HEAD_DIM - half), F32)], axis=1)
    s2 = jnp.concatenate([jnp.zeros((n, half), F32), sin, jnp.zeros((n, rest), F32)], axis=1)
    return c, s1, s2


def _gelu_tanh(x):
    return 0.5 * x * (1.0 + jnp.tanh(math.sqrt(2.0 / math.pi) * (x + 0.044715 * (x * x * x))))


def _compress_blocks(x_ref, base, pitch, nblk, pe_ref, w1k_ref, w1v_ref, w2k_ref, w2v_ref, ko_ref, vo_ref):
    hk = jnp.zeros((nblk, CMP_HID), F32)
    hv = jnp.zeros((nblk, CMP_HID), F32)
    for r in range(BLOCK):
        xk = x_ref[pl.ds(base + 2 * r, nblk, stride=pitch), :] + pe_ref[0, r:r + 1, :]
        xv = x_ref[pl.ds(base + 2 * r + 1, nblk, stride=pitch), :] + pe_ref[1, r:r + 1, :]
        hk = hk + _dot(xk, w1k_ref[r])
        hv = hv + _dot(xv, w1v_ref[r])
    ko_ref[...] = _dot(_gelu_tanh(hk), w2k_ref[...])
    vo_ref[...] = _dot(_gelu_tanh(hv), w2v_ref[...])


def _compress_kernel(x_ref, *rest):
    _compress_blocks(x_ref, 0, BLK_ROWS, x_ref.shape[0] // BLK_ROWS, *rest)


def _compress_weights(cmp_pe, cmp_w1, cmp_w2, l):
    w1 = cmp_w1[l].astype(BF16).reshape(2, BLOCK, HEAD_DIM, CMP_HID)
    w2 = cmp_w2[l].astype(BF16)
    return cmp_pe[l], w1[0], w1[1], w2[0], w2[1]


def _cw_specs(nargs):
    z = (0,) * 3
    return [pl.BlockSpec((2, BLOCK, HEAD_DIM), lambda *a: z),
            pl.BlockSpec((BLOCK, HEAD_DIM, CMP_HID), lambda *a: z),
            pl.BlockSpec((BLOCK, HEAD_DIM, CMP_HID), lambda *a: z),
            pl.BlockSpec((CMP_HID, HEAD_DIM), lambda *a: (0, 0)),
            pl.BlockSpec((CMP_HID, HEAD_DIM), lambda *a: (0, 0))]


def _compress(x_rows, cw, tr):
    nblocks = x_rows.shape[0] // BLK_ROWS
    out = jax.ShapeDtypeStruct((nblocks, HEAD_DIM), F32)
    return pl.pallas_call(
        _compress_kernel,
        out_shape=(out, out),
        grid=(nblocks // tr,),
        in_specs=[pl.BlockSpec((tr * BLK_ROWS, HEAD_DIM), lambda i: (i, 0))] + _cw_specs(1),
        out_specs=(pl.BlockSpec((tr, HEAD_DIM), lambda i: (i, 0)),) * 2,
        compiler_params=_cparams(("arbitrary",)),
        name="compress",
    )(x_rows, *cw)


MAX_PAGES_PER_STEP = 32


def _compress_paged_kernel(pt_ref, cache_ref, pe_ref, w1k_ref, w1v_ref, w2k_ref, w2v_ref, ko_ref, vo_ref,
                           xbuf, sem, *, l):
    bpp = cache_ref.shape[2]
    nblk = ko_ref.shape[0]
    pages_per_step = nblk // bpp
    ns = pl.num_programs(1)
    step = pl.program_id(0) * ns + pl.program_id(1)
    total = pl.num_programs(0) * ns

    def copies(st, slot):
        bi, s = st // ns, st % ns
        out = []
        for k in range(pages_per_step):
            pg = pt_ref[bi, s * pages_per_step + k]
            for j in range(bpp):
                row0 = pl.multiple_of(slot * (nblk * BLK_PITCH) + (k * bpp + j) * BLK_PITCH, 8)
                out.append(pltpu.make_async_copy(cache_ref.at[pg, l, j], xbuf.at[pl.ds(row0, BLK_ROWS), :],
                                                 sem.at[slot]))
        return out

    slot = step % 2

    @pl.when(step == 0)
    def _():
        for c in copies(step, slot):
            c.start()

    @pl.when(step + 1 < total)
    def _():
        for c in copies(step + 1, 1 - slot):
            c.start()

    for c in copies(step, slot):
        c.wait()
    _compress_blocks(xbuf, slot * (nblk * BLK_PITCH), BLK_PITCH, nblk,
                     pe_ref, w1k_ref, w1v_ref, w2k_ref, w2v_ref, ko_ref, vo_ref)


def _compress_paged(cache, page_table, cw, l):
    b, n_pages = page_table.shape
    bpp = cache.shape[2]
    pages_per_step = math.gcd(n_pages, MAX_PAGES_PER_STEP)
    steps = n_pages // pages_per_step
    tr = pages_per_step * bpp
    out = jax.ShapeDtypeStruct((b * steps * tr, HEAD_DIM), F32)
    grid_spec = pltpu.PrefetchScalarGridSpec(
        num_scalar_prefetch=1,
        grid=(b, steps),
        in_specs=[pl.BlockSpec(memory_space=pl.ANY)] + _cw_specs(3),
        out_specs=(pl.BlockSpec((tr, HEAD_DIM), lambda bi, s, pt: (bi * steps + s, 0)),) * 2,
        scratch_shapes=[pltpu.VMEM((2 * tr * BLK_PITCH, HEAD_DIM), F32), pltpu.SemaphoreType.DMA((2,))],
    )
    return pl.pallas_call(
        functools.partial(_compress_paged_kernel, l=l),
        out_shape=(out, out),
        grid_spec=grid_spec,
        compiler_params=_cparams(("arbitrary", "arbitrary")),
        name="compress_paged",
    )(page_table, cache, *cw)


def _stable_topn_masks(scores, n_sel):
    rows, nb = scores[0].shape
    tn = (((0,), (0,)), ((), ()))
    eye_r = jnp.where(lax.broadcasted_iota(jnp.int32, (rows, rows), 0)
                      == lax.broadcasted_iota(jnp.int32, (rows, rows), 1), 1.0, 0.0)
    sts = [lax.dot_general(sc, eye_r, tn, preferred_element_type=F32, precision=lax.Precision.HIGHEST)
           for sc in scores]
    jrow = lax.broadcasted_iota(jnp.int32, (nb, rows), 0)
    ranks = [jnp.zeros((nb, rows), F32) for _ in scores]
    for k in range(nb):
        for p, st in enumerate(sts):
            ck = st[k:k + 1, :]
            beats = (ck > st) | ((ck == st) & (jrow > k))
            ranks[p] = ranks[p] + jnp.where(beats, 1.0, 0.0)
    eye_n = jnp.where(lax.broadcasted_iota(jnp.int32, (nb, nb), 0)
                      == lax.broadcasted_iota(jnp.int32, (nb, nb), 1), 1.0, 0.0).astype(BF16)
    return [lax.dot_general(jnp.where(r < n_sel, 1.0, 0.0).astype(BF16), eye_n, tn, preferred_element_type=F32)
            for r in ranks]


MASK_BIAS = -(2.0 ** 100)
SCORE_C = SCALE * 1.4426950408889634


def _softmax_pv(problems):
    ms = []
    for parts in problems:
        m = jnp.max(parts[0][0], axis=-1, keepdims=True)
        for s, _ in parts[1:]:
            m = jnp.maximum(m, jnp.max(s, axis=-1, keepdims=True))
        ms.append(m)
    dens = [0.0 for _ in problems]
    accs = [0.0 for _ in problems]
    for j in range(len(problems[0])):
        for p, parts in enumerate(problems):
            s, v = parts[j]
            e = jnp.exp2(s - ms[p])
            dens[p] = dens[p] + jnp.sum(e, axis=-1, keepdims=True)
            accs[p] = accs[p] + jnp.dot(e.astype(BF16), v, preferred_element_type=F32)
    return [a / d for a, d in zip(accs, dens)]


def _nsa_prompt_kernel(q_ref, sm_ref, kcb_ref, vcb_ref, att_ref, o_ref, os_ref, *, n_sel, wl, n_cls):
    i = pl.program_id(1)
    groups = range(NSA_KV)
    t_all = att_ref.shape[2]
    nb = kcb_ref.shape[2]
    rows = NSA_HPG * BLOCK
    nt = (((1,), (1,)), ((), ()))
    q = q_ref[...]
    qs = [jnp.concatenate([q[:, (g * NSA_HPG + h) * HEAD_DIM:(g * NSA_HPG + h + 1) * HEAD_DIM]
                           for h in range(NSA_HPG)], axis=0) for g in groups]
    q0 = i * BLOCK
    tq = lax.broadcasted_iota(jnp.int32, (rows, 1), 0) & (BLOCK - 1)
    qpos = q0 + tq

    start = pl.multiple_of(jnp.clip(q0 - WINDOW, 0, t_all - wl), BLOCK)
    wpos = start + lax.broadcasted_iota(jnp.int32, (rows, wl), 1)
    wbias = jnp.where((wpos <= qpos) & (wpos >= qpos - WINDOW), 0.0, MASK_BIAS)
    s_w = [lax.dot_general(qs[g], att_ref[0, g, pl.ds(start, wl), 3 * HEAD_DIM:4 * HEAD_DIM], nt,
                           preferred_element_type=F32) * SCORE_C + wbias for g in groups]
    o_w = _softmax_pv([[(s_w[g], att_ref[0, g, pl.ds(start, wl), 4 * HEAD_DIM:5 * HEAD_DIM])] for g in groups])

    jc = lax.broadcasted_iota(jnp.int32, (rows, nb), 1)
    cmask = (jc * BLOCK + (BLOCK - 1)) <= qpos
    s_c = [jnp.where(cmask, _dot_nt(qs[g], kcb_ref[0, g]) * SCALE, NEG_INF) for g in groups]
    m_c = [jnp.max(s, axis=-1, keepdims=True) for s in s_c]
    m_c = [jnp.where(m == NEG_INF, 0.0, m) for m in m_c]
    e_c = [jnp.where(cmask, jnp.exp(s - m), 0.0) for s, m in zip(s_c, m_c)]
    p_c = [e / jnp.maximum(jnp.sum(e, axis=-1, keepdims=True), 1e-30) for e in e_c]
    o_c = [_dot(p_c[g], vcb_ref[0, g]) for g in groups]

    jb = lax.broadcasted_iota(jnp.int32, (BLOCK, nb), 1)
    forced = (jb == 0) | (jb == i) | (jb == i - 1)
    scores = []
    for g in groups:
        imp = p_c[g][0:BLOCK]
        for h in range(1, NSA_HPG):
            imp = imp + p_c[g][h * BLOCK:(h + 1) * BLOCK]
        scores.append(jnp.where(forced, 2.0 * NSA_HPG, jnp.where(jb > i, -1.0, imp)))
    sel = _stable_topn_masks(scores, n_sel)
    r0 = pl.multiple_of(q0, BLOCK)
    tl = lax.broadcasted_iota(jnp.int32, (rows, BLOCK), 1)
    dbias = jnp.where(tl <= tq, 0.0, MASK_BIAS)
    q_aug, s_d, v_d = [], [], []
    for g in groups:
        blk_bias = jnp.where((jb < i) & (sel[g] > 0.5), 0.0, MASK_BIAS)
        blk_bias = jnp.concatenate([blk_bias, jnp.zeros((BLOCK, HEAD_DIM - nb), F32)], axis=1).astype(BF16)
        q_aug.append(jnp.concatenate([qs[g], jnp.concatenate([blk_bias] * NSA_HPG, axis=0)], axis=1))
        s_d.append(lax.dot_general(qs[g], att_ref[0, g, pl.ds(r0, BLOCK), 0:HEAD_DIM], nt,
                                   preferred_element_type=F32) * SCORE_C + dbias)
        v_d.append(att_ref[0, g, pl.ds(r0, BLOCK), 2 * HEAD_DIM:3 * HEAD_DIM])
    per_cls = nb // n_cls
    for c in range(n_cls):
        nk = (c + 1) * per_cls * BLOCK

        @pl.when((i >= c * per_cls) & (i < (c + 1) * per_cls))
        def _():
            s_b = [lax.dot_general(q_aug[g], att_ref[0, g, 0:nk, 0:2 * HEAD_DIM], nt,
                                   preferred_element_type=F32) * SCORE_C for g in groups]
            o = _softmax_pv([[(s_b[g], att_ref[0, g, 0:nk, 2 * HEAD_DIM:3 * HEAD_DIM]), (s_d[g], v_d[g])]
                             for g in groups])
            for g in groups:
                os_ref[g] = o[g]

    for g in groups:
        o_s = os_ref[g]
        gt = _sigmoid(sm_ref[:, g * HEAD_DIM:(g + 1) * HEAD_DIM])
        for h in range(NSA_HPG):
            r = slice(h * BLOCK, (h + 1) * BLOCK)
            o = (gt[:, 3 * h:3 * h + 1] * o_c[g][r] + gt[:, 3 * h + 1:3 * h + 2] * o_s[r]
                 + gt[:, 3 * h + 2:3 * h + 3] * o_w[g][r])
            c0 = (g * NSA_HPG + h) * HEAD_DIM
            o_ref[:, c0:c0 + HEAD_DIM] = o.astype(BF16)


def _nsa_prompt(q_rot, small, kcb, vcb, att, b, t):
    nq = t // BLOCK
    nb = kcb.shape[2]
    assert nb <= HEAD_DIM
    wl = min(WINDOW + BLOCK, t)
    cb_spec = pl.BlockSpec((1, NSA_KV, nb, HEAD_DIM), lambda bi, i: (bi, 0, 0, 0))
    return pl.pallas_call(
        functools.partial(_nsa_prompt_kernel, n_sel=min(TOPN, nb), wl=wl, n_cls=math.gcd(nb, 8)),
        out_shape=jax.ShapeDtypeStruct((b * t, NSA_Q_W), BF16),
        grid=(b, nq),
        in_specs=[pl.BlockSpec((BLOCK, NSA_Q_W), lambda bi, i: (bi * nq + i, 0)),
                  pl.BlockSpec((BLOCK, SMALL_W), lambda bi, i: (bi * nq + i, 0)),
                  cb_spec, cb_spec,
                  pl.BlockSpec((1, NSA_KV, t, ATT_W), lambda bi, i: (bi, 0, 0, 0))],
        out_specs=pl.BlockSpec((BLOCK, NSA_Q_W), lambda bi, i: (bi * nq + i, 0)),
        scratch_shapes=[pltpu.VMEM((NSA_KV, NSA_HPG * BLOCK, HEAD_DIM), F32)],
        compiler_params=_cparams(("arbitrary", "arbitrary")),
        name="nsa_prompt",
    )(q_rot, small, kcb, vcb, att)


GDN_SOLVE_BLOCKS = 4


def _softplus(x):
    return jnp.maximum(x, 0.0) + jnp.log(1.0 + jnp.exp(-jnp.abs(x)))


def _unit_lower_solve(l_strict, rhs):
    n = l_strict[0].shape[0]
    sub = 8
    nt = n // sub
    tiles = [[r[t * sub:(t + 1) * sub, :] for t in range(nt)] for r in rhs]
    for i in range(n - 1):
        t0, s = divmod(i, sub)
        for p, lm in enumerate(l_strict):
            xi = tiles[p][t0][s:s + 1, :]
            col = lm[:, i:i + 1]
            for t in range(t0, nt):
                tiles[p][t] = tiles[p][t] - col[t * sub:(t + 1) * sub, :] * xi
    return [jnp.concatenate(tp, axis=0) for tp in tiles]


def _gdn_prompt_kernel(x_ref, z_ref, sm_ref, abr_ref, cw_ref, prow_ref, pcol_ref, nw_ref,
                       o_ref, s_out_ref, conv_out_ref, s_ref, xe_ref):
    n = pl.program_id(1)
    c = GDN_CHUNK
    pad = 8

    @pl.when(n == 0)
    def _():
        s_ref[...] = jnp.zeros_like(s_ref)
        xe_ref[0:pad, :] = jnp.zeros((pad, CONV_CH), F32)

    xe_ref[pad:pad + c, :] = x_ref[...]
    y = xe_ref[pl.ds(pad, c), :] * cw_ref[CONV_W - 1:CONV_W, :]
    for w in range(CONV_W - 1):
        y = y + xe_ref[pl.ds(pad - (CONV_W - 1) + w, c), :] * cw_ref[w:w + 1, :]
    y = _silu(y)

    ri = lax.broadcasted_iota(jnp.int32, (c, c), 0)
    ci = lax.broadcasted_iota(jnp.int32, (c, c), 1)
    tri = ri >= ci
    strict = ri > ci
    tri_f = jnp.where(tri, 1.0, 0.0)

    a_col = sm_ref[:, A_COL:A_COL + GDN_HEADS]
    b_col = sm_ref[:, B_COL:B_COL + GDN_HEADS]
    g_col = -jnp.exp(prow_ref[0:1, :]) * _softplus(a_col + prow_ref[1:2, :])
    g_row = -jnp.exp(pcol_ref[:, 0:1]) * _softplus(abr_ref[0, 0, 0:GDN_HEADS, :] + pcol_ref[:, 1:2])
    gc_col = _dot_exact(tri_f, g_col)
    gc_row = _dot_nt_exact(g_row, tri_f)
    beta_col = _sigmoid(b_col)

    heads = range(GDN_HEADS)
    yq = [y[:, h * HEAD_DIM:(h + 1) * HEAD_DIM] for h in heads]
    yk = [y[:, GDN_W + h * HEAD_DIM:GDN_W + (h + 1) * HEAD_DIM] for h in heads]
    yv = [y[:, 2 * GDN_W + h * HEAD_DIM:2 * GDN_W + (h + 1) * HEAD_DIM] for h in heads]
    qn = [jnp.sum(a * a, axis=-1, keepdims=True) for a in yq]
    kn = [jnp.sum(a * a, axis=-1, keepdims=True) for a in yk]
    qh = [a * lax.rsqrt(n + EPS) * SCALE for a, n in zip(yq, qn)]
    kh = [a * lax.rsqrt(n + EPS) for a, n in zip(yk, kn)]
    gccs = [gc_col[:, h:h + 1] for h in heads]
    betas = [beta_col[:, h:h + 1] for h in heads]
    decays = [jnp.exp(jnp.where(tri, gccs[h] - gc_row[h:h + 1, :], NEG_INF)) for h in heads]
    kbs = [k * b for k, b in zip(kh, betas)]
    kks = [_dot_nt(kb, k) for kb, k in zip(kbs, kh)]
    qks = [_dot_nt(q, k) for q, k in zip(qh, kh)]
    lmats = [jnp.where(strict, kk * d, 0.0) for kk, d in zip(kks, decays)]
    qkds = [qk * d for qk, d in zip(qks, decays)]
    rhss = [jnp.concatenate([yv[h] * betas[h], kbs[h] * jnp.exp(gccs[h])], axis=1) for h in heads]
    pre = [(qh[h], kh[h], gccs[h], qkds[h], lmats[h], rhss[h]) for h in heads]

    nblk = GDN_SOLVE_BLOCKS
    rb = c // nblk
    l_list, r_list = [], []
    for p in pre:
        lmat, rhs = p[4], p[5]
        for b in range(nblk):
            r0 = b * rb
            l_list.append(lmat[r0:r0 + rb, r0:r0 + rb])
            if b == 0:
                r_list.append(rhs[0:rb, :])
            else:
                r_list.append(jnp.concatenate([rhs[r0:r0 + rb, :], lmat[r0:r0 + rb, 0:r0],
                                               jnp.zeros((rb, HEAD_DIM - r0), F32)], axis=1))
    part = _unit_lower_solve(l_list, r_list)
    xs = [[part[h * nblk][:, 0:2 * HEAD_DIM]] for h in range(GDN_HEADS)]
    for b in range(1, nblk):
        for h in range(GDN_HEADS):
            y = part[h * nblk + b]
            xs[h].append(y[:, 0:2 * HEAD_DIM]
                         - _dot_exact(y[:, 2 * HEAD_DIM:2 * HEAD_DIM + b * rb], jnp.concatenate(xs[h], axis=0)))
    sols = [jnp.concatenate(x, axis=0) for x in xs]

    tn = (((0,), (0,)), ((), ()))
    s_in = [s_ref[h] for h in heads]
    ws = [_dot(sols[h][:, HEAD_DIM:], s_in[h]) for h in heads]
    v_new = [sols[h][:, :HEAD_DIM] - ws[h] for h in heads]
    qs_ = [_dot(qh[h] * jnp.exp(gccs[h]), s_in[h]) for h in heads]
    os_ = [qs_[h] + _dot(qkds[h], v_new[h]) for h in heads]
    g_last = [g[c - 1:c, :] for g in gccs]
    k_dec = [kh[h] * jnp.exp(g_last[h] - gccs[h]) for h in heads]
    s_out = [s_in[h] * jnp.exp(g_last[h])
             + lax.dot_general(k_dec[h].astype(BF16), v_new[h].astype(BF16), tn, preferred_element_type=F32)
             for h in heads]
    ms_ = [jnp.mean(o * o, axis=-1, keepdims=True) for o in os_]
    o_out = [(os_[h] * lax.rsqrt(ms_[h] + EPS) * nw_ref[...]
              * _silu(z_ref[:, h * HEAD_DIM:(h + 1) * HEAD_DIM])).astype(BF16) for h in heads]
    for h in range(GDN_HEADS):
        s_ref[h] = s_out[h]
        o_ref[:, h * HEAD_DIM:(h + 1) * HEAD_DIM] = o_out[h]
    xe_ref[0:pad, :] = xe_ref[c:c + pad, :]

    @pl.when(n == pl.num_programs(1) - 1)
    def _():
        s_out_ref[0] = s_ref[...]
        conv_out_ref[0] = xe_ref[pad - (CONV_W - 1):pad, :]


def _gdn_prompt(proj, small, conv_w, a_log, dt_bias, norm_w, b, t):
    c = GDN_CHUNK
    nc = t // c
    ab_row = small.reshape(b, nc, c, SMALL_W)[..., A_COL:A_COL + 2 * GDN_HEADS].transpose(0, 1, 3, 2)
    prow = jnp.stack([a_log, dt_bias], axis=0)
    pcol = prow.T
    return pl.pallas_call(
        _gdn_prompt_kernel,
        out_shape=(jax.ShapeDtypeStruct((b * t, GDN_W), BF16),
                   jax.ShapeDtypeStruct((b, GDN_HEADS, HEAD_DIM, HEAD_DIM), F32),
                   jax.ShapeDtypeStruct((b, CONV_W - 1, CONV_CH), F32)),
        grid=(b, nc),
        in_specs=[pl.BlockSpec((c, CONV_CH), lambda bi, n: (bi * nc + n, OFF_CONV // CONV_CH)),
                  pl.BlockSpec((c, GDN_W), lambda bi, n: (bi * nc + n, OFF_Z // GDN_W)),
                  pl.BlockSpec((c, HEAD_DIM), lambda bi, n: (bi * nc + n, 0)),
                  pl.BlockSpec((1, 1, 2 * GDN_HEADS, c), lambda bi, n: (bi, n, 0, 0)),
                  pl.BlockSpec((CONV_W, CONV_CH), lambda bi, n: (0, 0)),
                  pl.BlockSpec((2, GDN_HEADS), lambda bi, n: (0, 0)),
                  pl.BlockSpec((GDN_HEADS, 2), lambda bi, n: (0, 0)),
                  pl.BlockSpec((1, HEAD_DIM), lambda bi, n: (0, 0))],
        out_specs=(pl.BlockSpec((c, GDN_W), lambda bi, n: (bi * nc + n, 0)),
                   pl.BlockSpec((1, GDN_HEADS, HEAD_DIM, HEAD_DIM), lambda bi, n: (bi, 0, 0, 0)),
                   pl.BlockSpec((1, CONV_W - 1, CONV_CH), lambda bi, n: (bi, 0, 0))),
        scratch_shapes=[pltpu.VMEM((GDN_HEADS, HEAD_DIM, HEAD_DIM), F32),
                        pltpu.VMEM((c + 16, CONV_CH), F32)],
        compiler_params=_cparams(("arbitrary", "arbitrary")),
        name="gdn_prompt",
    )(proj, proj, small, ab_row, conv_w, prow, pcol, norm_w.reshape(1, HEAD_DIM))


def _outproj_kernel(on_ref, og_ref, w_ref, x_ref, ga_ref, g_ref, o_ref):
    o = jnp.concatenate([on_ref[...], og_ref[...]], axis=1)
    mix = jnp.dot(o, w_ref[0], preferred_element_type=F32)
    o_ref[...] = x_ref[...] + ga_ref[0] * _rms(mix, g_ref[...])


def _out_proj(o_nsa, o_gdn, w_out, x, ga, g, l, rows_per_mod, tm):
    m, d = x.shape
    kw = o_nsa.shape[1]
    return pl.pallas_call(
        _outproj_kernel,
        out_shape=jax.ShapeDtypeStruct((m, d), F32),
        grid=(m // tm,),
        in_specs=[pl.BlockSpec((tm, kw), lambda i: (i, 0)),
                  pl.BlockSpec((tm, kw), lambda i: (i, 0)),
                  pl.BlockSpec((1, 2 * kw, d), lambda i: (l, 0, 0)),
                  pl.BlockSpec((tm, d), lambda i: (i, 0)),
                  _mod_spec(ga, rows_per_mod, tm),
                  pl.BlockSpec((1, d), lambda i: (0, 0))],
        out_specs=pl.BlockSpec((tm, d), lambda i: (i, 0)),
        compiler_params=_cparams(("arbitrary",)),
        name="out_proj",
    )(o_nsa, o_gdn, w_out, x, ga, g)


def _ffn_kernel(x_ref, g1_ref, sc_ref, sh_ref, wg_ref, wu_ref, wd_ref, ga_ref, g2_ref, o_ref, h_ref, acc_ref):
    j = pl.program_id(1)

    @pl.when(j == 0)
    def _():
        h = _rms(x_ref[...], g1_ref[...]) * (1.0 + sc_ref[0]) + sh_ref[0]
        h_ref[...] = h.astype(BF16)
        acc_ref[...] = jnp.zeros_like(acc_ref)

    h = h_ref[...]
    a = jnp.dot(h, wg_ref[0], preferred_element_type=F32)
    u = jnp.dot(h, wu_ref[0], preferred_element_type=F32)
    acc_ref[...] += _dot(_silu(a) * u, wd_ref[0])

    @pl.when(j == pl.num_programs(1) - 1)
    def _():
        o_ref[...] = x_ref[...] + ga_ref[0] * _rms(acc_ref[...], g2_ref[...])


def _ffn(x, g1, sc, sh, w_gate, w_up, w_down, ga, g2, l, rows_per_mod, tm, tf=512):
    m, d = x.shape
    f = w_gate.shape[2]
    vec = pl.BlockSpec((1, d), lambda i, j: (0, 0))
    return pl.pallas_call(
        _ffn_kernel,
        out_shape=jax.ShapeDtypeStruct((m, d), F32),
        grid=(m // tm, f // tf),
        in_specs=[pl.BlockSpec((tm, d), lambda i, j: (i, 0)), vec,
                  _mod_spec(sc, rows_per_mod, tm), _mod_spec(sh, rows_per_mod, tm),
                  pl.BlockSpec((1, d, tf), lambda i, j: (l, 0, j)),
                  pl.BlockSpec((1, d, tf), lambda i, j: (l, 0, j)),
                  pl.BlockSpec((1, tf, d), lambda i, j: (l, j, 0)),
                  _mod_spec(ga, rows_per_mod, tm), vec],
        out_specs=pl.BlockSpec((tm, d), lambda i, j: (i, 0)),
        scratch_shapes=[pltpu.VMEM((tm, d), BF16), pltpu.VMEM((tm, d), F32)],
        compiler_params=_cparams(("arbitrary", "arbitrary")),
        name="ffn",
    )(x, g1, sc, sh, w_gate, w_up, w_down, ga, g2)


def _stack_heads(q, h0):
    rows = [q[:, (h0 + h) * HEAD_DIM:(h0 + h + 1) * HEAD_DIM] for h in range(NSA_HPG)]
    return jnp.concatenate(rows + [jnp.zeros((NSA_HPG, HEAD_DIM), F32)], axis=0)


def _nsa_sample_cmp_kernel(q_ref, kcb_ref, vcb_ref, oc_ref, idx_ref, *, n_sel, q_pos):
    nseq, nbp = kcb_ref.shape[0], kcb_ref.shape[2]
    cur = q_pos // BLOCK
    imps = []
    for bi in range(nseq):
        q = q_ref[bi].astype(F32)
        for g in range(NSA_KV):
            qs = _stack_heads(q, g * NSA_HPG)
            jc = lax.broadcasted_iota(jnp.int32, (qs.shape[0], nbp), 1)
            cmask = (jc * BLOCK + (BLOCK - 1)) <= q_pos
            p_c = _masked_softmax(_dot_nt(qs, kcb_ref[bi, g]) * SCALE, cmask)
            o_c = _dot(p_c, vcb_ref[bi, g])
            for h in range(NSA_HPG):
                c0 = (g * NSA_HPG + h) * HEAD_DIM
                oc_ref[bi, :, c0:c0 + HEAD_DIM] = o_c[h:h + 1]
            imp = p_c[0:1]
            for h in range(1, NSA_HPG):
                imp = imp + p_c[h:h + 1]
            imps.append(imp)
    imp = jnp.concatenate(imps, axis=0)
    jb = lax.broadcasted_iota(jnp.int32, imp.shape, 1)
    forced = (jb == 0) | (jb == cur) | (jb == cur - 1)
    score = jnp.where(forced, jnp.inf, jnp.where(jb > cur, NEG_INF, imp))
    left = jb >= 0
    for r in range(n_sel):
        best = jnp.max(jnp.where(left, score, NEG_INF), axis=1, keepdims=True)
        pick = jnp.min(jnp.where(left & (score == best), jb, nbp), axis=1, keepdims=True)
        idx_ref[r] = jnp.broadcast_to(pick, (imp.shape[0], HEAD_DIM))
        left = left & (jb != pick)


def _nsa_sample_cmp(q_rot, kcb, vcb, q_pos, n_blocks):
    b = q_rot.shape[0]
    nbp = kcb.shape[2]
    n_sel = min(TOPN, n_blocks)
    o_c, idx = pl.pallas_call(
        functools.partial(_nsa_sample_cmp_kernel, n_sel=n_sel, q_pos=q_pos),
        out_shape=(jax.ShapeDtypeStruct((b, 1, NSA_Q_W), F32),
                   jax.ShapeDtypeStruct((n_sel, b * NSA_KV, HEAD_DIM), jnp.int32)),
        compiler_params=pltpu.CompilerParams(vmem_limit_bytes=VMEM_LIMIT),
        name="nsa_sample_cmp",
    )(q_rot.reshape(b, 1, NSA_Q_W), kcb, vcb)
    return o_c, idx[:, :, 0].T.reshape(b, NSA_KV, n_sel)


def _nsa_sample_attn_kernel(idx_ref, pt_ref, q_ref, sm_ref, oc_ref, cache_ref, newsel_ref, win_ref, newwin_ref,
                            o_ref, wino_ref, blk_buf, sem, *, l, npb, n_sel):
    bi, g = pl.program_id(0), pl.program_id(1)
    bpp = cache_ref.shape[3]
    copies, picks = [], []
    for n in range(n_sel):
        pick = idx_ref[bi, g, n]
        jp = jnp.minimum(pick, npb - 1)
        cp = pltpu.make_async_copy(cache_ref.at[pt_ref[bi, jp // bpp], l, g, jp % bpp],
                                   blk_buf.at[pl.ds(n * BLK_ROWS, BLK_ROWS), :], sem)
        cp.start()
        copies.append(cp)
        picks.append(pick)

    qs = _stack_heads(q_ref[0].astype(F32), 0)

    def row_scores(k):
        return jnp.sum(qs * k.astype(BF16).astype(F32), axis=-1, keepdims=True) * SCALE

    wb = win_ref.shape[3] // 2
    s_buf = _dot_nt(qs, win_ref[0, 0, 0, pl.ds(0, wb, stride=2), :]) * SCALE
    s_new = row_scores(newwin_ref[0, 0, 0:1, :])
    m = jnp.maximum(jnp.max(s_buf, axis=-1, keepdims=True), s_new)
    e_buf, e_new = jnp.exp(s_buf - m), jnp.exp(s_new - m)
    den = jnp.maximum(jnp.sum(e_buf, axis=-1, keepdims=True) + e_new, 1e-30)
    o_w = (_dot(e_buf, win_ref[0, 0, 0, pl.ds(1, wb, stride=2), :])
           + e_new * newwin_ref[0, 0, 1:2, :].astype(BF16).astype(F32)) / den
    wino_ref[0, 0, 0:2 * wb - 2, :] = win_ref[0, 0, 0, 2:2 * wb, :]
    wino_ref[0, 0, 2 * wb - 2:2 * wb, :] = newwin_ref[0, 0]

    for cp in copies:
        cp.wait()
    nk = n_sel * BLOCK
    slot = lax.broadcasted_iota(jnp.int32, (1, nk), 1) // BLOCK
    bias = jnp.zeros((1, nk), F32)
    has_new = picks[0] >= npb
    for n in range(n_sel):
        bias = jnp.where((slot == n) & (picks[n] >= npb), NEG_INF, bias)
        has_new = has_new | (picks[n] >= npb)
    s_blk = _dot_nt(qs, blk_buf[pl.ds(0, nk, stride=2), :]) * SCALE + bias
    s_row = jnp.where(has_new, row_scores(newsel_ref[0, 0, 0:1, :]), NEG_INF)
    m = jnp.maximum(jnp.max(s_blk, axis=-1, keepdims=True), s_row)
    m = jnp.where(m == NEG_INF, 0.0, m)
    e_blk, e_row = jnp.exp(s_blk - m), jnp.exp(s_row - m)
    den = jnp.maximum(jnp.sum(e_blk, axis=-1, keepdims=True) + e_row, 1e-30)
    o_s = (_dot(e_blk, blk_buf[pl.ds(1, nk, stride=2), :])
           + e_row * newsel_ref[0, 0, 1:2, :].astype(BF16).astype(F32)) / den

    gt = _sigmoid(sm_ref[0])
    for h in range(NSA_HPG):
        c0 = h * HEAD_DIM
        o = (gt[:, 3 * h:3 * h + 1] * oc_ref[0, :, c0:c0 + HEAD_DIM] + gt[:, 3 * h + 1:3 * h + 2] * o_s[h:h + 1]
             + gt[:, 3 * h + 2:3 * h + 3] * o_w[h:h + 1])
        o_ref[0, :, c0:c0 + HEAD_DIM] = o.astype(BF16)


def _nsa_sample_attn(idx, page_table, q_rot, small, o_c, cache_sel, new_sel, cache_win, new_win, l, npb):
    b = q_rot.shape[0]
    n_sel = idx.shape[2]
    bpp = cache_sel.shape[3]
    wb2 = cache_win.shape[3]
    gw = NSA_HPG * HEAD_DIM

    new_spec = pl.BlockSpec((1, 1, 2, HEAD_DIM), lambda bi, g, *_: (bi, g, 0, 0))
    grid_spec = pltpu.PrefetchScalarGridSpec(
        num_scalar_prefetch=2,
        grid=(b, NSA_KV),
        in_specs=[pl.BlockSpec((1, 1, gw), lambda bi, g, *_: (bi, 0, g)),
                  pl.BlockSpec((1, 1, HEAD_DIM), lambda bi, g, *_: (bi, 0, g)),
                  pl.BlockSpec((1, 1, gw), lambda bi, g, *_: (bi, 0, g)),
                  pl.BlockSpec(memory_space=pl.ANY),
                  new_spec,
                  pl.BlockSpec((1, 1, 1, wb2, HEAD_DIM), lambda bi, g, *_: (bi, l, g, 0, 0)),
                  new_spec],
        out_specs=(pl.BlockSpec((1, 1, gw), lambda bi, g, *_: (bi, 0, g)),
                   pl.BlockSpec((1, 1, wb2, HEAD_DIM), lambda bi, g, *_: (bi, g, 0, 0))),
        scratch_shapes=[pltpu.VMEM((n_sel * BLK_ROWS, HEAD_DIM), F32), pltpu.SemaphoreType.DMA(())],
    )
    return pl.pallas_call(
        functools.partial(_nsa_sample_attn_kernel, l=l, npb=npb, n_sel=n_sel),
        out_shape=(jax.ShapeDtypeStruct((b, 1, NSA_Q_W), BF16),
                   jax.ShapeDtypeStruct((b, NSA_KV, wb2, HEAD_DIM), F32)),
        grid_spec=grid_spec,
        compiler_params=_cparams(("arbitrary", "arbitrary")),
        name="nsa_sample_attn",
    )(idx, page_table, q_rot.reshape(b, 1, NSA_Q_W), small.reshape(b, 1, SMALL_W), o_c,
      cache_sel, new_sel, cache_win, new_win)


def _gdn_sample_kernel(x_ref, z_ref, sm_ref, buf_ref, s0_ref, cw_ref, prow_ref, nw_ref, eye_ref,
                       o_ref, s_out_ref, conv_out_ref):
    x = x_ref[0]
    buf = buf_ref[0, 0]
    y = x * cw_ref[CONV_W - 1:CONV_W, :]
    for w in range(CONV_W - 1):
        y = y + buf[w:w + 1, :] * cw_ref[w:w + 1, :]
    y = _silu(y)
    conv_out_ref[0, 0:CONV_W - 2, :] = buf[1:CONV_W - 1, :]
    conv_out_ref[0, CONV_W - 2:CONV_W - 1, :] = x

    def heads(off):
        return jnp.concatenate([y[:, off + h * HEAD_DIM:off + (h + 1) * HEAD_DIM] for h in range(GDN_HEADS)], axis=0)

    yq, yk, v = heads(0), heads(GDN_W), heads(2 * GDN_W)
    q = yq * lax.rsqrt(jnp.sum(yq * yq, axis=-1, keepdims=True) + EPS) * SCALE
    k = yk * lax.rsqrt(jnp.sum(yk * yk, axis=-1, keepdims=True) + EPS)
    sm = sm_ref[0]
    g_row = -jnp.exp(prow_ref[0:1, :]) * _softplus(sm[:, A_COL:A_COL + GDN_HEADS] + prow_ref[1:2, :])
    beta_row = _sigmoid(sm[:, B_COL:B_COL + GDN_HEADS])
    eg_row = jnp.exp(g_row)
    k_t = _dot_nt_exact(eye_ref[...], k)
    q_t = _dot_nt_exact(eye_ref[...], q)
    z = z_ref[0]
    for h in range(GDN_HEADS):
        s0 = s0_ref[0, 0, h]
        eg = eg_row[:, h:h + 1]
        kc = k_t[:, h:h + 1].astype(BF16).astype(F32)
        qc = q_t[:, h:h + 1].astype(BF16).astype(F32)
        s0b = s0.astype(BF16).astype(F32)
        ks = jnp.sum(kc * beta_row[:, h:h + 1] * eg * s0b, axis=0, keepdims=True)
        v_new = v[h:h + 1] * beta_row[:, h:h + 1] - ks
        vb = v_new.astype(BF16).astype(F32)
        s1 = s0 * eg + kc * vb
        qe = (q_t[:, h:h + 1] * eg).astype(BF16).astype(F32)
        o = jnp.sum(qe * s0b, axis=0, keepdims=True) + jnp.sum(qc * kc, axis=0, keepdims=True) * vb
        s_out_ref[0, h] = s1
        sl = slice(h * HEAD_DIM, (h + 1) * HEAD_DIM)
        o = _rms(o, nw_ref[...]) * _silu(z[:, sl])
        o_ref[0, :, sl] = o.astype(BF16)


def _gdn_sample(proj, small, state_conv, state_gdn, conv_w, a_log, dt_bias, norm_w, l, b):
    prow = jnp.stack([a_log, dt_bias], axis=0)
    rows = proj.shape[0]
    return pl.pallas_call(
        _gdn_sample_kernel,
        out_shape=(jax.ShapeDtypeStruct((b, 1, GDN_W), BF16),
                   jax.ShapeDtypeStruct((b, GDN_HEADS, HEAD_DIM, HEAD_DIM), F32),
                   jax.ShapeDtypeStruct((b, CONV_W - 1, CONV_CH), F32)),
        grid=(b,),
        in_specs=[pl.BlockSpec((1, 1, CONV_CH), lambda bi: (bi, 0, OFF_CONV // CONV_CH)),
                  pl.BlockSpec((1, 1, GDN_W), lambda bi: (bi, 0, OFF_Z // GDN_W)),
                  pl.BlockSpec((1, 1, SMALL_W), lambda bi: (bi, 0, 0)),
                  pl.BlockSpec((1, 1, CONV_W - 1, CONV_CH), lambda bi: (bi, l, 0, 0)),
                  pl.BlockSpec((1, 1, GDN_HEADS, HEAD_DIM, HEAD_DIM), lambda bi: (bi, l, 0, 0, 0)),
                  pl.BlockSpec((CONV_W, CONV_CH), lambda bi: (0, 0)),
                  pl.BlockSpec((2, GDN_HEADS), lambda bi: (0, 0)),
                  pl.BlockSpec((1, HEAD_DIM), lambda bi: (0, 0)),
                  pl.BlockSpec((HEAD_DIM, HEAD_DIM), lambda bi: (0, 0))],
        out_specs=(pl.BlockSpec((1, 1, GDN_W), lambda bi: (bi, 0, 0)),
                   pl.BlockSpec((1, GDN_HEADS, HEAD_DIM, HEAD_DIM), lambda bi: (bi, 0, 0, 0)),
                   pl.BlockSpec((1, CONV_W - 1, CONV_CH), lambda bi: (bi, 0, 0))),
        compiler_params=_cparams(("arbitrary",)),
        name="gdn_sample",
    )(proj.reshape(rows, 1, MAIN_W), proj.reshape(rows, 1, MAIN_W), small.reshape(rows, 1, SMALL_W),
      state_conv, state_gdn, conv_w, prow, norm_w.reshape(1, HEAD_DIM), jnp.eye(HEAD_DIM, dtype=F32))


def _repack_w_in(w_in):
    kvw = NSA_KV * HEAD_DIM
    o_q = 0
    o_kv = [NSA_Q_W + i * kvw for i in range(6)]
    o_gl = NSA_Q_W + 6 * kvw
    o_conv = o_gl + 3 * NSA_HEADS
    o_a = o_conv + CONV_CH
    o_b = o_a + GDN_HEADS
    o_z = o_b + GDN_HEADS

    def cols(a, n):
        return w_in[:, :, a:a + n]

    pieces = [cols(o_conv, CONV_CH), cols(o_z, GDN_W), cols(o_q, NSA_Q_W)]
    for pair in range(3):
        for g in range(NSA_KV):
            pieces.append(cols(o_kv[2 * pair] + g * HEAD_DIM, HEAD_DIM))
            pieces.append(cols(o_kv[2 * pair + 1] + g * HEAD_DIM, HEAD_DIM))
    main = jnp.concatenate(pieces, axis=-1).astype(BF16)
    depth, d, _ = w_in.shape
    ngl = 3 * NSA_HPG
    small = jnp.concatenate([
        cols(o_gl, ngl), cols(o_a, GDN_HEADS), cols(o_b, GDN_HEADS),
        jnp.zeros((depth, d, HEAD_DIM - ngl - 2 * GDN_HEADS), w_in.dtype),
        cols(o_gl + ngl, ngl), jnp.zeros((depth, d, HEAD_DIM - ngl), w_in.dtype)], axis=-1).astype(BF16)
    return main, small


def _mods(mod_l, lo, hi, broadcast):
    d = mod_l.shape[1] // 6
    out = []
    for k in range(6):
        m = mod_l[lo:hi, k * d:(k + 1) * d]
        out.append(m[:, None, :] if broadcast else m[None])
    return out


def kernel(x_prompt, x_sample, cache_cmp_kv, cache_sel_kv, cache_win_kv, state_gdn, state_conv, page_table,
           c_prompt, c_sample, w_ada, b_ada, g_pre_mix, w_in, cmp_pe, cmp_w1, cmp_w2, conv_w, gdn_a_log,
           gdn_dt_bias, gdn_norm, w_out, g_post_mix, g_pre_ffn, w_gate, w_up, w_down, g_post_ffn):
    bp, t, d = x_prompt.shape
    bs = x_sample.shape[0]
    depth = w_in.shape[0]
    n_pool, _, _, page, _, _ = cache_cmp_kv.shape
    past = page_table.shape[1] * page
    npb = past // BLOCK
    bpp = page // BLOCK
    wb = cache_win_kv.shape[3]
    ts = 16
    assert x_sample.shape[1] == 1 and bs <= ts and t % BLOCK == 0 and page % BLOCK == 0

    c_all = jnp.concatenate([c_sample, jnp.zeros((ts - bs, d), F32), c_prompt,
                             jnp.zeros((-bp % 8, d), F32)], axis=0)
    mod = _ada_mod(c_all, w_ada, b_ada)

    w_main, w_small = _repack_w_in(w_in)
    w_out_b, w_gate_b, w_up_b, w_down_b = (w.astype(BF16) for w in (w_out, w_gate, w_up, w_down))
    rope_p = _rope_tables(jnp.arange(t))
    rope_s = _rope_tables(jnp.full((ts,), past))

    tm = min(512, t)
    xp = x_prompt.reshape(bp * t, d)
    xs = jnp.concatenate([x_sample.reshape(bs, d), jnp.zeros((ts - bs, d), F32)], axis=0)
    cache_cmp_blocks = cache_cmp_kv.reshape(n_pool, depth, NSA_KV * bpp, BLK_ROWS, HEAD_DIM)
    cache_sel_blocks = cache_sel_kv.reshape(n_pool, depth, NSA_KV, bpp, BLK_ROWS, HEAD_DIM)
    cache_win_rows = cache_win_kv.reshape(bs, depth, NSA_KV, 2 * wb, HEAD_DIM)
    nbp = -(-(npb + 1) // HEAD_DIM) * HEAD_DIM

    p_states, s_states = [], []
    for l in range(depth):
        gpm, gpo, gpf, gpof = (g[l][None] for g in (g_pre_mix, g_post_mix, g_pre_ffn, g_post_ffn))
        cw = _compress_weights(cmp_pe, cmp_w1, cmp_w2, l)

        sh1, sc1, ga1, sh2, sc2, ga2 = _mods(mod[l], ts, ts + bp, True)
        proj, small = _in_proj(xp, gpm, sc1, sh1, w_main, w_small, l, t, tm)
        q_rot, cmp_st, sel_st, win_st, att = _post_proj(proj, rope_p, bp, t, min(256, t))
        nb = t // BLOCK
        kcb, vcb = _compress(cmp_st.reshape(bp * NSA_KV * 2 * t, HEAD_DIM), cw, math.gcd(bp * NSA_KV * nb, 128))
        kcb, vcb = (a.reshape(bp, NSA_KV, nb, HEAD_DIM) for a in (kcb, vcb))
        o_nsa = _nsa_prompt(q_rot, small, kcb, vcb, att, bp, t)
        o_gdn, s_fin, conv_fin = _gdn_prompt(proj, small, conv_w[l], gdn_a_log[l], gdn_dt_bias[l], gdn_norm[l], bp, t)
        xp = _out_proj(o_nsa, o_gdn, w_out_b, xp, ga1, gpo, l, t, tm)
        xp = _ffn(xp, gpf, sc2, sh2, w_gate_b, w_up_b, w_down_b, ga2, gpof, l, t, tm)
        wl = min(WINDOW, t)
        p_states.append((cmp_st.reshape(bp, NSA_KV, t, 2, HEAD_DIM), sel_st.reshape(bp, NSA_KV, t, 2, HEAD_DIM),
                         win_st[:, :, 2 * (t - wl):].reshape(bp, NSA_KV, wl, 2, HEAD_DIM), s_fin, conv_fin))

        sh1, sc1, ga1, sh2, sc2, ga2 = _mods(mod[l], 0, ts, False)
        proj, small = _in_proj(xs, gpm, sc1, sh1, w_main, w_small, l, ts, ts)
        q_rot, cmp_new, sel_new, win_new, _ = _post_proj(proj, rope_s, 1, ts, ts)
        cmp_new, sel_new, win_new = (a.reshape(NSA_KV, ts, 2, HEAD_DIM)[:, :bs].transpose(1, 0, 2, 3)
                                     for a in (cmp_new, sel_new, win_new))
        kc_past, vc_past = _compress_paged(cache_cmp_blocks, page_table, cw, l)
        new_blk = jnp.pad(cmp_new.reshape(bs * NSA_KV, 2, HEAD_DIM), ((0, 0), (0, BLK_ROWS - 2), (0, 0)))
        kc_new, vc_new = _compress(new_blk.reshape(bs * NSA_KV * BLK_ROWS, HEAD_DIM), cw, bs * NSA_KV)

        def summaries(past_rows, new_rows):
            a = past_rows.reshape(bs, past // page, NSA_KV, bpp, HEAD_DIM).transpose(0, 2, 1, 3, 4)
            a = a.reshape(bs, NSA_KV, npb, HEAD_DIM)
            a = jnp.concatenate([a, new_rows.reshape(bs, NSA_KV, 1, HEAD_DIM)], axis=2)
            return jnp.pad(a, ((0, 0), (0, 0), (0, nbp - npb - 1), (0, 0)))

        kcb, vcb = summaries(kc_past, kc_new), summaries(vc_past, vc_new)
        o_c, idx = _nsa_sample_cmp(q_rot[:bs], kcb, vcb, past, npb + 1)
        o_nsa, win_out = _nsa_sample_attn(idx, page_table, q_rot[:bs], small[:bs], o_c, cache_sel_blocks,
                                          sel_new, cache_win_rows, win_new, l, npb)
        o_gdn, s_fin, conv_fin = _gdn_sample(proj, small, state_conv, state_gdn, conv_w[l], gdn_a_log[l],
                                             gdn_dt_bias[l], gdn_norm[l], l, bs)
        pad_rows = ((0, ts - bs), (0, 0))
        o_nsa = jnp.pad(o_nsa.reshape(bs, NSA_Q_W), pad_rows)
        o_gdn = jnp.pad(o_gdn.reshape(bs, GDN_W), pad_rows)
        xs = _out_proj(o_nsa, o_gdn, w_out_b, xs, ga1, gpo, l, ts, ts)
        xs = _ffn(xs, gpf, sc2, sh2, w_gate_b, w_up_b, w_down_b, ga2, gpof, l, ts, ts)
        s_states.append((cmp_new.reshape(bs, NSA_KV, 1, 2, HEAD_DIM), sel_new.reshape(bs, NSA_KV, 1, 2, HEAD_DIM),
                         win_out.reshape(bs, NSA_KV, wb, 2, HEAD_DIM), s_fin, conv_fin))

    p_st = [jnp.stack(s, axis=1) for s in zip(*p_states)]
    s_st = [jnp.stack(s, axis=1) for s in zip(*s_states)]
    return (xp.reshape(bp, t, d), xs[:bs].reshape(bs, 1, d), *p_st, *s_st)
```

```python
import functools
import math

import jax
import jax.numpy as jnp
from jax import lax
from jax.experimental import pallas as pl
from jax.experimental.pallas import tpu as pltpu

F32 = jnp.float32
BF16 = jnp.bfloat16

HEAD_DIM = 128
NSA_HEADS = 8
NSA_KV = 2
NSA_HPG = NSA_HEADS // NSA_KV
GDN_HEADS = 8
ROT_DIM = HEAD_DIM // 4
ROPE_THETA = 500000.0
BLOCK = 64
TOPN = 16
WINDOW = 512
CMP_HID = 256
CONV_W = 4
GDN_CHUNK = 64
EPS = 1e-6
KV_ROW = 2 * HEAD_DIM
BLK_ROWS = 2 * BLOCK
BLK_PITCH = BLK_ROWS + 8
NSA_Q_W = NSA_HEADS * HEAD_DIM
GDN_W = GDN_HEADS * HEAD_DIM
CONV_CH = 3 * GDN_W
SMALL_W = 2 * HEAD_DIM
A_COL = 12
B_COL = 20
OFF_CONV = 0
OFF_Z = CONV_CH
OFF_Q = OFF_Z + GDN_W
OFF_CMP = OFF_Q + NSA_Q_W
OFF_SEL = OFF_CMP + NSA_KV * KV_ROW
OFF_WIN = OFF_SEL + NSA_KV * KV_ROW
MAIN_W = OFF_WIN + NSA_KV * KV_ROW
VMEM_LIMIT = 48 * 1024 * 1024
NEG_INF = float("-inf")
SCALE = HEAD_DIM ** -0.5


def _cparams(sem):
    return pltpu.CompilerParams(dimension_semantics=sem, vmem_limit_bytes=VMEM_LIMIT)


def _dot(a, b):
    return jnp.dot(a.astype(BF16), b.astype(BF16), preferred_element_type=F32)


def _dot_nt(a, b):
    return lax.dot_general(a.astype(BF16), b.astype(BF16), (((1,), (1,)), ((), ())),
                           preferred_element_type=F32)


def _dot_exact(a, b):
    return jnp.dot(a, b, preferred_element_type=F32, precision=lax.Precision.HIGHEST)


def _dot_nt_exact(a, b):
    return lax.dot_general(a, b, (((1,), (1,)), ((), ())), preferred_element_type=F32,
                           precision=lax.Precision.HIGHEST)


def _sigmoid(x):
    return 1.0 / (1.0 + jnp.exp(-x))


def _silu(x):
    return x * _sigmoid(x)


def _rms(x, g):
    return x * lax.rsqrt(jnp.mean(x * x, axis=-1, keepdims=True) + EPS) * g


def _masked_softmax(s, mask):
    s = jnp.where(mask, s, NEG_INF)
    m = jnp.max(s, axis=-1, keepdims=True)
    m = jnp.where(m == NEG_INF, 0.0, m)
    e = jnp.where(mask, jnp.exp(s - m), 0.0)
    return e / jnp.maximum(jnp.sum(e, axis=-1, keepdims=True), 1e-30)


def _ada_kernel(c_ref, w_ref, b_ref, o_ref):
    o_ref[0] = _dot(_silu(c_ref[...]), w_ref[0]) + b_ref[0]


def _ada_mod(c_all, w_ada, b_ada, tn=1024):
    depth, d, n = w_ada.shape
    r = c_all.shape[0]
    return pl.pallas_call(
        _ada_kernel,
        out_shape=jax.ShapeDtypeStruct((depth, r, n), F32),
        grid=(depth, n // tn),
        in_specs=[pl.BlockSpec((r, d), lambda l, j: (0, 0)),
                  pl.BlockSpec((1, d, tn), lambda l, j: (l, 0, j)),
                  pl.BlockSpec((1, 1, tn), lambda l, j: (l, 0, j))],
        out_specs=pl.BlockSpec((1, r, tn), lambda l, j: (l, 0, j)),
        compiler_params=_cparams(("arbitrary", "arbitrary")),
        name="ada_mod",
    )(c_all, w_ada, b_ada.reshape(depth, 1, n))


def _inproj_kernel(x_ref, g_ref, sc_ref, sh_ref, w_ref, ws_ref, o_ref, os_ref, h_ref):
    @pl.when(pl.program_id(1) == 0)
    def _():
        h = _rms(x_ref[...], g_ref[...]) * (1.0 + sc_ref[0]) + sh_ref[0]
        hb = h.astype(BF16)
        h_ref[...] = hb
        os_ref[...] = jnp.dot(hb, ws_ref[0], preferred_element_type=F32)

    o_ref[...] = jnp.dot(h_ref[...], w_ref[0], preferred_element_type=F32)


def _mod_spec(mod, rows_per_mod, tm):
    d = mod.shape[-1]
    if mod.shape[1] == 1:
        return pl.BlockSpec((1, 1, d), lambda i, *_: ((i * tm) // rows_per_mod, 0, 0))
    return pl.BlockSpec((1, tm, d), lambda i, *_: (0, i, 0))


def _in_proj(x, g, sc, sh, w_main, w_small, l, rows_per_mod, tm, tn=MAIN_W // 4):
    m, d = x.shape
    return pl.pallas_call(
        _inproj_kernel,
        out_shape=(jax.ShapeDtypeStruct((m, MAIN_W), F32), jax.ShapeDtypeStruct((m, SMALL_W), F32)),
        grid=(m // tm, MAIN_W // tn),
        in_specs=[pl.BlockSpec((tm, d), lambda i, j: (i, 0)),
                  pl.BlockSpec((1, d), lambda i, j: (0, 0)),
                  _mod_spec(sc, rows_per_mod, tm), _mod_spec(sh, rows_per_mod, tm),
                  pl.BlockSpec((1, d, tn), lambda i, j: (l, 0, j)),
                  pl.BlockSpec((1, d, SMALL_W), lambda i, j: (l, 0, 0))],
        out_specs=(pl.BlockSpec((tm, tn), lambda i, j: (i, j)),
                   pl.BlockSpec((tm, SMALL_W), lambda i, j: (i, 0))),
        scratch_shapes=[pltpu.VMEM((tm, d), BF16)],
        compiler_params=_cparams(("arbitrary", "arbitrary")),
        name="in_proj",
    )(x, g, sc, sh, w_main, w_small)


def _rope(x, c, s1, s2):
    half = ROT_DIM // 2
    return x * c + pltpu.roll(x, HEAD_DIM - half, 1) * s1 + pltpu.roll(x, half, 1) * s2


ATT_W = 5 * HEAD_DIM


def _postproj_kernel(q_ref, cmp_ref, sel_ref, win_ref, c_ref, s1_ref, s2_ref,
                     qo_ref, cmpo_ref, selo_ref, wino_ref, att_ref):
    c, s1, s2 = c_ref[...], s1_ref[...], s2_ref[...]
    for h in range(NSA_HEADS):
        sl = slice(h * HEAD_DIM, (h + 1) * HEAD_DIM)
        qo_ref[:, sl] = _rope(q_ref[:, sl], c, s1, s2).astype(BF16)
    tt = c.shape[0]
    blk = (pl.program_id(1) * tt + lax.broadcasted_iota(jnp.int32, (tt, HEAD_DIM), 0)) // BLOCK
    onehot = jnp.where(lax.broadcasted_iota(jnp.int32, (tt, HEAD_DIM), 1) == blk, 1.0, 0.0).astype(BF16)
    for src, dst, col in ((cmp_ref, cmpo_ref, None), (sel_ref, selo_ref, 0), (win_ref, wino_ref, 3 * HEAD_DIM)):
        for g in range(NSA_KV):
            base = g * KV_ROW
            k = _rope(src[:, base:base + HEAD_DIM], c, s1, s2)
            v = src[:, base + HEAD_DIM:base + KV_ROW]
            dst[0, g, pl.ds(0, tt, stride=2), :] = k
            dst[0, g, pl.ds(1, tt, stride=2), :] = v
            if col == 0:
                att_ref[0, g, :, 0:HEAD_DIM] = k.astype(BF16)
                att_ref[0, g, :, HEAD_DIM:2 * HEAD_DIM] = onehot
                att_ref[0, g, :, 2 * HEAD_DIM:3 * HEAD_DIM] = v.astype(BF16)
            elif col is not None:
                att_ref[0, g, :, col:col + HEAD_DIM] = k.astype(BF16)
                att_ref[0, g, :, col + HEAD_DIM:col + 2 * HEAD_DIM] = v.astype(BF16)


def _post_proj(proj, rope_tabs, b, t, tt):
    nt = t // tt
    state = jax.ShapeDtypeStruct((b, NSA_KV, 2 * t, HEAD_DIM), F32)
    att = jax.ShapeDtypeStruct((b, NSA_KV, t, ATT_W), BF16)
    kvw = NSA_KV * KV_ROW
    tab_spec = pl.BlockSpec((tt, HEAD_DIM), lambda bi, ti: (ti, 0))
    st_spec = pl.BlockSpec((1, NSA_KV, 2 * tt, HEAD_DIM), lambda bi, ti: (bi, 0, ti, 0))
    att_spec = pl.BlockSpec((1, NSA_KV, tt, ATT_W), lambda bi, ti: (bi, 0, ti, 0))
    return pl.pallas_call(
        _postproj_kernel,
        out_shape=(jax.ShapeDtypeStruct((b * t, NSA_Q_W), BF16), state, state, state, att),
        grid=(b, nt),
        in_specs=[pl.BlockSpec((tt, NSA_Q_W), lambda bi, ti: (bi * nt + ti, OFF_Q // NSA_Q_W)),
                  pl.BlockSpec((tt, kvw), lambda bi, ti: (bi * nt + ti, OFF_CMP // kvw)),
                  pl.BlockSpec((tt, kvw), lambda bi, ti: (bi * nt + ti, OFF_SEL // kvw)),
                  pl.BlockSpec((tt, kvw), lambda bi, ti: (bi * nt + ti, OFF_WIN // kvw)),
                  tab_spec, tab_spec, tab_spec],
        out_specs=(pl.BlockSpec((tt, NSA_Q_W), lambda bi, ti: (bi * nt + ti, 0)),
                   st_spec, st_spec, st_spec, att_spec),
        compiler_params=_cparams(("arbitrary", "arbitrary")),
        name="post_proj",
    )(proj, proj, proj, proj, *rope_tabs)


def _rope_tables(pos):
    half = ROT_DIM // 2
    inv = ROPE_THETA ** (-jnp.arange(half, dtype=F32) * 2.0 / ROT_DIM)
    ang = pos.astype(F32)[:, None] * inv
    cos, sin = jnp.cos(ang), jnp.sin(ang)
    n = pos.shape[0]
    rest = HEAD_DIM - ROT_DIM
    c = jnp.concatenate([cos, cos, jnp.ones((n, rest), F32)], axis=1)
    s1 = jnp.concatenate([-sin, jnp.zeros((n, HEAD_DIM - half), F32)], axis=1)
    s2 = jnp.concatenate([jnp.zeros((n, half), F32), sin, jnp.zeros((n, rest), F32)], axis=1)
    return c, s1, s2


def _gelu_tanh(x):
    return 0.5 * x * (1.0 + jnp.tanh(math.sqrt(2.0 / math.pi) * (x + 0.044715 * (x * x * x))))


def _compress_blocks(x_ref, base, pitch, nblk, pe_ref, w1k_ref, w1v_ref, w2k_ref, w2v_ref, ko_ref, vo_ref):
    hk = jnp.zeros((nblk, CMP_HID), F32)
    hv = jnp.zeros((nblk, CMP_HID), F32)
    for r in range(BLOCK):
        xk = x_ref[pl.ds(base + 2 * r, nblk, stride=pitch), :] + pe_ref[0, r:r + 1, :]
        xv = x_ref[pl.ds(base + 2 * r + 1, nblk, stride=pitch), :] + pe_ref[1, r:r + 1, :]
        hk = hk + _dot(xk, w1k_ref[r])
        hv = hv + _dot(xv, w1v_ref[r])
    ko_ref[...] = _dot(_gelu_tanh(hk), w2k_ref[...])
    vo_ref[...] = _dot(_gelu_tanh(hv), w2v_ref[...])


def _compress_kernel(x_ref, *rest):
    _compress_blocks(x_ref, 0, BLK_ROWS, x_ref.shape[0] // BLK_ROWS, *rest)


def _compress_weights(cmp_pe, cmp_w1, cmp_w2, l):
    w1 = cmp_w1[l].astype(BF16).reshape(2, BLOCK, HEAD_DIM, CMP_HID)
    w2 = cmp_w2[l].astype(BF16)
    return cmp_pe[l], w1[0], w1[1], w2[0], w2[1]


def _cw_specs(nargs):
    z = (0,) * 3
    return [pl.BlockSpec((2, BLOCK, HEAD_DIM), lambda *a: z),
            pl.BlockSpec((BLOCK, HEAD_DIM, CMP_HID), lambda *a: z),
            pl.BlockSpec((BLOCK, HEAD_DIM, CMP_HID), lambda *a: z),
            pl.BlockSpec((CMP_HID, HEAD_DIM), lambda *a: (0, 0)),
            pl.BlockSpec((CMP_HID, HEAD_DIM), lambda *a: (0, 0))]


def _compress(x_rows, cw, tr):
    nblocks = x_rows.shape[0] // BLK_ROWS
    out = jax.ShapeDtypeStruct((nblocks, HEAD_DIM), F32)
    return pl.pallas_call(
        _compress_kernel,
        out_shape=(out, out),
        grid=(nblocks // tr,),
        in_specs=[pl.BlockSpec((tr * BLK_ROWS, HEAD_DIM), lambda i: (i, 0))] + _cw_specs(1),
        out_specs=(pl.BlockSpec((tr, HEAD_DIM), lambda i: (i, 0)),) * 2,
        compiler_params=_cparams(("arbitrary",)),
        name="compress",
    )(x_rows, *cw)


MAX_PAGES_PER_STEP = 32


def _compress_paged_kernel(pt_ref, cache_ref, pe_ref, w1k_ref, w1v_ref, w2k_ref, w2v_ref, ko_ref, vo_ref,
                           xbuf, sem, *, l):
    bpp = cache_ref.shape[2]
    nblk = ko_ref.shape[0]
    pages_per_step = nblk // bpp
    ns = pl.num_programs(1)
    step = pl.program_id(0) * ns + pl.program_id(1)
    total = pl.num_programs(0) * ns

    def copies(st, slot):
        bi, s = st // ns, st % ns
        out = []
        for k in range(pages_per_step):
            pg = pt_ref[bi, s * pages_per_step + k]
            for j in range(bpp):
                row0 = pl.multiple_of(slot * (nblk * BLK_PITCH) + (k * bpp + j) * BLK_PITCH, 8)
                out.append(pltpu.make_async_copy(cache_ref.at[pg, l, j], xbuf.at[pl.ds(row0, BLK_ROWS), :],
                                                 sem.at[slot]))
        return out

    slot = step % 2

    @pl.when(step == 0)
    def _():
        for c in copies(step, slot):
            c.start()

    @pl.when(step + 1 < total)
    def _():
        for c in copies(step + 1, 1 - slot):
            c.start()

    for c in copies(step, slot):
        c.wait()
    _compress_blocks(xbuf, slot * (nblk * BLK_PITCH), BLK_PITCH, nblk,
                     pe_ref, w1k_ref, w1v_ref, w2k_ref, w2v_ref, ko_ref, vo_ref)


def _compress_paged(cache, page_table, cw, l):
    b, n_pages = page_table.shape
    bpp = cache.shape[2]
    pages_per_step = math.gcd(n_pages, MAX_PAGES_PER_STEP)
    steps = n_pages // pages_per_step
    tr = pages_per_step * bpp
    out = jax.ShapeDtypeStruct((b * steps * tr, HEAD_DIM), F32)
    grid_spec = pltpu.PrefetchScalarGridSpec(
        num_scalar_prefetch=1,
        grid=(b, steps),
        in_specs=[pl.BlockSpec(memory_space=pl.ANY)] + _cw_specs(3),
        out_specs=(pl.BlockSpec((tr, HEAD_DIM), lambda bi, s, pt: (bi * steps + s, 0)),) * 2,
        scratch_shapes=[pltpu.VMEM((2 * tr * BLK_PITCH, HEAD_DIM), F32), pltpu.SemaphoreType.DMA((2,))],
    )
    return pl.pallas_call(
        functools.partial(_compress_paged_kernel, l=l),
        out_shape=(out, out),
        grid_spec=grid_spec,
        compiler_params=_cparams(("arbitrary", "arbitrary")),
        name="compress_paged",
    )(page_table, cache, *cw)


def _stable_topn_masks(scores, n_sel):
    rows, nb = scores[0].shape
    tn = (((0,), (0,)), ((), ()))
    eye_r = jnp.where(lax.broadcasted_iota(jnp.int32, (rows, rows), 0)
                      == lax.broadcasted_iota(jnp.int32, (rows, rows), 1), 1.0, 0.0)
    sts = [lax.dot_general(sc, eye_r, tn, preferred_element_type=F32, precision=lax.Precision.HIGHEST)
           for sc in scores]
    jrow = lax.broadcasted_iota(jnp.int32, (nb, rows), 0)
    ranks = [jnp.zeros((nb, rows), F32) for _ in scores]
    for k in range(nb):
        for p, st in enumerate(sts):
            ck = st[k:k + 1, :]
            beats = (ck > st) | ((ck == st) & (jrow > k))
            ranks[p] = ranks[p] + jnp.where(beats, 1.0, 0.0)
    eye_n = jnp.where(lax.broadcasted_iota(jnp.int32, (nb, nb), 0)
                      == lax.broadcasted_iota(jnp.int32, (nb, nb), 1), 1.0, 0.0).astype(BF16)
    return [lax.dot_general(jnp.where(r < n_sel, 1.0, 0.0).astype(BF16), eye_n, tn, preferred_element_type=F32)
            for r in ranks]


QBLOCKS_PER_STEP = 2
MASK_BIAS = -(2.0 ** 100)
SCORE_C = SCALE * 1.4426950408889634


def _softmax_pv(problems):
    ms = []
    for parts in problems:
        m = jnp.max(parts[0][0], axis=-1, keepdims=True)
        for s, _ in parts[1:]:
            m = jnp.maximum(m, jnp.max(s, axis=-1, keepdims=True))
        ms.append(m)
    dens = [0.0 for _ in problems]
    accs = [0.0 for _ in problems]
    for j in range(len(problems[0])):
        for p, parts in enumerate(problems):
            s, v = parts[j]
            e = jnp.exp2(s - ms[p])
            dens[p] = dens[p] + jnp.sum(e, axis=-1, keepdims=True)
            accs[p] = accs[p] + jnp.dot(e.astype(BF16), v, preferred_element_type=F32)
    return [a / d for a, d in zip(accs, dens)]


def _nsa_prompt_kernel(q_ref, sm_ref, kcb_ref, vcb_ref, att_ref, o_ref, os_ref, *, n_sel, wl, n_cls):
    probs = [(half, g) for half in range(QBLOCKS_PER_STEP) for g in range(NSA_KV)]
    np_ = len(probs)
    t_all = att_ref.shape[2]
    nb = kcb_ref.shape[2]
    rows = NSA_HPG * BLOCK
    nt = (((1,), (1,)), ((), ()))
    q = q_ref[...]
    qs = [jnp.concatenate([q[half * BLOCK:(half + 1) * BLOCK, (g * NSA_HPG + h) * HEAD_DIM:(g * NSA_HPG + h + 1) * HEAD_DIM]
                           for h in range(NSA_HPG)], axis=0) for half, g in probs]
    ib = [pl.program_id(1) * QBLOCKS_PER_STEP + half for half, _ in probs]
    q0 = [i * BLOCK for i in ib]
    tq = lax.broadcasted_iota(jnp.int32, (rows, 1), 0) & (BLOCK - 1)
    qpos = [a + tq for a in q0]

    start = [pl.multiple_of(jnp.clip(a - WINDOW, 0, t_all - wl), BLOCK) for a in q0]
    wiota = lax.broadcasted_iota(jnp.int32, (rows, wl), 1)
    wbias = [jnp.where((start[p] + wiota <= qpos[p]) & (start[p] + wiota >= qpos[p] - WINDOW), 0.0, MASK_BIAS)
             for p in range(np_)]
    s_w = [lax.dot_general(qs[p], att_ref[0, probs[p][1], pl.ds(start[p], wl), 3 * HEAD_DIM:4 * HEAD_DIM], nt,
                           preferred_element_type=F32) * SCORE_C + wbias[p] for p in range(np_)]
    o_w = _softmax_pv([[(s_w[p], att_ref[0, probs[p][1], pl.ds(start[p], wl), 4 * HEAD_DIM:5 * HEAD_DIM])]
                       for p in range(np_)])

    jc = lax.broadcasted_iota(jnp.int32, (rows, nb), 1)
    cmask = [(jc * BLOCK + (BLOCK - 1)) <= qp for qp in qpos]
    s_c = [jnp.where(cmask[p], _dot_nt(qs[p], kcb_ref[0, probs[p][1]]) * SCALE, NEG_INF) for p in range(np_)]
    m_c = [jnp.max(s, axis=-1, keepdims=True) for s in s_c]
    m_c = [jnp.where(m == NEG_INF, 0.0, m) for m in m_c]
    e_c = [jnp.where(cm, jnp.exp(s - m), 0.0) for cm, s, m in zip(cmask, s_c, m_c)]
    p_c = [e / jnp.maximum(jnp.sum(e, axis=-1, keepdims=True), 1e-30) for e in e_c]
    o_c = [_dot(p_c[p], vcb_ref[0, probs[p][1]]) for p in range(np_)]

    jb = lax.broadcasted_iota(jnp.int32, (BLOCK, nb), 1)
    scores = []
    for p in range(np_):
        imp = p_c[p][0:BLOCK]
        for h in range(1, NSA_HPG):
            imp = imp + p_c[p][h * BLOCK:(h + 1) * BLOCK]
        forced = (jb == 0) | (jb == ib[p]) | (jb == ib[p] - 1)
        scores.append(jnp.where(forced, 2.0 * NSA_HPG, jnp.where(jb > ib[p], -1.0, imp)))
    sel = _stable_topn_masks(scores, n_sel)
    tl = lax.broadcasted_iota(jnp.int32, (rows, BLOCK), 1)
    dbias = jnp.where(tl <= tq, 0.0, MASK_BIAS)
    q_aug, s_d, v_d = [], [], []
    for p, (half, g) in enumerate(probs):
        r0 = pl.multiple_of(q0[p], BLOCK)
        blk_bias = jnp.where((jb < ib[p]) & (sel[p] > 0.5), 0.0, MASK_BIAS)
        blk_bias = jnp.concatenate([blk_bias, jnp.zeros((BLOCK, HEAD_DIM - nb), F32)], axis=1).astype(BF16)
        q_aug.append(jnp.concatenate([qs[p], jnp.concatenate([blk_bias] * NSA_HPG, axis=0)], axis=1))
        s_d.append(lax.dot_general(qs[p], att_ref[0, g, pl.ds(r0, BLOCK), 0:HEAD_DIM], nt,
                                   preferred_element_type=F32) * SCORE_C + dbias)
        v_d.append(att_ref[0, g, pl.ds(r0, BLOCK), 2 * HEAD_DIM:3 * HEAD_DIM])
    per_cls = nb // n_cls
    for c in range(n_cls):
        nk = (c + 1) * per_cls * BLOCK

        @pl.when((ib[0] >= c * per_cls) & (ib[0] < (c + 1) * per_cls))
        def _():
            s_b = [lax.dot_general(q_aug[p], att_ref[0, probs[p][1], 0:nk, 0:2 * HEAD_DIM], nt,
                                   preferred_element_type=F32) * SCORE_C for p in range(np_)]
            o = _softmax_pv([[(s_b[p], att_ref[0, probs[p][1], 0:nk, 2 * HEAD_DIM:3 * HEAD_DIM]), (s_d[p], v_d[p])]
                             for p in range(np_)])
            for p in range(np_):
                os_ref[p] = o[p]

    for p, (half, g) in enumerate(probs):
        o_s = os_ref[p]
        gt = _sigmoid(sm_ref[half * BLOCK:(half + 1) * BLOCK, g * HEAD_DIM:(g + 1) * HEAD_DIM])
        for h in range(NSA_HPG):
            r = slice(h * BLOCK, (h + 1) * BLOCK)
            o = (gt[:, 3 * h:3 * h + 1] * o_c[p][r] + gt[:, 3 * h + 1:3 * h + 2] * o_s[r]
                 + gt[:, 3 * h + 2:3 * h + 3] * o_w[p][r])
            c0 = (g * NSA_HPG + h) * HEAD_DIM
            o_ref[half * BLOCK:(half + 1) * BLOCK, c0:c0 + HEAD_DIM] = o.astype(BF16)


def _nsa_prompt(q_rot, small, kcb, vcb, att, b, t):
    tq = QBLOCKS_PER_STEP * BLOCK
    nq = t // tq
    nb = kcb.shape[2]
    assert nb <= HEAD_DIM and nb % QBLOCKS_PER_STEP == 0
    wl = min(WINDOW + BLOCK, t)
    cb_spec = pl.BlockSpec((1, NSA_KV, nb, HEAD_DIM), lambda bi, i: (bi, 0, 0, 0))
    n_cls = math.gcd(nb // QBLOCKS_PER_STEP, 8)
    return pl.pallas_call(
        functools.partial(_nsa_prompt_kernel, n_sel=min(TOPN, nb), wl=wl, n_cls=n_cls),
        out_shape=jax.ShapeDtypeStruct((b * t, NSA_Q_W), BF16),
        grid=(b, nq),
        in_specs=[pl.BlockSpec((tq, NSA_Q_W), lambda bi, i: (bi * nq + i, 0)),
                  pl.BlockSpec((tq, SMALL_W), lambda bi, i: (bi * nq + i, 0)),
                  cb_spec, cb_spec,
                  pl.BlockSpec((1, NSA_KV, t, ATT_W), lambda bi, i: (bi, 0, 0, 0))],
        out_specs=pl.BlockSpec((tq, NSA_Q_W), lambda bi, i: (bi * nq + i, 0)),
        scratch_shapes=[pltpu.VMEM((QBLOCKS_PER_STEP * NSA_KV, NSA_HPG * BLOCK, HEAD_DIM), F32)],
        compiler_params=_cparams(("arbitrary", "arbitrary")),
        name="nsa_prompt",
    )(q_rot, small, kcb, vcb, att)


GDN_SOLVE_BLOCKS = 4


def _softplus(x):
    return jnp.maximum(x, 0.0) + jnp.log(1.0 + jnp.exp(-jnp.abs(x)))


def _unit_lower_solve(l_strict, rhs):
    n = l_strict[0].shape[0]
    sub = 8
    nt = n // sub
    tiles = [[r[t * sub:(t + 1) * sub, :] for t in range(nt)] for r in rhs]
    for i in range(n - 1):
        t0, s = divmod(i, sub)
        for p, lm in enumerate(l_strict):
            xi = tiles[p][t0][s:s + 1, :]
            col = lm[:, i:i + 1]
            for t in range(t0, nt):
                tiles[p][t] = tiles[p][t] - col[t * sub:(t + 1) * sub, :] * xi
    return [jnp.concatenate(tp, axis=0) for tp in tiles]


def _gdn_prompt_kernel(x_ref, z_ref, sm_ref, abr_ref, cw_ref, prow_ref, pcol_ref, nw_ref,
                       o_ref, s_out_ref, conv_out_ref, s_ref, xe_ref):
    n = pl.program_id(1)
    c = GDN_CHUNK
    pad = 8

    @pl.when(n == 0)
    def _():
        s_ref[...] = jnp.zeros_like(s_ref)
        xe_ref[0:pad, :] = jnp.zeros((pad, CONV_CH), F32)

    xe_ref[pad:pad + c, :] = x_ref[...]
    y = xe_ref[pl.ds(pad, c), :] * cw_ref[CONV_W - 1:CONV_W, :]
    for w in range(CONV_W - 1):
        y = y + xe_ref[pl.ds(pad - (CONV_W - 1) + w, c), :] * cw_ref[w:w + 1, :]
    y = _silu(y)

    ri = lax.broadcasted_iota(jnp.int32, (c, c), 0)
    ci = lax.broadcasted_iota(jnp.int32, (c, c), 1)
    tri = ri >= ci
    strict = ri > ci
    tri_f = jnp.where(tri, 1.0, 0.0)

    a_col = sm_ref[:, A_COL:A_COL + GDN_HEADS]
    b_col = sm_ref[:, B_COL:B_COL + GDN_HEADS]
    g_col = -jnp.exp(prow_ref[0:1, :]) * _softplus(a_col + prow_ref[1:2, :])
    g_row = -jnp.exp(pcol_ref[:, 0:1]) * _softplus(abr_ref[0, 0, 0:GDN_HEADS, :] + pcol_ref[:, 1:2])
    gc_col = _dot_exact(tri_f, g_col)
    gc_row = _dot_nt_exact(g_row, tri_f)
    beta_col = _sigmoid(b_col)

    heads = range(GDN_HEADS)
    yq = [y[:, h * HEAD_DIM:(h + 1) * HEAD_DIM] for h in heads]
    yk = [y[:, GDN_W + h * HEAD_DIM:GDN_W + (h + 1) * HEAD_DIM] for h in heads]
    yv = [y[:, 2 * GDN_W + h * HEAD_DIM:2 * GDN_W + (h + 1) * HEAD_DIM] for h in heads]
    qn = [jnp.sum(a * a, axis=-1, keepdims=True) for a in yq]
    kn = [jnp.sum(a * a, axis=-1, keepdims=True) for a in yk]
    qh = [a * lax.rsqrt(n + EPS) * SCALE for a, n in zip(yq, qn)]
    kh = [a * lax.rsqrt(n + EPS) for a, n in zip(yk, kn)]
    gccs = [gc_col[:, h:h + 1] for h in heads]
    betas = [beta_col[:, h:h + 1] for h in heads]
    decays = [jnp.exp(jnp.where(tri, gccs[h] - gc_row[h:h + 1, :], NEG_INF)) for h in heads]
    kbs = [k * b for k, b in zip(kh, betas)]
    kks = [_dot_nt(kb, k) for kb, k in zip(kbs, kh)]
    qks = [_dot_nt(q, k) for q, k in zip(qh, kh)]
    lmats = [jnp.where(strict, kk * d, 0.0) for kk, d in zip(kks, decays)]
    qkds = [qk * d for qk, d in zip(qks, decays)]
    rhss = [jnp.concatenate([yv[h] * betas[h], kbs[h] * jnp.exp(gccs[h])], axis=1) for h in heads]
    pre = [(qh[h], kh[h], gccs[h], qkds[h], lmats[h], rhss[h]) for h in heads]

    nblk = GDN_SOLVE_BLOCKS
    rb = c // nblk
    l_list, r_list = [], []
    for p in pre:
        lmat, rhs = p[4], p[5]
        for b in range(nblk):
            r0 = b * rb
            l_list.append(lmat[r0:r0 + rb, r0:r0 + rb])
            if b == 0:
                r_list.append(rhs[0:rb, :])
            else:
                r_list.append(jnp.concatenate([rhs[r0:r0 + rb, :], lmat[r0:r0 + rb, 0:r0],
                                               jnp.zeros((rb, HEAD_DIM - r0), F32)], axis=1))
    part = _unit_lower_solve(l_list, r_list)
    xs = [[part[h * nblk][:, 0:2 * HEAD_DIM]] for h in range(GDN_HEADS)]
    for b in range(1, nblk):
        for h in range(GDN_HEADS):
            y = part[h * nblk + b]
            xs[h].append(y[:, 0:2 * HEAD_DIM]
                         - _dot_exact(y[:, 2 * HEAD_DIM:2 * HEAD_DIM + b * rb], jnp.concatenate(xs[h], axis=0)))
    sols = [jnp.concatenate(x, axis=0) for x in xs]

    tn = (((0,), (0,)), ((), ()))
    s_in = [s_ref[h] for h in heads]
    ws = [_dot(sols[h][:, HEAD_DIM:], s_in[h]) for h in heads]
    v_new = [sols[h][:, :HEAD_DIM] - ws[h] for h in heads]
    qs_ = [_dot(qh[h] * jnp.exp(gccs[h]), s_in[h]) for h in heads]
    os_ = [qs_[h] + _dot(qkds[h], v_new[h]) for h in heads]
    g_last = [g[c - 1:c, :] for g in gccs]
    k_dec = [kh[h] * jnp.exp(g_last[h] - gccs[h]) for h in heads]
    s_out = [s_in[h] * jnp.exp(g_last[h])
             + lax.dot_general(k_dec[h].astype(BF16), v_new[h].astype(BF16), tn, preferred_element_type=F32)
             for h in heads]
    ms_ = [jnp.mean(o * o, axis=-1, keepdims=True) for o in os_]
    o_out = [(os_[h] * lax.rsqrt(ms_[h] + EPS) * nw_ref[...]
              * _silu(z_ref[:, h * HEAD_DIM:(h + 1) * HEAD_DIM])).astype(BF16) for h in heads]
    for h in range(GDN_HEADS):
        s_ref[h] = s_out[h]
        o_ref[:, h * HEAD_DIM:(h + 1) * HEAD_DIM] = o_out[h]
    xe_ref[0:pad, :] = xe_ref[c:c + pad, :]

    @pl.when(n == pl.num_programs(1) - 1)
    def _():
        s_out_ref[0] = s_ref[...]
        conv_out_ref[0] = xe_ref[pad - (CONV_W - 1):pad, :]


def _gdn_prompt(proj, small, conv_w, a_log, dt_bias, norm_w, b, t):
    c = GDN_CHUNK
    nc = t // c
    ab_row = small.reshape(b, nc, c, SMALL_W)[..., A_COL:A_COL + 2 * GDN_HEADS].transpose(0, 1, 3, 2)
    prow = jnp.stack([a_log, dt_bias], axis=0)
    pcol = prow.T
    return pl.pallas_call(
        _gdn_prompt_kernel,
        out_shape=(jax.ShapeDtypeStruct((b * t, GDN_W), BF16),
                   jax.ShapeDtypeStruct((b, GDN_HEADS, HEAD_DIM, HEAD_DIM), F32),
                   jax.ShapeDtypeStruct((b, CONV_W - 1, CONV_CH), F32)),
        grid=(b, nc),
        in_specs=[pl.BlockSpec((c, CONV_CH), lambda bi, n: (bi * nc + n, OFF_CONV // CONV_CH)),
                  pl.BlockSpec((c, GDN_W), lambda bi, n: (bi * nc + n, OFF_Z // GDN_W)),
                  pl.BlockSpec((c, HEAD_DIM), lambda bi, n: (bi * nc + n, 0)),
                  pl.BlockSpec((1, 1, 2 * GDN_HEADS, c), lambda bi, n: (bi, n, 0, 0)),
                  pl.BlockSpec((CONV_W, CONV_CH), lambda bi, n: (0, 0)),
                  pl.BlockSpec((2, GDN_HEADS), lambda bi, n: (0, 0)),
                  pl.BlockSpec((GDN_HEADS, 2), lambda bi, n: (0, 0)),
                  pl.BlockSpec((1, HEAD_DIM), lambda bi, n: (0, 0))],
        out_specs=(pl.BlockSpec((c, GDN_W), lambda bi, n: (bi * nc + n, 0)),
                   pl.BlockSpec((1, GDN_HEADS, HEAD_DIM, HEAD_DIM), lambda bi, n: (bi, 0, 0, 0)),
                   pl.BlockSpec((1, CONV_W - 1, CONV_CH), lambda bi, n: (bi, 0, 0))),
        scratch_shapes=[pltpu.VMEM((GDN_HEADS, HEAD_DIM, HEAD_DIM), F32),
                        pltpu.VMEM((c + 16, CONV_CH), F32)],
        compiler_params=_cparams(("arbitrary", "arbitrary")),
        name="gdn_prompt",
    )(proj, proj, small, ab_row, conv_w, prow, pcol, norm_w.reshape(1, HEAD_DIM))


def _outproj_kernel(on_ref, og_ref, w_ref, x_ref, ga_ref, g_ref, o_ref):
    o = jnp.concatenate([on_ref[...], og_ref[...]], axis=1)
    mix = jnp.dot(o, w_ref[0], preferred_element_type=F32)
    o_ref[...] = x_ref[...] + ga_ref[0] * _rms(mix, g_ref[...])


def _out_proj(o_nsa, o_gdn, w_out, x, ga, g, l, rows_per_mod, tm):
    m, d = x.shape
    kw = o_nsa.shape[1]
    return pl.pallas_call(
        _outproj_kernel,
        out_shape=jax.ShapeDtypeStruct((m, d), F32),
        grid=(m // tm,),
        in_specs=[pl.BlockSpec((tm, kw), lambda i: (i, 0)),
                  pl.BlockSpec((tm, kw), lambda i: (i, 0)),
                  pl.BlockSpec((1, 2 * kw, d), lambda i: (l, 0, 0)),
                  pl.BlockSpec((tm, d), lambda i: (i, 0)),
                  _mod_spec(ga, rows_per_mod, tm),
                  pl.BlockSpec((1, d), lambda i: (0, 0))],
        out_specs=pl.BlockSpec((tm, d), lambda i: (i, 0)),
        compiler_params=_cparams(("arbitrary",)),
        name="out_proj",
    )(o_nsa, o_gdn, w_out, x, ga, g)


def _ffn_kernel(x_ref, g1_ref, sc_ref, sh_ref, wg_ref, wu_ref, wd_ref, ga_ref, g2_ref, o_ref, h_ref, acc_ref):
    j = pl.program_id(1)

    @pl.when(j == 0)
    def _():
        h = _rms(x_ref[...], g1_ref[...]) * (1.0 + sc_ref[0]) + sh_ref[0]
        h_ref[...] = h.astype(BF16)
        acc_ref[...] = jnp.zeros_like(acc_ref)

    h = h_ref[...]
    a = jnp.dot(h, wg_ref[0], preferred_element_type=F32)
    u = jnp.dot(h, wu_ref[0], preferred_element_type=F32)
    acc_ref[...] += _dot(_silu(a) * u, wd_ref[0])

    @pl.when(j == pl.num_programs(1) - 1)
    def _():
        o_ref[...] = x_ref[...] + ga_ref[0] * _rms(acc_ref[...], g2_ref[...])


def _ffn(x, g1, sc, sh, w_gate, w_up, w_down, ga, g2, l, rows_per_mod, tm, tf=512):
    m, d = x.shape
    f = w_gate.shape[2]
    vec = pl.BlockSpec((1, d), lambda i, j: (0, 0))
    return pl.pallas_call(
        _ffn_kernel,
        out_shape=jax.ShapeDtypeStruct((m, d), F32),
        grid=(m // tm, f // tf),
        in_specs=[pl.BlockSpec((tm, d), lambda i, j: (i, 0)), vec,
                  _mod_spec(sc, rows_per_mod, tm), _mod_spec(sh, rows_per_mod, tm),
                  pl.BlockSpec((1, d, tf), lambda i, j: (l, 0, j)),
                  pl.BlockSpec((1, d, tf), lambda i, j: (l, 0, j)),
                  pl.BlockSpec((1, tf, d), lambda i, j: (l, j, 0)),
                  _mod_spec(ga, rows_per_mod, tm), vec],
        out_specs=pl.BlockSpec((tm, d), lambda i, j: (i, 0)),
        scratch_shapes=[pltpu.VMEM((tm, d), BF16), pltpu.VMEM((tm, d), F32)],
        compiler_params=_cparams(("arbitrary", "arbitrary")),
        name="ffn",
    )(x, g1, sc, sh, w_gate, w_up, w_down, ga, g2)


def _stack_heads(q, h0):
    rows = [q[:, (h0 + h) * HEAD_DIM:(h0 + h + 1) * HEAD_DIM] for h in range(NSA_HPG)]
    return jnp.concatenate(rows + [jnp.zeros((NSA_HPG, HEAD_DIM), F32)], axis=0)


def _nsa_sample_cmp_kernel(q_ref, kcb_ref, vcb_ref, oc_ref, idx_ref, *, n_sel, q_pos):
    nseq, nbp = kcb_ref.shape[0], kcb_ref.shape[2]
    cur = q_pos // BLOCK
    imps = []
    for bi in range(nseq):
        q = q_ref[bi].astype(F32)
        for g in range(NSA_KV):
            qs = _stack_heads(q, g * NSA_HPG)
            jc = lax.broadcasted_iota(jnp.int32, (qs.shape[0], nbp), 1)
            cmask = (jc * BLOCK + (BLOCK - 1)) <= q_pos
            p_c = _masked_softmax(_dot_nt(qs, kcb_ref[bi, g]) * SCALE, cmask)
            o_c = _dot(p_c, vcb_ref[bi, g])
            for h in range(NSA_HPG):
                c0 = (g * NSA_HPG + h) * HEAD_DIM
                oc_ref[bi, :, c0:c0 + HEAD_DIM] = o_c[h:h + 1]
            imp = p_c[0:1]
            for h in range(1, NSA_HPG):
                imp = imp + p_c[h:h + 1]
            imps.append(imp)
    imp = jnp.concatenate(imps, axis=0)
    jb = lax.broadcasted_iota(jnp.int32, imp.shape, 1)
    forced = (jb == 0) | (jb == cur) | (jb == cur - 1)
    score = jnp.where(forced, jnp.inf, jnp.where(jb > cur, NEG_INF, imp))
    left = jb >= 0
    for r in range(n_sel):
        best = jnp.max(jnp.where(left, score, NEG_INF), axis=1, keepdims=True)
        pick = jnp.min(jnp.where(left & (score == best), jb, nbp), axis=1, keepdims=True)
        idx_ref[r] = jnp.broadcast_to(pick, (imp.shape[0], HEAD_DIM))
        left = left & (jb != pick)


def _nsa_sample_cmp(q_rot, kcb, vcb, q_pos, n_blocks):
    b = q_rot.shape[0]
    nbp = kcb.shape[2]
    n_sel = min(TOPN, n_blocks)
    o_c, idx = pl.pallas_call(
        functools.partial(_nsa_sample_cmp_kernel, n_sel=n_sel, q_pos=q_pos),
        out_shape=(jax.ShapeDtypeStruct((b, 1, NSA_Q_W), F32),
                   jax.ShapeDtypeStruct((n_sel, b * NSA_KV, HEAD_DIM), jnp.int32)),
        compiler_params=pltpu.CompilerParams(vmem_limit_bytes=VMEM_LIMIT),
        name="nsa_sample_cmp",
    )(q_rot.reshape(b, 1, NSA_Q_W), kcb, vcb)
    return o_c, idx[:, :, 0].T.reshape(b, NSA_KV, n_sel)


def _nsa_sample_attn_kernel(idx_ref, pt_ref, q_ref, sm_ref, oc_ref, cache_ref, newsel_ref, win_ref, newwin_ref,
                            o_ref, wino_ref, blk_buf, sem, *, l, npb, n_sel):
    bi, g = pl.program_id(0), pl.program_id(1)
    bpp = cache_ref.shape[3]
    copies, picks = [], []
    for n in range(n_sel):
        pick = idx_ref[bi, g, n]
        jp = jnp.minimum(pick, npb - 1)
        cp = pltpu.make_async_copy(cache_ref.at[pt_ref[bi, jp // bpp], l, g, jp % bpp],
                                   blk_buf.at[pl.ds(n * BLK_ROWS, BLK_ROWS), :], sem)
        cp.start()
        copies.append(cp)
        picks.append(pick)

    qs = _stack_heads(q_ref[0].astype(F32), 0)

    def row_scores(k):
        return jnp.sum(qs * k.astype(BF16).astype(F32), axis=-1, keepdims=True) * SCALE

    wb = win_ref.shape[3] // 2
    s_buf = _dot_nt(qs, win_ref[0, 0, 0, pl.ds(0, wb, stride=2), :]) * SCALE
    s_new = row_scores(newwin_ref[0, 0, 0:1, :])
    m = jnp.maximum(jnp.max(s_buf, axis=-1, keepdims=True), s_new)
    e_buf, e_new = jnp.exp(s_buf - m), jnp.exp(s_new - m)
    den = jnp.maximum(jnp.sum(e_buf, axis=-1, keepdims=True) + e_new, 1e-30)
    o_w = (_dot(e_buf, win_ref[0, 0, 0, pl.ds(1, wb, stride=2), :])
           + e_new * newwin_ref[0, 0, 1:2, :].astype(BF16).astype(F32)) / den
    wino_ref[0, 0, 0:2 * wb - 2, :] = win_ref[0, 0, 0, 2:2 * wb, :]
    wino_ref[0, 0, 2 * wb - 2:2 * wb, :] = newwin_ref[0, 0]

    for cp in copies:
        cp.wait()
    nk = n_sel * BLOCK
    slot = lax.broadcasted_iota(jnp.int32, (1, nk), 1) // BLOCK
    bias = jnp.zeros((1, nk), F32)
    has_new = picks[0] >= npb
    for n in range(n_sel):
        bias = jnp.where((slot == n) & (picks[n] >= npb), NEG_INF, bias)
        has_new = has_new | (picks[n] >= npb)
    s_blk = _dot_nt(qs, blk_buf[pl.ds(0, nk, stride=2), :]) * SCALE + bias
    s_row = jnp.where(has_new, row_scores(newsel_ref[0, 0, 0:1, :]), NEG_INF)
    m = jnp.maximum(jnp.max(s_blk, axis=-1, keepdims=True), s_row)
    m = jnp.where(m == NEG_INF, 0.0, m)
    e_blk, e_row = jnp.exp(s_blk - m), jnp.exp(s_row - m)
    den = jnp.maximum(jnp.sum(e_blk, axis=-1, keepdims=True) + e_row, 1e-30)
    o_s = (_dot(e_blk, blk_buf[pl.ds(1, nk, stride=2), :])
           + e_row * newsel_ref[0, 0, 1:2, :].astype(BF16).astype(F32)) / den

    gt = _sigmoid(sm_ref[0])
    for h in range(NSA_HPG):
        c0 = h * HEAD_DIM
        o = (gt[:, 3 * h:3 * h + 1] * oc_ref[0, :, c0:c0 + HEAD_DIM] + gt[:, 3 * h + 1:3 * h + 2] * o_s[h:h + 1]
             + gt[:, 3 * h + 2:3 * h + 3] * o_w[h:h + 1])
        o_ref[0, :, c0:c0 + HEAD_DIM] = o.astype(BF16)


def _nsa_sample_attn(idx, page_table, q_rot, small, o_c, cache_sel, new_sel, cache_win, new_win, l, npb):
    b = q_rot.shape[0]
    n_sel = idx.shape[2]
    bpp = cache_sel.shape[3]
    wb2 = cache_win.shape[3]
    gw = NSA_HPG * HEAD_DIM

    new_spec = pl.BlockSpec((1, 1, 2, HEAD_DIM), lambda bi, g, *_: (bi, g, 0, 0))
    grid_spec = pltpu.PrefetchScalarGridSpec(
        num_scalar_prefetch=2,
        grid=(b, NSA_KV),
        in_specs=[pl.BlockSpec((1, 1, gw), lambda bi, g, *_: (bi, 0, g)),
                  pl.BlockSpec((1, 1, HEAD_DIM), lambda bi, g, *_: (bi, 0, g)),
                  pl.BlockSpec((1, 1, gw), lambda bi, g, *_: (bi, 0, g)),
                  pl.BlockSpec(memory_space=pl.ANY),
                  new_spec,
                  pl.BlockSpec((1, 1, 1, wb2, HEAD_DIM), lambda bi, g, *_: (bi, l, g, 0, 0)),
                  new_spec],
        out_specs=(pl.BlockSpec((1, 1, gw), lambda bi, g, *_: (bi, 0, g)),
                   pl.BlockSpec((1, 1, wb2, HEAD_DIM), lambda bi, g, *_: (bi, g, 0, 0))),
        scratch_shapes=[pltpu.VMEM((n_sel * BLK_ROWS, HEAD_DIM), F32), pltpu.SemaphoreType.DMA(())],
    )
    return pl.pallas_call(
        functools.partial(_nsa_sample_attn_kernel, l=l, npb=npb, n_sel=n_sel),
        out_shape=(jax.ShapeDtypeStruct((b, 1, NSA_Q_W), BF16),
                   jax.ShapeDtypeStruct((b, NSA_KV, wb2, HEAD_DIM), F32)),
        grid_spec=grid_spec,
        compiler_params=_cparams(("arbitrary", "arbitrary")),
        name="nsa_sample_attn",
    )(idx, page_table, q_rot.reshape(b, 1, NSA_Q_W), small.reshape(b, 1, SMALL_W), o_c,
      cache_sel, new_sel, cache_win, new_win)


def _gdn_sample_kernel(x_ref, z_ref, sm_ref, buf_ref, s0_ref, cw_ref, prow_ref, nw_ref, eye_ref,
                       o_ref, s_out_ref, conv_out_ref):
    x = x_ref[0]
    buf = buf_ref[0, 0]
    y = x * cw_ref[CONV_W - 1:CONV_W, :]
    for w in range(CONV_W - 1):
        y = y + buf[w:w + 1, :] * cw_ref[w:w + 1, :]
    y = _silu(y)
    conv_out_ref[0, 0:CONV_W - 2, :] = buf[1:CONV_W - 1, :]
    conv_out_ref[0, CONV_W - 2:CONV_W - 1, :] = x

    def heads(off):
        return jnp.concatenate([y[:, off + h * HEAD_DIM:off + (h + 1) * HEAD_DIM] for h in range(GDN_HEADS)], axis=0)

    yq, yk, v = heads(0), heads(GDN_W), heads(2 * GDN_W)
    q = yq * lax.rsqrt(jnp.sum(yq * yq, axis=-1, keepdims=True) + EPS) * SCALE
    k = yk * lax.rsqrt(jnp.sum(yk * yk, axis=-1, keepdims=True) + EPS)
    sm = sm_ref[0]
    g_row = -jnp.exp(prow_ref[0:1, :]) * _softplus(sm[:, A_COL:A_COL + GDN_HEADS] + prow_ref[1:2, :])
    beta_row = _sigmoid(sm[:, B_COL:B_COL + GDN_HEADS])
    eg_row = jnp.exp(g_row)
    k_t = _dot_nt_exact(eye_ref[...], k)
    q_t = _dot_nt_exact(eye_ref[...], q)
    z = z_ref[0]
    for h in range(GDN_HEADS):
        s0 = s0_ref[0, 0, h]
        eg = eg_row[:, h:h + 1]
        kc = k_t[:, h:h + 1].astype(BF16).astype(F32)
        qc = q_t[:, h:h + 1].astype(BF16).astype(F32)
        s0b = s0.astype(BF16).astype(F32)
        ks = jnp.sum(kc * beta_row[:, h:h + 1] * eg * s0b, axis=0, keepdims=True)
        v_new = v[h:h + 1] * beta_row[:, h:h + 1] - ks
        vb = v_new.astype(BF16).astype(F32)
        s1 = s0 * eg + kc * vb
        qe = (q_t[:, h:h + 1] * eg).astype(BF16).astype(F32)
        o = jnp.sum(qe * s0b, axis=0, keepdims=True) + jnp.sum(qc * kc, axis=0, keepdims=True) * vb
        s_out_ref[0, h] = s1
        sl = slice(h * HEAD_DIM, (h + 1) * HEAD_DIM)
        o = _rms(o, nw_ref[...]) * _silu(z[:, sl])
        o_ref[0, :, sl] = o.astype(BF16)


def _gdn_sample(proj, small, state_conv, state_gdn, conv_w, a_log, dt_bias, norm_w, l, b):
    prow = jnp.stack([a_log, dt_bias], axis=0)
    rows = proj.shape[0]
    return pl.pallas_call(
        _gdn_sample_kernel,
        out_shape=(jax.ShapeDtypeStruct((b, 1, GDN_W), BF16),
                   jax.ShapeDtypeStruct((b, GDN_HEADS, HEAD_DIM, HEAD_DIM), F32),
                   jax.ShapeDtypeStruct((b, CONV_W - 1, CONV_CH), F32)),
        grid=(b,),
        in_specs=[pl.BlockSpec((1, 1, CONV_CH), lambda bi: (bi, 0, OFF_CONV // CONV_CH)),
                  pl.BlockSpec((1, 1, GDN_W), lambda bi: (bi, 0, OFF_Z // GDN_W)),
                  pl.BlockSpec((1, 1, SMALL_W), lambda bi: (bi, 0, 0)),
                  pl.BlockSpec((1, 1, CONV_W - 1, CONV_CH), lambda bi: (bi, l, 0, 0)),
                  pl.BlockSpec((1, 1, GDN_HEADS, HEAD_DIM, HEAD_DIM), lambda bi: (bi, l, 0, 0, 0)),
                  pl.BlockSpec((CONV_W, CONV_CH), lambda bi: (0, 0)),
                  pl.BlockSpec((2, GDN_HEADS), lambda bi: (0, 0)),
                  pl.BlockSpec((1, HEAD_DIM), lambda bi: (0, 0)),
                  pl.BlockSpec((HEAD_DIM, HEAD_DIM), lambda bi: (0, 0))],
        out_specs=(pl.BlockSpec((1, 1, GDN_W), lambda bi: (bi, 0, 0)),
                   pl.BlockSpec((1, GDN_HEADS, HEAD_DIM, HEAD_DIM), lambda bi: (bi, 0, 0, 0)),
                   pl.BlockSpec((1, CONV_W - 1, CONV_CH), lambda bi: (bi, 0, 0))),
        compiler_params=_cparams(("arbitrary",)),
        name="gdn_sample",
    )(proj.reshape(rows, 1, MAIN_W), proj.reshape(rows, 1, MAIN_W), small.reshape(rows, 1, SMALL_W),
      state_conv, state_gdn, conv_w, prow, norm_w.reshape(1, HEAD_DIM), jnp.eye(HEAD_DIM, dtype=F32))


def _repack_w_in(w_in):
    kvw = NSA_KV * HEAD_DIM
    o_q = 0
    o_kv = [NSA_Q_W + i * kvw for i in range(6)]
    o_gl = NSA_Q_W + 6 * kvw
    o_conv = o_gl + 3 * NSA_HEADS
    o_a = o_conv + CONV_CH
    o_b = o_a + GDN_HEADS
    o_z = o_b + GDN_HEADS

    def cols(a, n):
        return w_in[:, :, a:a + n]

    pieces = [cols(o_conv, CONV_CH), cols(o_z, GDN_W), cols(o_q, NSA_Q_W)]
    for pair in range(3):
        for g in range(NSA_KV):
            pieces.append(cols(o_kv[2 * pair] + g * HEAD_DIM, HEAD_DIM))
            pieces.append(cols(o_kv[2 * pair + 1] + g * HEAD_DIM, HEAD_DIM))
    main = jnp.concatenate(pieces, axis=-1).astype(BF16)
    depth, d, _ = w_in.shape
    ngl = 3 * NSA_HPG
    small = jnp.concatenate([
        cols(o_gl, ngl), cols(o_a, GDN_HEADS), cols(o_b, GDN_HEADS),
        jnp.zeros((depth, d, HEAD_DIM - ngl - 2 * GDN_HEADS), w_in.dtype),
        cols(o_gl + ngl, ngl), jnp.zeros((depth, d, HEAD_DIM - ngl), w_in.dtype)], axis=-1).astype(BF16)
    return main, small


def _mods(mod_l, lo, hi, broadcast):
    d = mod_l.shape[1] // 6
    out = []
    for k in range(6):
        m = mod_l[lo:hi, k * d:(k + 1) * d]
        out.append(m[:, None, :] if broadcast else m[None])
    return out


def kernel(x_prompt, x_sample, cache_cmp_kv, cache_sel_kv, cache_win_kv, state_gdn, state_conv, page_table,
           c_prompt, c_sample, w_ada, b_ada, g_pre_mix, w_in, cmp_pe, cmp_w1, cmp_w2, conv_w, gdn_a_log,
           gdn_dt_bias, gdn_norm, w_out, g_post_mix, g_pre_ffn, w_gate, w_up, w_down, g_post_ffn):
    bp, t, d = x_prompt.shape
    bs = x_sample.shape[0]
    depth = w_in.shape[0]
    n_pool, _, _, page, _, _ = cache_cmp_kv.shape
    past = page_table.shape[1] * page
    npb = past // BLOCK
    bpp = page // BLOCK
    wb = cache_win_kv.shape[3]
    ts = 16
    assert x_sample.shape[1] == 1 and bs <= ts and t % BLOCK == 0 and page % BLOCK == 0

    c_all = jnp.concatenate([c_sample, jnp.zeros((ts - bs, d), F32), c_prompt,
                             jnp.zeros((-bp % 8, d), F32)], axis=0)
    mod = _ada_mod(c_all, w_ada, b_ada)

    w_main, w_small = _repack_w_in(w_in)
    w_out_b, w_gate_b, w_up_b, w_down_b = (w.astype(BF16) for w in (w_out, w_gate, w_up, w_down))
    rope_p = _rope_tables(jnp.arange(t))
    rope_s = _rope_tables(jnp.full((ts,), past))

    tm = min(512, t)
    xp = x_prompt.reshape(bp * t, d)
    xs = jnp.concatenate([x_sample.reshape(bs, d), jnp.zeros((ts - bs, d), F32)], axis=0)
    cache_cmp_blocks = cache_cmp_kv.reshape(n_pool, depth, NSA_KV * bpp, BLK_ROWS, HEAD_DIM)
    cache_sel_blocks = cache_sel_kv.reshape(n_pool, depth, NSA_KV, bpp, BLK_ROWS, HEAD_DIM)
    cache_win_rows = cache_win_kv.reshape(bs, depth, NSA_KV, 2 * wb, HEAD_DIM)
    nbp = -(-(npb + 1) // HEAD_DIM) * HEAD_DIM

    p_states, s_states = [], []
    for l in range(depth):
        gpm, gpo, gpf, gpof = (g[l][None] for g in (g_pre_mix, g_post_mix, g_pre_ffn, g_post_ffn))
        cw = _compress_weights(cmp_pe, cmp_w1, cmp_w2, l)

        sh1, sc1, ga1, sh2, sc2, ga2 = _mods(mod[l], ts, ts + bp, True)
        proj, small = _in_proj(xp, gpm, sc1, sh1, w_main, w_small, l, t, tm)
        q_rot, cmp_st, sel_st, win_st, att = _post_proj(proj, rope_p, bp, t, min(256, t))
        nb = t // BLOCK
        kcb, vcb = _compress(cmp_st.reshape(bp * NSA_KV * 2 * t, HEAD_DIM), cw, math.gcd(bp * NSA_KV * nb, 128))
        kcb, vcb = (a.reshape(bp, NSA_KV, nb, HEAD_DIM) for a in (kcb, vcb))
        o_nsa = _nsa_prompt(q_rot, small, kcb, vcb, att, bp, t)
        o_gdn, s_fin, conv_fin = _gdn_prompt(proj, small, conv_w[l], gdn_a_log[l], gdn_dt_bias[l], gdn_norm[l], bp, t)
        xp = _out_proj(o_nsa, o_gdn, w_out_b, xp, ga1, gpo, l, t, tm)
        xp = _ffn(xp, gpf, sc2, sh2, w_gate_b, w_up_b, w_down_b, ga2, gpof, l, t, tm)
        wl = min(WINDOW, t)
        p_states.append((cmp_st.reshape(bp, NSA_KV, t, 2, HEAD_DIM), sel_st.reshape(bp, NSA_KV, t, 2, HEAD_DIM),
                         win_st[:, :, 2 * (t - wl):].reshape(bp, NSA_KV, wl, 2, HEAD_DIM), s_fin, conv_fin))

        sh1, sc1, ga1, sh2, sc2, ga2 = _mods(mod[l], 0, ts, False)
        proj, small = _in_proj(xs, gpm, sc1, sh1, w_main, w_small, l, ts, ts)
        q_rot, cmp_new, sel_new, win_new, _ = _post_proj(proj, rope_s, 1, ts, ts)
        cmp_new, sel_new, win_new = (a.reshape(NSA_KV, ts, 2, HEAD_DIM)[:, :bs].transpose(1, 0, 2, 3)
                                     for a in (cmp_new, sel_new, win_new))
        kc_past, vc_past = _compress_paged(cache_cmp_blocks, page_table, cw, l)
        new_blk = jnp.pad(cmp_new.reshape(bs * NSA_KV, 2, HEAD_DIM), ((0, 0), (0, BLK_ROWS - 2), (0, 0)))
        kc_new, vc_new = _compress(new_blk.reshape(bs * NSA_KV * BLK_ROWS, HEAD_DIM), cw, bs * NSA_KV)

        def summaries(past_rows, new_rows):
            a = past_rows.reshape(bs, past // page, NSA_KV, bpp, HEAD_DIM).transpose(0, 2, 1, 3, 4)
            a = a.reshape(bs, NSA_KV, npb, HEAD_DIM)
            a = jnp.concatenate([a, new_rows.reshape(bs, NSA_KV, 1, HEAD_DIM)], axis=2)
            return jnp.pad(a, ((0, 0), (0, 0), (0, nbp - npb - 1), (0, 0)))

        kcb, vcb = summaries(kc_past, kc_new), summaries(vc_past, vc_new)
        o_c, idx = _nsa_sample_cmp(q_rot[:bs], kcb, vcb, past, npb + 1)
        o_nsa, win_out = _nsa_sample_attn(idx, page_table, q_rot[:bs], small[:bs], o_c, cache_sel_blocks,
                                          sel_new, cache_win_rows, win_new, l, npb)
        o_gdn, s_fin, conv_fin = _gdn_sample(proj, small, state_conv, state_gdn, conv_w[l], gdn_a_log[l],
                                             gdn_dt_bias[l], gdn_norm[l], l, bs)
        pad_rows = ((0, ts - bs), (0, 0))
        o_nsa = jnp.pad(o_nsa.reshape(bs, NSA_Q_W), pad_rows)
        o_gdn = jnp.pad(o_gdn.reshape(bs, GDN_W), pad_rows)
        xs = _out_proj(o_nsa, o_gdn, w_out_b, xs, ga1, gpo, l, ts, ts)
        xs = _ffn(xs, gpf, sc2, sh2, w_gate_b, w_up_b, w_down_b, ga2, gpof, l, ts, ts)
        s_states.append((cmp_new.reshape(bs, NSA_KV, 1, 2, HEAD_DIM), sel_new.reshape(bs, NSA_KV, 1, 2, HEAD_DIM),
                         win_out.reshape(bs, NSA_KV, wb, 2, HEAD_DIM), s_fin, conv_fin))

    p_st = [jnp.stack(s, axis=1) for s in zip(*p_states)]
    s_st = [jnp.stack(s, axis=1) for s in zip(*s_states)]
    return (xp.reshape(bp, t, d), xs[:bs].reshape(bs, 1, d), *p_st, *s_st)
```

```python
import functools
import math

import jax
import jax.numpy as jnp
from jax import lax
from jax.experimental import pallas as pl
from jax.experimental.pallas import tpu as pltpu

F32 = jnp.float32
BF16 = jnp.bfloat16

HEAD_DIM = 128
NSA_HEADS = 8
NSA_KV = 2
NSA_HPG = NSA_HEADS // NSA_KV
GDN_HEADS = 8
ROT_DIM = HEAD_DIM // 4
ROPE_THETA = 500000.0
BLOCK = 64
TOPN = 16
WINDOW = 512
CMP_HID = 256
CONV_W = 4
GDN_CHUNK = 64
EPS = 1e-6
KV_ROW = 2 * HEAD_DIM
BLK_ROWS = 2 * BLOCK
BLK_PITCH = BLK_ROWS + 8
NSA_Q_W = NSA_HEADS * HEAD_DIM
GDN_W = GDN_HEADS * HEAD_DIM
CONV_CH = 3 * GDN_W
SMALL_W = 2 * HEAD_DIM
A_COL = 12
B_COL = 20
OFF_CONV = 0
OFF_Z = CONV_CH
OFF_Q = OFF_Z + GDN_W
OFF_CMP = OFF_Q + NSA_Q_W
OFF_SEL = OFF_CMP + NSA_KV * KV_ROW
OFF_WIN = OFF_SEL + NSA_KV * KV_ROW
MAIN_W = OFF_WIN + NSA_KV * KV_ROW
VMEM_LIMIT = 48 * 1024 * 1024
NEG_INF = float("-inf")
SCALE = HEAD_DIM ** -0.5


def _cparams(sem):
    return pltpu.CompilerParams(dimension_semantics=sem, vmem_limit_bytes=VMEM_LIMIT)


def _dot(a, b):
    return jnp.dot(a.astype(BF16), b.astype(BF16), preferred_element_type=F32)


def _dot_nt(a, b):
    return lax.dot_general(a.astype(BF16), b.astype(BF16), (((1,), (1,)), ((), ())),
                           preferred_element_type=F32)


def _dot_exact(a, b):
    return jnp.dot(a, b, preferred_element_type=F32, precision=lax.Precision.HIGHEST)


def _dot_nt_exact(a, b):
    return lax.dot_general(a, b, (((1,), (1,)), ((), ())), preferred_element_type=F32,
                           precision=lax.Precision.HIGHEST)


def _sigmoid(x):
    return 1.0 / (1.0 + jnp.exp(-x))


def _silu(x):
    return x * _sigmoid(x)


def _rms(x, g):
    return x * lax.rsqrt(jnp.mean(x * x, axis=-1, keepdims=True) + EPS) * g


def _masked_softmax(s, mask):
    s = jnp.where(mask, s, NEG_INF)
    m = jnp.max(s, axis=-1, keepdims=True)
    m = jnp.where(m == NEG_INF, 0.0, m)
    e = jnp.where(mask, jnp.exp(s - m), 0.0)
    return e / jnp.maximum(jnp.sum(e, axis=-1, keepdims=True), 1e-30)


def _ada_kernel(c_ref, w_ref, b_ref, o_ref):
    o_ref[0] = _dot(_silu(c_ref[...]), w_ref[0]) + b_ref[0]


def _ada_mod(c_all, w_ada, b_ada, tn=1024):
    depth, d, n = w_ada.shape
    r = c_all.shape[0]
    return pl.pallas_call(
        _ada_kernel,
        out_shape=jax.ShapeDtypeStruct((depth, r, n), F32),
        grid=(depth, n // tn),
        in_specs=[pl.BlockSpec((r, d), lambda l, j: (0, 0)),
                  pl.BlockSpec((1, d, tn), lambda l, j: (l, 0, j)),
                  pl.BlockSpec((1, 1, tn), lambda l, j: (l, 0, j))],
        out_specs=pl.BlockSpec((1, r, tn), lambda l, j: (l, 0, j)),
        compiler_params=_cparams(("arbitrary", "arbitrary")),
        name="ada_mod",
    )(c_all, w_ada, b_ada.reshape(depth, 1, n))


def _inproj_kernel(x_ref, g_ref, sc_ref, sh_ref, w_ref, ws_ref, o_ref, os_ref, h_ref):
    @pl.when(pl.program_id(1) == 0)
    def _():
        h = _rms(x_ref[...], g_ref[...]) * (1.0 + sc_ref[0]) + sh_ref[0]
        hb = h.astype(BF16)
        h_ref[...] = hb
        os_ref[...] = jnp.dot(hb, ws_ref[0], preferred_element_type=F32)

    o_ref[...] = jnp.dot(h_ref[...], w_ref[0], preferred_element_type=F32)


def _mod_spec(mod, rows_per_mod, tm):
    d = mod.shape[-1]
    if mod.shape[1] == 1:
        return pl.BlockSpec((1, 1, d), lambda i, *_: ((i * tm) // rows_per_mod, 0, 0))
    return pl.BlockSpec((1, tm, d), lambda i, *_: (0, i, 0))


def _in_proj(x, g, sc, sh, w_main, w_small, l, rows_per_mod, tm, tn=MAIN_W // 4):
    m, d = x.shape
    return pl.pallas_call(
        _inproj_kernel,
        out_shape=(jax.ShapeDtypeStruct((m, MAIN_W), F32), jax.ShapeDtypeStruct((m, SMALL_W), F32)),
        grid=(m // tm, MAIN_W // tn),
        in_specs=[pl.BlockSpec((tm, d), lambda i, j: (i, 0)),
                  pl.BlockSpec((1, d), lambda i, j: (0, 0)),
                  _mod_spec(sc, rows_per_mod, tm), _mod_spec(sh, rows_per_mod, tm),
                  pl.BlockSpec((1, d, tn), lambda i, j: (l, 0, j)),
                  pl.BlockSpec((1, d, SMALL_W), lambda i, j: (l, 0, 0))],
        out_specs=(pl.BlockSpec((tm, tn), lambda i, j: (i, j)),
                   pl.BlockSpec((tm, SMALL_W), lambda i, j: (i, 0))),
        scratch_shapes=[pltpu.VMEM((tm, d), BF16)],
        compiler_params=_cparams(("arbitrary", "arbitrary")),
        name="in_proj",
    )(x, g, sc, sh, w_main, w_small)


def _rope(x, c, s1, s2):
    half = ROT_DIM // 2
    return x * c + pltpu.roll(x, HEAD_DIM - half, 1) * s1 + pltpu.roll(x, half, 1) * s2


ATT_W = 5 * HEAD_DIM


def _postproj_kernel(q_ref, cmp_ref, sel_ref, win_ref, c_ref, s1_ref, s2_ref,
                     qo_ref, cmpo_ref, selo_ref, wino_ref, att_ref):
    c, s1, s2 = c_ref[...], s1_ref[...], s2_ref[...]
    for h in range(NSA_HEADS):
        sl = slice(h * HEAD_DIM, (h + 1) * HEAD_DIM)
        qo_ref[:, sl] = _rope(q_ref[:, sl], c, s1, s2).astype(BF16)
    tt = c.shape[0]
    blk = (pl.program_id(1) * tt + lax.broadcasted_iota(jnp.int32, (tt, HEAD_DIM), 0)) // BLOCK
    onehot = jnp.where(lax.broadcasted_iota(jnp.int32, (tt, HEAD_DIM), 1) == blk, 1.0, 0.0).astype(BF16)
    for src, dst, col in ((cmp_ref, cmpo_ref, None), (sel_ref, selo_ref, 0), (win_ref, wino_ref, 3 * HEAD_DIM)):
        for g in range(NSA_KV):
            base = g * KV_ROW
            k = _rope(src[:, base:base + HEAD_DIM], c, s1, s2)
            v = src[:, base + HEAD_DIM:base + KV_ROW]
            dst[0, g, pl.ds(0, tt, stride=2), :] = k
            dst[0, g, pl.ds(1, tt, stride=2), :] = v
            if col == 0:
                att_ref[0, g, :, 0:HEAD_DIM] = k.astype(BF16)
                att_ref[0, g, :, HEAD_DIM:2 * HEAD_DIM] = onehot
                att_ref[0, g, :, 2 * HEAD_DIM:3 * HEAD_DIM] = v.astype(BF16)
            elif col is not None:
                att_ref[0, g, :, col:col + HEAD_DIM] = k.astype(BF16)
                att_ref[0, g, :, col + HEAD_DIM:col + 2 * HEAD_DIM] = v.astype(BF16)


def _post_proj(proj, rope_tabs, b, t, tt):
    nt = t // tt
    state = jax.ShapeDtypeStruct((b, NSA_KV, 2 * t, HEAD_DIM), F32)
    att = jax.ShapeDtypeStruct((b, NSA_KV, t, ATT_W), BF16)
    kvw = NSA_KV * KV_ROW
    tab_spec = pl.BlockSpec((tt, HEAD_DIM), lambda bi, ti: (ti, 0))
    st_spec = pl.BlockSpec((1, NSA_KV, 2 * tt, HEAD_DIM), lambda bi, ti: (bi, 0, ti, 0))
    att_spec = pl.BlockSpec((1, NSA_KV, tt, ATT_W), lambda bi, ti: (bi, 0, ti, 0))
    return pl.pallas_call(
        _postproj_kernel,
        out_shape=(jax.ShapeDtypeStruct((b * t, NSA_Q_W), BF16), state, state, state, att),
        grid=(b, nt),
        in_specs=[pl.BlockSpec((tt, NSA_Q_W), lambda bi, ti: (bi * nt + ti, OFF_Q // NSA_Q_W)),
                  pl.BlockSpec((tt, kvw), lambda bi, ti: (bi * nt + ti, OFF_CMP // kvw)),
                  pl.BlockSpec((tt, kvw), lambda bi, ti: (bi * nt + ti, OFF_SEL // kvw)),
                  pl.BlockSpec((tt, kvw), lambda bi, ti: (bi * nt + ti, OFF_WIN // kvw)),
                  tab_spec, tab_spec, tab_spec],
        out_specs=(pl.BlockSpec((tt, NSA_Q_W), lambda bi, ti: (bi * nt + ti, 0)),
                   st_spec, st_spec, st_spec, att_spec),
        compiler_params=_cparams(("arbitrary", "arbitrary")),
        name="post_proj",
    )(proj, proj, proj, proj, *rope_tabs)


def _rope_tables(pos):
    half = ROT_DIM // 2
    inv = ROPE_THETA ** (-jnp.arange(half, dtype=F32) * 2.0 / ROT_DIM)
    ang = pos.astype(F32)[:, None] * inv
    cos, sin = jnp.cos(ang), jnp.sin(ang)
    n = pos.shape[0]
    rest = HEAD_DIM - ROT_DIM
    c = jnp.concatenate([cos, cos, jnp.ones((n, rest), F32)], axis=1)
    s1 = jnp.concatenate([-sin, jnp.zeros((n, HEAD_DIM - half), F32)], axis=1)
    s2 = jnp.concatenate([jnp.zeros((n, half), F32), sin, jnp.zeros((n, rest), F32)], axis=1)
    return c, s1, s2


def _gelu_tanh(x):
    return 0.5 * x * (1.0 + jnp.tanh(math.sqrt(2.0 / math.pi) * (x + 0.044715 * (x * x * x))))


def _compress_blocks(x_ref, base, pitch, nblk, pe_ref, w1k_ref, w1v_ref, w2k_ref, w2v_ref, ko_ref, vo_ref):
    hk = jnp.zeros((nblk, CMP_HID), F32)
    hv = jnp.zeros((nblk, CMP_HID), F32)
    for r in range(BLOCK):
        xk = x_ref[pl.ds(base + 2 * r, nblk, stride=pitch), :] + pe_ref[0, r:r + 1, :]
        xv = x_ref[pl.ds(base + 2 * r + 1, nblk, stride=pitch), :] + pe_ref[1, r:r + 1, :]
        hk = hk + _dot(xk, w1k_ref[r])
        hv = hv + _dot(xv, w1v_ref[r])
    ko_ref[...] = _dot(_gelu_tanh(hk), w2k_ref[...])
    vo_ref[...] = _dot(_gelu_tanh(hv), w2v_ref[...])


def _compress_kernel(x_ref, *rest):
    _compress_blocks(x_ref, 0, BLK_ROWS, x_ref.shape[0] // BLK_ROWS, *rest)


def _compress_weights(cmp_pe, cmp_w1, cmp_w2, l):
    w1 = cmp_w1[l].astype(BF16).reshape(2, BLOCK, HEAD_DIM, CMP_HID)
    w2 = cmp_w2[l].astype(BF16)
    return cmp_pe[l], w1[0], w1[1], w2[0], w2[1]


def _cw_specs(nargs):
    z = (0,) * 3
    return [pl.BlockSpec((2, BLOCK, HEAD_DIM), lambda *a: z),
            pl.BlockSpec((BLOCK, HEAD_DIM, CMP_HID), lambda *a: z),
            pl.BlockSpec((BLOCK, HEAD_DIM, CMP_HID), lambda *a: z),
            pl.BlockSpec((CMP_HID, HEAD_DIM), lambda *a: (0, 0)),
            pl.BlockSpec((CMP_HID, HEAD_DIM), lambda *a: (0, 0))]


def _compress(x_rows, cw, tr):
    nblocks = x_rows.shape[0] // BLK_ROWS
    out = jax.ShapeDtypeStruct((nblocks, HEAD_DIM), F32)
    return pl.pallas_call(
        _compress_kernel,
        out_shape=(out, out),
        grid=(nblocks // tr,),
        in_specs=[pl.BlockSpec((tr * BLK_ROWS, HEAD_DIM), lambda i: (i, 0))] + _cw_specs(1),
        out_specs=(pl.BlockSpec((tr, HEAD_DIM), lambda i: (i, 0)),) * 2,
        compiler_params=_cparams(("arbitrary",)),
        name="compress",
    )(x_rows, *cw)


MAX_PAGES_PER_STEP = 32


def _compress_paged_kernel(pt_ref, cache_ref, pe_ref, w1k_ref, w1v_ref, w2k_ref, w2v_ref, ko_ref, vo_ref,
                           xbuf, sem, *, l):
    bpp = cache_ref.shape[2]
    nblk = ko_ref.shape[0]
    pages_per_step = nblk // bpp
    ns = pl.num_programs(1)
    step = pl.program_id(0) * ns + pl.program_id(1)
    total = pl.num_programs(0) * ns

    def copies(st, slot):
        bi, s = st // ns, st % ns
        out = []
        for k in range(pages_per_step):
            pg = pt_ref[bi, s * pages_per_step + k]
            for j in range(bpp):
                row0 = pl.multiple_of(slot * (nblk * BLK_PITCH) + (k * bpp + j) * BLK_PITCH, 8)
                out.append(pltpu.make_async_copy(cache_ref.at[pg, l, j], xbuf.at[pl.ds(row0, BLK_ROWS), :],
                                                 sem.at[slot]))
        return out

    slot = step % 2

    @pl.when(step == 0)
    def _():
        for c in copies(step, slot):
            c.start()

    @pl.when(step + 1 < total)
    def _():
        for c in copies(step + 1, 1 - slot):
            c.start()

    for c in copies(step, slot):
        c.wait()
    _compress_blocks(xbuf, slot * (nblk * BLK_PITCH), BLK_PITCH, nblk,
                     pe_ref, w1k_ref, w1v_ref, w2k_ref, w2v_ref, ko_ref, vo_ref)


def _compress_paged(cache, page_table, cw, l):
    b, n_pages = page_table.shape
    bpp = cache.shape[2]
    pages_per_step = math.gcd(n_pages, MAX_PAGES_PER_STEP)
    steps = n_pages // pages_per_step
    tr = pages_per_step * bpp
    out = jax.ShapeDtypeStruct((b * steps * tr, HEAD_DIM), F32)
    grid_spec = pltpu.PrefetchScalarGridSpec(
        num_scalar_prefetch=1,
        grid=(b, steps),
        in_specs=[pl.BlockSpec(memory_space=pl.ANY)] + _cw_specs(3),
        out_specs=(pl.BlockSpec((tr, HEAD_DIM), lambda bi, s, pt: (bi * steps + s, 0)),) * 2,
        scratch_shapes=[pltpu.VMEM((2 * tr * BLK_PITCH, HEAD_DIM), F32), pltpu.SemaphoreType.DMA((2,))],
    )
    return pl.pallas_call(
        functools.partial(_compress_paged_kernel, l=l),
        out_shape=(out, out),
        grid_spec=grid_spec,
        compiler_params=_cparams(("arbitrary", "arbitrary")),
        name="compress_paged",
    )(page_table, cache, *cw)


def _stable_topn_masks(scores, n_sel):
    rows, nb = scores[0].shape
    tn = (((0,), (0,)), ((), ()))
    eye_r = jnp.where(lax.broadcasted_iota(jnp.int32, (rows, rows), 0)
                      == lax.broadcasted_iota(jnp.int32, (rows, rows), 1), 1.0, 0.0)
    sts = [lax.dot_general(sc, eye_r, tn, preferred_element_type=F32, precision=lax.Precision.HIGHEST)
           for sc in scores]
    jrow = lax.broadcasted_iota(jnp.int32, (nb, rows), 0)
    ranks = [jnp.zeros((nb, rows), F32) for _ in scores]
    for k in range(nb):
        for p, st in enumerate(sts):
            ck = st[k:k + 1, :]
            beats = (ck > st) | ((ck == st) & (jrow > k))
            ranks[p] = ranks[p] + jnp.where(beats, 1.0, 0.0)
    eye_n = jnp.where(lax.broadcasted_iota(jnp.int32, (nb, nb), 0)
                      == lax.broadcasted_iota(jnp.int32, (nb, nb), 1), 1.0, 0.0).astype(BF16)
    return [lax.dot_general(jnp.where(r < n_sel, 1.0, 0.0).astype(BF16), eye_n, tn, preferred_element_type=F32)
            for r in ranks]


QBLOCKS_PER_STEP = 4
MASK_BIAS = -(2.0 ** 100)
SCORE_C = SCALE * 1.4426950408889634


def _softmax_pv(problems):
    ms = []
    for parts in problems:
        m = jnp.max(parts[0][0], axis=-1, keepdims=True)
        for s, _ in parts[1:]:
            m = jnp.maximum(m, jnp.max(s, axis=-1, keepdims=True))
        ms.append(m)
    dens = [0.0 for _ in problems]
    accs = [0.0 for _ in problems]
    for j in range(len(problems[0])):
        for p, parts in enumerate(problems):
            s, v = parts[j]
            e = jnp.exp2(s - ms[p])
            dens[p] = dens[p] + jnp.sum(e, axis=-1, keepdims=True)
            accs[p] = accs[p] + jnp.dot(e.astype(BF16), v, preferred_element_type=F32)
    return [a / d for a, d in zip(accs, dens)]


def _nsa_prompt_kernel(q_ref, sm_ref, kcb_ref, vcb_ref, att_ref, o_ref, os_ref, *, n_sel, wl, n_cls):
    probs = [(half, g) for half in range(QBLOCKS_PER_STEP) for g in range(NSA_KV)]
    np_ = len(probs)
    t_all = att_ref.shape[2]
    nb = kcb_ref.shape[2]
    rows = NSA_HPG * BLOCK
    nt = (((1,), (1,)), ((), ()))
    q = q_ref[...]
    qs = [jnp.concatenate([q[half * BLOCK:(half + 1) * BLOCK, (g * NSA_HPG + h) * HEAD_DIM:(g * NSA_HPG + h + 1) * HEAD_DIM]
                           for h in range(NSA_HPG)], axis=0) for half, g in probs]
    ib = [pl.program_id(1) * QBLOCKS_PER_STEP + half for half, _ in probs]
    q0 = [i * BLOCK for i in ib]
    tq = lax.broadcasted_iota(jnp.int32, (rows, 1), 0) & (BLOCK - 1)
    qpos = [a + tq for a in q0]

    start = [pl.multiple_of(jnp.clip(a - WINDOW, 0, t_all - wl), BLOCK) for a in q0]
    wiota = lax.broadcasted_iota(jnp.int32, (rows, wl), 1)
    wbias = [jnp.where((start[p] + wiota <= qpos[p]) & (start[p] + wiota >= qpos[p] - WINDOW), 0.0, MASK_BIAS)
             for p in range(np_)]
    s_w = [lax.dot_general(qs[p], att_ref[0, probs[p][1], pl.ds(start[p], wl), 3 * HEAD_DIM:4 * HEAD_DIM], nt,
                           preferred_element_type=F32) * SCORE_C + wbias[p] for p in range(np_)]
    o_w = _softmax_pv([[(s_w[p], att_ref[0, probs[p][1], pl.ds(start[p], wl), 4 * HEAD_DIM:5 * HEAD_DIM])]
                       for p in range(np_)])

    jc = lax.broadcasted_iota(jnp.int32, (rows, nb), 1)
    cmask = [(jc * BLOCK + (BLOCK - 1)) <= qp for qp in qpos]
    s_c = [jnp.where(cmask[p], _dot_nt(qs[p], kcb_ref[0, probs[p][1]]) * SCALE, NEG_INF) for p in range(np_)]
    m_c = [jnp.max(s, axis=-1, keepdims=True) for s in s_c]
    m_c = [jnp.where(m == NEG_INF, 0.0, m) for m in m_c]
    e_c = [jnp.where(cm, jnp.exp(s - m), 0.0) for cm, s, m in zip(cmask, s_c, m_c)]
    p_c = [e / jnp.maximum(jnp.sum(e, axis=-1, keepdims=True), 1e-30) for e in e_c]
    o_c = [_dot(p_c[p], vcb_ref[0, probs[p][1]]) for p in range(np_)]

    jb = lax.broadcasted_iota(jnp.int32, (BLOCK, nb), 1)
    scores = []
    for p in range(np_):
        imp = p_c[p][0:BLOCK]
        for h in range(1, NSA_HPG):
            imp = imp + p_c[p][h * BLOCK:(h + 1) * BLOCK]
        forced = (jb == 0) | (jb == ib[p]) | (jb == ib[p] - 1)
        scores.append(jnp.where(forced, 2.0 * NSA_HPG, jnp.where(jb > ib[p], -1.0, imp)))
    sel = _stable_topn_masks(scores, n_sel)
    tl = lax.broadcasted_iota(jnp.int32, (rows, BLOCK), 1)
    dbias = jnp.where(tl <= tq, 0.0, MASK_BIAS)
    q_aug, s_d, v_d = [], [], []
    for p, (half, g) in enumerate(probs):
        r0 = pl.multiple_of(q0[p], BLOCK)
        blk_bias = jnp.where((jb < ib[p]) & (sel[p] > 0.5), 0.0, MASK_BIAS)
        blk_bias = jnp.concatenate([blk_bias, jnp.zeros((BLOCK, HEAD_DIM - nb), F32)], axis=1).astype(BF16)
        q_aug.append(jnp.concatenate([qs[p], jnp.concatenate([blk_bias] * NSA_HPG, axis=0)], axis=1))
        s_d.append(lax.dot_general(qs[p], att_ref[0, g, pl.ds(r0, BLOCK), 0:HEAD_DIM], nt,
                                   preferred_element_type=F32) * SCORE_C + dbias)
        v_d.append(att_ref[0, g, pl.ds(r0, BLOCK), 2 * HEAD_DIM:3 * HEAD_DIM])
    per_cls = nb // n_cls
    for c in range(n_cls):
        nk = (c + 1) * per_cls * BLOCK

        @pl.when((ib[0] >= c * per_cls) & (ib[0] < (c + 1) * per_cls))
        def _():
            s_b = [lax.dot_general(q_aug[p], att_ref[0, probs[p][1], 0:nk, 0:2 * HEAD_DIM], nt,
                                   preferred_element_type=F32) * SCORE_C for p in range(np_)]
            o = _softmax_pv([[(s_b[p], att_ref[0, probs[p][1], 0:nk, 2 * HEAD_DIM:3 * HEAD_DIM]), (s_d[p], v_d[p])]
                             for p in range(np_)])
            for p in range(np_):
                os_ref[p] = o[p]

    for p, (half, g) in enumerate(probs):
        o_s = os_ref[p]
        gt = _sigmoid(sm_ref[half * BLOCK:(half + 1) * BLOCK, g * HEAD_DIM:(g + 1) * HEAD_DIM])
        for h in range(NSA_HPG):
            r = slice(h * BLOCK, (h + 1) * BLOCK)
            o = (gt[:, 3 * h:3 * h + 1] * o_c[p][r] + gt[:, 3 * h + 1:3 * h + 2] * o_s[r]
                 + gt[:, 3 * h + 2:3 * h + 3] * o_w[p][r])
            c0 = (g * NSA_HPG + h) * HEAD_DIM
            o_ref[half * BLOCK:(half + 1) * BLOCK, c0:c0 + HEAD_DIM] = o.astype(BF16)


def _nsa_prompt(q_rot, small, kcb, vcb, att, b, t):
    tq = QBLOCKS_PER_STEP * BLOCK
    nq = t // tq
    nb = kcb.shape[2]
    assert nb <= HEAD_DIM and nb % QBLOCKS_PER_STEP == 0
    wl = min(WINDOW + BLOCK, t)
    cb_spec = pl.BlockSpec((1, NSA_KV, nb, HEAD_DIM), lambda bi, i: (bi, 0, 0, 0))
    n_cls = math.gcd(nb // QBLOCKS_PER_STEP, 8)
    return pl.pallas_call(
        functools.partial(_nsa_prompt_kernel, n_sel=min(TOPN, nb), wl=wl, n_cls=n_cls),
        out_shape=jax.ShapeDtypeStruct((b * t, NSA_Q_W), BF16),
        grid=(b, nq),
        in_specs=[pl.BlockSpec((tq, NSA_Q_W), lambda bi, i: (bi * nq + i, 0)),
                  pl.BlockSpec((tq, SMALL_W), lambda bi, i: (bi * nq + i, 0)),
                  cb_spec, cb_spec,
                  pl.BlockSpec((1, NSA_KV, t, ATT_W), lambda bi, i: (bi, 0, 0, 0))],
        out_specs=pl.BlockSpec((tq, NSA_Q_W), lambda bi, i: (bi * nq + i, 0)),
        scratch_shapes=[pltpu.VMEM((QBLOCKS_PER_STEP * NSA_KV, NSA_HPG * BLOCK, HEAD_DIM), F32)],
        compiler_params=_cparams(("arbitrary", "arbitrary")),
        name="nsa_prompt",
    )(q_rot, small, kcb, vcb, att)


GDN_SOLVE_BLOCKS = 4


def _softplus(x):
    return jnp.maximum(x, 0.0) + jnp.log(1.0 + jnp.exp(-jnp.abs(x)))


def _unit_lower_solve(l_strict, rhs):
    n = l_strict[0].shape[0]
    sub = 8
    nt = n // sub
    tiles = [[r[t * sub:(t + 1) * sub, :] for t in range(nt)] for r in rhs]
    for i in range(n - 1):
        t0, s = divmod(i, sub)
        for p, lm in enumerate(l_strict):
            xi = tiles[p][t0][s:s + 1, :]
            col = lm[:, i:i + 1]
            for t in range(t0, nt):
                tiles[p][t] = tiles[p][t] - col[t * sub:(t + 1) * sub, :] * xi
    return [jnp.concatenate(tp, axis=0) for tp in tiles]


def _gdn_prompt_kernel(x_ref, z_ref, sm_ref, abr_ref, cw_ref, prow_ref, pcol_ref, nw_ref,
                       o_ref, s_out_ref, conv_out_ref, s_ref, xe_ref):
    n = pl.program_id(1)
    c = GDN_CHUNK
    pad = 8

    @pl.when(n == 0)
    def _():
        s_ref[...] = jnp.zeros_like(s_ref)
        xe_ref[0:pad, :] = jnp.zeros((pad, CONV_CH), F32)

    xe_ref[pad:pad + c, :] = x_ref[...]
    y = xe_ref[pl.ds(pad, c), :] * cw_ref[CONV_W - 1:CONV_W, :]
    for w in range(CONV_W - 1):
        y = y + xe_ref[pl.ds(pad - (CONV_W - 1) + w, c), :] * cw_ref[w:w + 1, :]
    y = _silu(y)

    ri = lax.broadcasted_iota(jnp.int32, (c, c), 0)
    ci = lax.broadcasted_iota(jnp.int32, (c, c), 1)
    tri = ri >= ci
    strict = ri > ci
    tri_f = jnp.where(tri, 1.0, 0.0)

    a_col = sm_ref[:, A_COL:A_COL + GDN_HEADS]
    b_col = sm_ref[:, B_COL:B_COL + GDN_HEADS]
    g_col = -jnp.exp(prow_ref[0:1, :]) * _softplus(a_col + prow_ref[1:2, :])
    g_row = -jnp.exp(pcol_ref[:, 0:1]) * _softplus(abr_ref[0, 0, 0:GDN_HEADS, :] + pcol_ref[:, 1:2])
    gc_col = _dot_exact(tri_f, g_col)
    gc_row = _dot_nt_exact(g_row, tri_f)
    beta_col = _sigmoid(b_col)

    heads = range(GDN_HEADS)
    yq = [y[:, h * HEAD_DIM:(h + 1) * HEAD_DIM] for h in heads]
    yk = [y[:, GDN_W + h * HEAD_DIM:GDN_W + (h + 1) * HEAD_DIM] for h in heads]
    yv = [y[:, 2 * GDN_W + h * HEAD_DIM:2 * GDN_W + (h + 1) * HEAD_DIM] for h in heads]
    qn = [jnp.sum(a * a, axis=-1, keepdims=True) for a in yq]
    kn = [jnp.sum(a * a, axis=-1, keepdims=True) for a in yk]
    qh = [a * lax.rsqrt(n + EPS) * SCALE for a, n in zip(yq, qn)]
    kh = [a * lax.rsqrt(n + EPS) for a, n in zip(yk, kn)]
    gccs = [gc_col[:, h:h + 1] for h in heads]
    betas = [beta_col[:, h:h + 1] for h in heads]
    decays = [jnp.exp(jnp.where(tri, gccs[h] - gc_row[h:h + 1, :], NEG_INF)) for h in heads]
    kbs = [k * b for k, b in zip(kh, betas)]
    kks = [_dot_nt(kb, k) for kb, k in zip(kbs, kh)]
    qks = [_dot_nt(q, k) for q, k in zip(qh, kh)]
    lmats = [jnp.where(strict, kk * d, 0.0) for kk, d in zip(kks, decays)]
    qkds = [qk * d for qk, d in zip(qks, decays)]
    rhss = [jnp.concatenate([yv[h] * betas[h], kbs[h] * jnp.exp(gccs[h])], axis=1) for h in heads]
    pre = [(qh[h], kh[h], gccs[h], qkds[h], lmats[h], rhss[h]) for h in heads]

    nblk = GDN_SOLVE_BLOCKS
    rb = c // nblk
    l_list, r_list = [], []
    for p in pre:
        lmat, rhs = p[4], p[5]
        for b in range(nblk):
            r0 = b * rb
            l_list.append(lmat[r0:r0 + rb, r0:r0 + rb])
            if b == 0:
                r_list.append(rhs[0:rb, :])
            else:
                r_list.append(jnp.concatenate([rhs[r0:r0 + rb, :], lmat[r0:r0 + rb, 0:r0],
                                               jnp.zeros((rb, HEAD_DIM - r0), F32)], axis=1))
    part = _unit_lower_solve(l_list, r_list)
    xs = [[part[h * nblk][:, 0:2 * HEAD_DIM]] for h in range(GDN_HEADS)]
    for b in range(1, nblk):
        for h in range(GDN_HEADS):
            y = part[h * nblk + b]
            xs[h].append(y[:, 0:2 * HEAD_DIM]
                         - _dot_exact(y[:, 2 * HEAD_DIM:2 * HEAD_DIM + b * rb], jnp.concatenate(xs[h], axis=0)))
    sols = [jnp.concatenate(x, axis=0) for x in xs]

    tn = (((0,), (0,)), ((), ()))
    s_in = [s_ref[h] for h in heads]
    ws = [_dot(sols[h][:, HEAD_DIM:], s_in[h]) for h in heads]
    v_new = [sols[h][:, :HEAD_DIM] - ws[h] for h in heads]
    qs_ = [_dot(qh[h] * jnp.exp(gccs[h]), s_in[h]) for h in heads]
    os_ = [qs_[h] + _dot(qkds[h], v_new[h]) for h in heads]
    g_last = [g[c - 1:c, :] for g in gccs]
    k_dec = [kh[h] * jnp.exp(g_last[h] - gccs[h]) for h in heads]
    s_out = [s_in[h] * jnp.exp(g_last[h])
             + lax.dot_general(k_dec[h].astype(BF16), v_new[h].astype(BF16), tn, preferred_element_type=F32)
             for h in heads]
    ms_ = [jnp.mean(o * o, axis=-1, keepdims=True) for o in os_]
    o_out = [(os_[h] * lax.rsqrt(ms_[h] + EPS) * nw_ref[...]
              * _silu(z_ref[:, h * HEAD_DIM:(h + 1) * HEAD_DIM])).astype(BF16) for h in heads]
    for h in range(GDN_HEADS):
        s_ref[h] = s_out[h]
        o_ref[:, h * HEAD_DIM:(h + 1) * HEAD_DIM] = o_out[h]
    xe_ref[0:pad, :] = xe_ref[c:c + pad, :]

    @pl.when(n == pl.num_programs(1) - 1)
    def _():
        s_out_ref[0] = s_ref[...]
        conv_out_ref[0] = xe_ref[pad - (CONV_W - 1):pad, :]


def _gdn_prompt(proj, small, conv_w, a_log, dt_bias, norm_w, b, t):
    c = GDN_CHUNK
    nc = t // c
    ab_row = small.reshape(b, nc, c, SMALL_W)[..., A_COL:A_COL + 2 * GDN_HEADS].transpose(0, 1, 3, 2)
    prow = jnp.stack([a_log, dt_bias], axis=0)
    pcol = prow.T
    return pl.pallas_call(
        _gdn_prompt_kernel,
        out_shape=(jax.ShapeDtypeStruct((b * t, GDN_W), BF16),
                   jax.ShapeDtypeStruct((b, GDN_HEADS, HEAD_DIM, HEAD_DIM), F32),
                   jax.ShapeDtypeStruct((b, CONV_W - 1, CONV_CH), F32)),
        grid=(b, nc),
        in_specs=[pl.BlockSpec((c, CONV_CH), lambda bi, n: (bi * nc + n, OFF_CONV // CONV_CH)),
                  pl.BlockSpec((c, GDN_W), lambda bi, n: (bi * nc + n, OFF_Z // GDN_W)),
                  pl.BlockSpec((c, HEAD_DIM), lambda bi, n: (bi * nc + n, 0)),
                  pl.BlockSpec((1, 1, 2 * GDN_HEADS, c), lambda bi, n: (bi, n, 0, 0)),
                  pl.BlockSpec((CONV_W, CONV_CH), lambda bi, n: (0, 0)),
                  pl.BlockSpec((2, GDN_HEADS), lambda bi, n: (0, 0)),
                  pl.BlockSpec((GDN_HEADS, 2), lambda bi, n: (0, 0)),
                  pl.BlockSpec((1, HEAD_DIM), lambda bi, n: (0, 0))],
        out_specs=(pl.BlockSpec((c, GDN_W), lambda bi, n: (bi * nc + n, 0)),
                   pl.BlockSpec((1, GDN_HEADS, HEAD_DIM, HEAD_DIM), lambda bi, n: (bi, 0, 0, 0)),
                   pl.BlockSpec((1, CONV_W - 1, CONV_CH), lambda bi, n: (bi, 0, 0))),
        scratch_shapes=[pltpu.VMEM((GDN_HEADS, HEAD_DIM, HEAD_DIM), F32),
                        pltpu.VMEM((c + 16, CONV_CH), F32)],
        compiler_params=_cparams(("arbitrary", "arbitrary")),
        name="gdn_prompt",
    )(proj, proj, small, ab_row, conv_w, prow, pcol, norm_w.reshape(1, HEAD_DIM))


def _outproj_kernel(on_ref, og_ref, w_ref, x_ref, ga_ref, g_ref, o_ref):
    o = jnp.concatenate([on_ref[...], og_ref[...]], axis=1)
    mix = jnp.dot(o, w_ref[0], preferred_element_type=F32)
    o_ref[...] = x_ref[...] + ga_ref[0] * _rms(mix, g_ref[...])


def _out_proj(o_nsa, o_gdn, w_out, x, ga, g, l, rows_per_mod, tm):
    m, d = x.shape
    kw = o_nsa.shape[1]
    return pl.pallas_call(
        _outproj_kernel,
        out_shape=jax.ShapeDtypeStruct((m, d), F32),
        grid=(m // tm,),
        in_specs=[pl.BlockSpec((tm, kw), lambda i: (i, 0)),
                  pl.BlockSpec((tm, kw), lambda i: (i, 0)),
                  pl.BlockSpec((1, 2 * kw, d), lambda i: (l, 0, 0)),
                  pl.BlockSpec((tm, d), lambda i: (i, 0)),
                  _mod_spec(ga, rows_per_mod, tm),
                  pl.BlockSpec((1, d), lambda i: (0, 0))],
        out_specs=pl.BlockSpec((tm, d), lambda i: (i, 0)),
        compiler_params=_cparams(("arbitrary",)),
        name="out_proj",
    )(o_nsa, o_gdn, w_out, x, ga, g)


def _ffn_kernel(x_ref, g1_ref, sc_ref, sh_ref, wg_ref, wu_ref, wd_ref, ga_ref, g2_ref, o_ref, h_ref, acc_ref):
    j = pl.program_id(1)

    @pl.when(j == 0)
    def _():
        h = _rms(x_ref[...], g1_ref[...]) * (1.0 + sc_ref[0]) + sh_ref[0]
        h_ref[...] = h.astype(BF16)
        acc_ref[...] = jnp.zeros_like(acc_ref)

    h = h_ref[...]
    a = jnp.dot(h, wg_ref[0], preferred_element_type=F32)
    u = jnp.dot(h, wu_ref[0], preferred_element_type=F32)
    acc_ref[...] += _dot(_silu(a) * u, wd_ref[0])

    @pl.when(j == pl.num_programs(1) - 1)
    def _():
        o_ref[...] = x_ref[...] + ga_ref[0] * _rms(acc_ref[...], g2_ref[...])


def _ffn(x, g1, sc, sh, w_gate, w_up, w_down, ga, g2, l, rows_per_mod, tm, tf=512):
    m, d = x.shape
    f = w_gate.shape[2]
    vec = pl.BlockSpec((1, d), lambda i, j: (0, 0))
    return pl.pallas_call(
        _ffn_kernel,
        out_shape=jax.ShapeDtypeStruct((m, d), F32),
        grid=(m // tm, f // tf),
        in_specs=[pl.BlockSpec((tm, d), lambda i, j: (i, 0)), vec,
                  _mod_spec(sc, rows_per_mod, tm), _mod_spec(sh, rows_per_mod, tm),
                  pl.BlockSpec((1, d, tf), lambda i, j: (l, 0, j)),
                  pl.BlockSpec((1, d, tf), lambda i, j: (l, 0, j)),
                  pl.BlockSpec((1, tf, d), lambda i, j: (l, j, 0)),
                  _mod_spec(ga, rows_per_mod, tm), vec],
        out_specs=pl.BlockSpec((tm, d), lambda i, j: (i, 0)),
        scratch_shapes=[pltpu.VMEM((tm, d), BF16), pltpu.VMEM((tm, d), F32)],
        compiler_params=_cparams(("arbitrary", "arbitrary")),
        name="ffn",
    )(x, g1, sc, sh, w_gate, w_up, w_down, ga, g2)


def _stack_heads(q, h0):
    rows = [q[:, (h0 + h) * HEAD_DIM:(h0 + h + 1) * HEAD_DIM] for h in range(NSA_HPG)]
    return jnp.concatenate(rows + [jnp.zeros((NSA_HPG, HEAD_DIM), F32)], axis=0)


def _nsa_sample_cmp_kernel(q_ref, kcb_ref, vcb_ref, oc_ref, idx_ref, *, n_sel, q_pos):
    nseq, nbp = kcb_ref.shape[0], kcb_ref.shape[2]
    cur = q_pos // BLOCK
    imps = []
    for bi in range(nseq):
        q = q_ref[bi].astype(F32)
        for g in range(NSA_KV):
            qs = _stack_heads(q, g * NSA_HPG)
            jc = lax.broadcasted_iota(jnp.int32, (qs.shape[0], nbp), 1)
            cmask = (jc * BLOCK + (BLOCK - 1)) <= q_pos
            p_c = _masked_softmax(_dot_nt(qs, kcb_ref[bi, g]) * SCALE, cmask)
            o_c = _dot(p_c, vcb_ref[bi, g])
            for h in range(NSA_HPG):
                c0 = (g * NSA_HPG + h) * HEAD_DIM
                oc_ref[bi, :, c0:c0 + HEAD_DIM] = o_c[h:h + 1]
            imp = p_c[0:1]
            for h in range(1, NSA_HPG):
                imp = imp + p_c[h:h + 1]
            imps.append(imp)
    imp = jnp.concatenate(imps, axis=0)
    jb = lax.broadcasted_iota(jnp.int32, imp.shape, 1)
    forced = (jb == 0) | (jb == cur) | (jb == cur - 1)
    score = jnp.where(forced, jnp.inf, jnp.where(jb > cur, NEG_INF, imp))
    left = jb >= 0
    for r in range(n_sel):
        best = jnp.max(jnp.where(left, score, NEG_INF), axis=1, keepdims=True)
        pick = jnp.min(jnp.where(left & (score == best), jb, nbp), axis=1, keepdims=True)
        idx_ref[r] = jnp.broadcast_to(pick, (imp.shape[0], HEAD_DIM))
        left = left & (jb != pick)


def _nsa_sample_cmp(q_rot, kcb, vcb, q_pos, n_blocks):
    b = q_rot.shape[0]
    nbp = kcb.shape[2]
    n_sel = min(TOPN, n_blocks)
    o_c, idx = pl.pallas_call(
        functools.partial(_nsa_sample_cmp_kernel, n_sel=n_sel, q_pos=q_pos),
        out_shape=(jax.ShapeDtypeStruct((b, 1, NSA_Q_W), F32),
                   jax.ShapeDtypeStruct((n_sel, b * NSA_KV, HEAD_DIM), jnp.int32)),
        compiler_params=pltpu.CompilerParams(vmem_limit_bytes=VMEM_LIMIT),
        name="nsa_sample_cmp",
    )(q_rot.reshape(b, 1, NSA_Q_W), kcb, vcb)
    return o_c, idx[:, :, 0].T.reshape(b, NSA_KV, n_sel)


def _nsa_sample_attn_kernel(idx_ref, pt_ref, q_ref, sm_ref, oc_ref, cache_ref, newsel_ref, win_ref, newwin_ref,
                            o_ref, wino_ref, blk_buf, sem, *, l, npb, n_sel):
    bi, g = pl.program_id(0), pl.program_id(1)
    bpp = cache_ref.shape[3]
    copies, picks = [], []
    for n in range(n_sel):
        pick = idx_ref[bi, g, n]
        jp = jnp.minimum(pick, npb - 1)
        cp = pltpu.make_async_copy(cache_ref.at[pt_ref[bi, jp // bpp], l, g, jp % bpp],
                                   blk_buf.at[pl.ds(n * BLK_ROWS, BLK_ROWS), :], sem)
        cp.start()
        copies.append(cp)
        picks.append(pick)

    qs = _stack_heads(q_ref[0].astype(F32), 0)

    def row_scores(k):
        return jnp.sum(qs * k.astype(BF16).astype(F32), axis=-1, keepdims=True) * SCALE

    wb = win_ref.shape[3] // 2
    s_buf = _dot_nt(qs, win_ref[0, 0, 0, pl.ds(0, wb, stride=2), :]) * SCALE
    s_new = row_scores(newwin_ref[0, 0, 0:1, :])
    m = jnp.maximum(jnp.max(s_buf, axis=-1, keepdims=True), s_new)
    e_buf, e_new = jnp.exp(s_buf - m), jnp.exp(s_new - m)
    den = jnp.maximum(jnp.sum(e_buf, axis=-1, keepdims=True) + e_new, 1e-30)
    o_w = (_dot(e_buf, win_ref[0, 0, 0, pl.ds(1, wb, stride=2), :])
           + e_new * newwin_ref[0, 0, 1:2, :].astype(BF16).astype(F32)) / den
    wino_ref[0, 0, 0:2 * wb - 2, :] = win_ref[0, 0, 0, 2:2 * wb, :]
    wino_ref[0, 0, 2 * wb - 2:2 * wb, :] = newwin_ref[0, 0]

    for cp in copies:
        cp.wait()
    nk = n_sel * BLOCK
    slot = lax.broadcasted_iota(jnp.int32, (1, nk), 1) // BLOCK
    bias = jnp.zeros((1, nk), F32)
    has_new = picks[0] >= npb
    for n in range(n_sel):
        bias = jnp.where((slot == n) & (picks[n] >= npb), NEG_INF, bias)
        has_new = has_new | (picks[n] >= npb)
    s_blk = _dot_nt(qs, blk_buf[pl.ds(0, nk, stride=2), :]) * SCALE + bias
    s_row = jnp.where(has_new, row_scores(newsel_ref[0, 0, 0:1, :]), NEG_INF)
    m = jnp.maximum(jnp.max(s_blk, axis=-1, keepdims=True), s_row)
    m = jnp.where(m == NEG_INF, 0.0, m)
    e_blk, e_row = jnp.exp(s_blk - m), jnp.exp(s_row - m)
    den = jnp.maximum(jnp.sum(e_blk, axis=-1, keepdims=True) + e_row, 1e-30)
    o_s = (_dot(e_blk, blk_buf[pl.ds(1, nk, stride=2), :])
           + e_row * newsel_ref[0, 0, 1:2, :].astype(BF16).astype(F32)) / den

    gt = _sigmoid(sm_ref[0])
    for h in range(NSA_HPG):
        c0 = h * HEAD_DIM
        o = (gt[:, 3 * h:3 * h + 1] * oc_ref[0, :, c0:c0 + HEAD_DIM] + gt[:, 3 * h + 1:3 * h + 2] * o_s[h:h + 1]
             + gt[:, 3 * h + 2:3 * h + 3] * o_w[h:h + 1])
        o_ref[0, :, c0:c0 + HEAD_DIM] = o.astype(BF16)


def _nsa_sample_attn(idx, page_table, q_rot, small, o_c, cache_sel, new_sel, cache_win, new_win, l, npb):
    b = q_rot.shape[0]
    n_sel = idx.shape[2]
    bpp = cache_sel.shape[3]
    wb2 = cache_win.shape[3]
    gw = NSA_HPG * HEAD_DIM

    new_spec = pl.BlockSpec((1, 1, 2, HEAD_DIM), lambda bi, g, *_: (bi, g, 0, 0))
    grid_spec = pltpu.PrefetchScalarGridSpec(
        num_scalar_prefetch=2,
        grid=(b, NSA_KV),
        in_specs=[pl.BlockSpec((1, 1, gw), lambda bi, g, *_: (bi, 0, g)),
                  pl.BlockSpec((1, 1, HEAD_DIM), lambda bi, g, *_: (bi, 0, g)),
                  pl.BlockSpec((1, 1, gw), lambda bi, g, *_: (bi, 0, g)),
                  pl.BlockSpec(memory_space=pl.ANY),
                  new_spec,
                  pl.BlockSpec((1, 1, 1, wb2, HEAD_DIM), lambda bi, g, *_: (bi, l, g, 0, 0)),
                  new_spec],
        out_specs=(pl.BlockSpec((1, 1, gw), lambda bi, g, *_: (bi, 0, g)),
                   pl.BlockSpec((1, 1, wb2, HEAD_DIM), lambda bi, g, *_: (bi, g, 0, 0))),
        scratch_shapes=[pltpu.VMEM((n_sel * BLK_ROWS, HEAD_DIM), F32), pltpu.SemaphoreType.DMA(())],
    )
    return pl.pallas_call(
        functools.partial(_nsa_sample_attn_kernel, l=l, npb=npb, n_sel=n_sel),
        out_shape=(jax.ShapeDtypeStruct((b, 1, NSA_Q_W), BF16),
                   jax.ShapeDtypeStruct((b, NSA_KV, wb2, HEAD_DIM), F32)),
        grid_spec=grid_spec,
        compiler_params=_cparams(("arbitrary", "arbitrary")),
        name="nsa_sample_attn",
    )(idx, page_table, q_rot.reshape(b, 1, NSA_Q_W), small.reshape(b, 1, SMALL_W), o_c,
      cache_sel, new_sel, cache_win, new_win)


def _gdn_sample_kernel(x_ref, z_ref, sm_ref, buf_ref, s0_ref, cw_ref, prow_ref, nw_ref, eye_ref,
                       o_ref, s_out_ref, conv_out_ref):
    x = x_ref[0]
    buf = buf_ref[0, 0]
    y = x * cw_ref[CONV_W - 1:CONV_W, :]
    for w in range(CONV_W - 1):
        y = y + buf[w:w + 1, :] * cw_ref[w:w + 1, :]
    y = _silu(y)
    conv_out_ref[0, 0:CONV_W - 2, :] = buf[1:CONV_W - 1, :]
    conv_out_ref[0, CONV_W - 2:CONV_W - 1, :] = x

    def heads(off):
        return jnp.concatenate([y[:, off + h * HEAD_DIM:off + (h + 1) * HEAD_DIM] for h in range(GDN_HEADS)], axis=0)

    yq, yk, v = heads(0), heads(GDN_W), heads(2 * GDN_W)
    q = yq * lax.rsqrt(jnp.sum(yq * yq, axis=-1, keepdims=True) + EPS) * SCALE
    k = yk * lax.rsqrt(jnp.sum(yk * yk, axis=-1, keepdims=True) + EPS)
    sm = sm_ref[0]
    g_row = -jnp.exp(prow_ref[0:1, :]) * _softplus(sm[:, A_COL:A_COL + GDN_HEADS] + prow_ref[1:2, :])
    beta_row = _sigmoid(sm[:, B_COL:B_COL + GDN_HEADS])
    eg_row = jnp.exp(g_row)
    k_t = _dot_nt_exact(eye_ref[...], k)
    q_t = _dot_nt_exact(eye_ref[...], q)
    z = z_ref[0]
    for h in range(GDN_HEADS):
        s0 = s0_ref[0, 0, h]
        eg = eg_row[:, h:h + 1]
        kc = k_t[:, h:h + 1].astype(BF16).astype(F32)
        qc = q_t[:, h:h + 1].astype(BF16).astype(F32)
        s0b = s0.astype(BF16).astype(F32)
        ks = jnp.sum(kc * beta_row[:, h:h + 1] * eg * s0b, axis=0, keepdims=True)
        v_new = v[h:h + 1] * beta_row[:, h:h + 1] - ks
        vb = v_new.astype(BF16).astype(F32)
        s1 = s0 * eg + kc * vb
        qe = (q_t[:, h:h + 1] * eg).astype(BF16).astype(F32)
        o = jnp.sum(qe * s0b, axis=0, keepdims=True) + jnp.sum(qc * kc, axis=0, keepdims=True) * vb
        s_out_ref[0, h] = s1
        sl = slice(h * HEAD_DIM, (h + 1) * HEAD_DIM)
        o = _rms(o, nw_ref[...]) * _silu(z[:, sl])
        o_ref[0, :, sl] = o.astype(BF16)


def _gdn_sample(proj, small, state_conv, state_gdn, conv_w, a_log, dt_bias, norm_w, l, b):
    prow = jnp.stack([a_log, dt_bias], axis=0)
    rows = proj.shape[0]
    return pl.pallas_call(
        _gdn_sample_kernel,
        out_shape=(jax.ShapeDtypeStruct((b, 1, GDN_W), BF16),
                   jax.ShapeDtypeStruct((b, GDN_HEADS, HEAD_DIM, HEAD_DIM), F32),
                   jax.ShapeDtypeStruct((b, CONV_W - 1, CONV_CH), F32)),
        grid=(b,),
        in_specs=[pl.BlockSpec((1, 1, CONV_CH), lambda bi: (bi, 0, OFF_CONV // CONV_CH)),
                  pl.BlockSpec((1, 1, GDN_W), lambda bi: (bi, 0, OFF_Z // GDN_W)),
                  pl.BlockSpec((1, 1, SMALL_W), lambda bi: (bi, 0, 0)),
                  pl.BlockSpec((1, 1, CONV_W - 1, CONV_CH), lambda bi: (bi, l, 0, 0)),
                  pl.BlockSpec((1, 1, GDN_HEADS, HEAD_DIM, HEAD_DIM), lambda bi: (bi, l, 0, 0, 0)),
                  pl.BlockSpec((CONV_W, CONV_CH), lambda bi: (0, 0)),
                  pl.BlockSpec((2, GDN_HEADS), lambda bi: (0, 0)),
                  pl.BlockSpec((1, HEAD_DIM), lambda bi: (0, 0)),
                  pl.BlockSpec((HEAD_DIM, HEAD_DIM), lambda bi: (0, 0))],
        out_specs=(pl.BlockSpec((1, 1, GDN_W), lambda bi: (bi, 0, 0)),
                   pl.BlockSpec((1, GDN_HEADS, HEAD_DIM, HEAD_DIM), lambda bi: (bi, 0, 0, 0)),
                   pl.BlockSpec((1, CONV_W - 1, CONV_CH), lambda bi: (bi, 0, 0))),
        compiler_params=_cparams(("arbitrary",)),
        name="gdn_sample",
    )(proj.reshape(rows, 1, MAIN_W), proj.reshape(rows, 1, MAIN_W), small.reshape(rows, 1, SMALL_W),
      state_conv, state_gdn, conv_w, prow, norm_w.reshape(1, HEAD_DIM), jnp.eye(HEAD_DIM, dtype=F32))


def _repack_w_in(w_in):
    kvw = NSA_KV * HEAD_DIM
    o_q = 0
    o_kv = [NSA_Q_W + i * kvw for i in range(6)]
    o_gl = NSA_Q_W + 6 * kvw
    o_conv = o_gl + 3 * NSA_HEADS
    o_a = o_conv + CONV_CH
    o_b = o_a + GDN_HEADS
    o_z = o_b + GDN_HEADS

    def cols(a, n):
        return w_in[:, :, a:a + n]

    pieces = [cols(o_conv, CONV_CH), cols(o_z, GDN_W), cols(o_q, NSA_Q_W)]
    for pair in range(3):
        for g in range(NSA_KV):
            pieces.append(cols(o_kv[2 * pair] + g * HEAD_DIM, HEAD_DIM))
            pieces.append(cols(o_kv[2 * pair + 1] + g * HEAD_DIM, HEAD_DIM))
    main = jnp.concatenate(pieces, axis=-1).astype(BF16)
    depth, d, _ = w_in.shape
    ngl = 3 * NSA_HPG
    small = jnp.concatenate([
        cols(o_gl, ngl), cols(o_a, GDN_HEADS), cols(o_b, GDN_HEADS),
        jnp.zeros((depth, d, HEAD_DIM - ngl - 2 * GDN_HEADS), w_in.dtype),
        cols(o_gl + ngl, ngl), jnp.zeros((depth, d, HEAD_DIM - ngl), w_in.dtype)], axis=-1).astype(BF16)
    return main, small


def _mods(mod_l, lo, hi, broadcast):
    d = mod_l.shape[1] // 6
    out = []
    for k in range(6):
        m = mod_l[lo:hi, k * d:(k + 1) * d]
        out.append(m[:, None, :] if broadcast else m[None])
    return out


def kernel(x_prompt, x_sample, cache_cmp_kv, cache_sel_kv, cache_win_kv, state_gdn, state_conv, page_table,
           c_prompt, c_sample, w_ada, b_ada, g_pre_mix, w_in, cmp_pe, cmp_w1, cmp_w2, conv_w, gdn_a_log,
           gdn_dt_bias, gdn_norm, w_out, g_post_mix, g_pre_ffn, w_gate, w_up, w_down, g_post_ffn):
    bp, t, d = x_prompt.shape
    bs = x_sample.shape[0]
    depth = w_in.shape[0]
    n_pool, _, _, page, _, _ = cache_cmp_kv.shape
    past = page_table.shape[1] * page
    npb = past // BLOCK
    bpp = page // BLOCK
    wb = cache_win_kv.shape[3]
    ts = 16
    assert x_sample.shape[1] == 1 and bs <= ts and t % BLOCK == 0 and page % BLOCK == 0

    c_all = jnp.concatenate([c_sample, jnp.zeros((ts - bs, d), F32), c_prompt,
                             jnp.zeros((-bp % 8, d), F32)], axis=0)
    mod = _ada_mod(c_all, w_ada, b_ada)

    w_main, w_small = _repack_w_in(w_in)
    w_out_b, w_gate_b, w_up_b, w_down_b = (w.astype(BF16) for w in (w_out, w_gate, w_up, w_down))
    rope_p = _rope_tables(jnp.arange(t))
    rope_s = _rope_tables(jnp.full((ts,), past))

    tm = min(512, t)
    xp = x_prompt.reshape(bp * t, d)
    xs = jnp.concatenate([x_sample.reshape(bs, d), jnp.zeros((ts - bs, d), F32)], axis=0)
    cache_cmp_blocks = cache_cmp_kv.reshape(n_pool, depth, NSA_KV * bpp, BLK_ROWS, HEAD_DIM)
    cache_sel_blocks = cache_sel_kv.reshape(n_pool, depth, NSA_KV, bpp, BLK_ROWS, HEAD_DIM)
    cache_win_rows = cache_win_kv.reshape(bs, depth, NSA_KV, 2 * wb, HEAD_DIM)
    nbp = -(-(npb + 1) // HEAD_DIM) * HEAD_DIM

    p_states, s_states = [], []
    for l in range(depth):
        gpm, gpo, gpf, gpof = (g[l][None] for g in (g_pre_mix, g_post_mix, g_pre_ffn, g_post_ffn))
        cw = _compress_weights(cmp_pe, cmp_w1, cmp_w2, l)

        sh1, sc1, ga1, sh2, sc2, ga2 = _mods(mod[l], ts, ts + bp, True)
        proj, small = _in_proj(xp, gpm, sc1, sh1, w_main, w_small, l, t, tm)
        q_rot, cmp_st, sel_st, win_st, att = _post_proj(proj, rope_p, bp, t, min(256, t))
        nb = t // BLOCK
        kcb, vcb = _compress(cmp_st.reshape(bp * NSA_KV * 2 * t, HEAD_DIM), cw, math.gcd(bp * NSA_KV * nb, 128))
        kcb, vcb = (a.reshape(bp, NSA_KV, nb, HEAD_DIM) for a in (kcb, vcb))
        o_nsa = _nsa_prompt(q_rot, small, kcb, vcb, att, bp, t)
        o_gdn, s_fin, conv_fin = _gdn_prompt(proj, small, conv_w[l], gdn_a_log[l], gdn_dt_bias[l], gdn_norm[l], bp, t)
        xp = _out_proj(o_nsa, o_gdn, w_out_b, xp, ga1, gpo, l, t, tm)
        xp = _ffn(xp, gpf, sc2, sh2, w_gate_b, w_up_b, w_down_b, ga2, gpof, l, t, tm)
        wl = min(WINDOW, t)
        p_states.append((cmp_st.reshape(bp, NSA_KV, t, 2, HEAD_DIM), sel_st.reshape(bp, NSA_KV, t, 2, HEAD_DIM),
                         win_st[:, :, 2 * (t - wl):].reshape(bp, NSA_KV, wl, 2, HEAD_DIM), s_fin, conv_fin))

        sh1, sc1, ga1, sh2, sc2, ga2 = _mods(mod[l], 0, ts, False)
        proj, small = _in_proj(xs, gpm, sc1, sh1, w_main, w_small, l, ts, ts)
        q_rot, cmp_new, sel_new, win_new, _ = _post_proj(proj, rope_s, 1, ts, ts)
        cmp_new, sel_new, win_new = (a.reshape(NSA_KV, ts, 2, HEAD_DIM)[:, :bs].transpose(1, 0, 2, 3)
                                     for a in (cmp_new, sel_new, win_new))
        kc_past, vc_past = _compress_paged(cache_cmp_blocks, page_table, cw, l)
        new_blk = jnp.pad(cmp_new.reshape(bs * NSA_KV, 2, HEAD_DIM), ((0, 0), (0, BLK_ROWS - 2), (0, 0)))
        kc_new, vc_new = _compress(new_blk.reshape(bs * NSA_KV * BLK_ROWS, HEAD_DIM), cw, bs * NSA_KV)

        def summaries(past_rows, new_rows):
            a = past_rows.reshape(bs, past // page, NSA_KV, bpp, HEAD_DIM).transpose(0, 2, 1, 3, 4)
            a = a.reshape(bs, NSA_KV, npb, HEAD_DIM)
            a = jnp.concatenate([a, new_rows.reshape(bs, NSA_KV, 1, HEAD_DIM)], axis=2)
            return jnp.pad(a, ((0, 0), (0, 0), (0, nbp - npb - 1), (0, 0)))

        kcb, vcb = summaries(kc_past, kc_new), summaries(vc_past, vc_new)
        o_c, idx = _nsa_sample_cmp(q_rot[:bs], kcb, vcb, past, npb + 1)
        o_nsa, win_out = _nsa_sample_attn(idx, page_table, q_rot[:bs], small[:bs], o_c, cache_sel_blocks,
                                          sel_new, cache_win_rows, win_new, l, npb)
        o_gdn, s_fin, conv_fin = _gdn_sample(proj, small, state_conv, state_gdn, conv_w[l], gdn_a_log[l],
                                             gdn_dt_bias[l], gdn_norm[l], l, bs)
        pad_rows = ((0, ts - bs), (0, 0))
        o_nsa = jnp.pad(o_nsa.reshape(bs, NSA_Q_W), pad_rows)
        o_gdn = jnp.pad(o_gdn.reshape(bs, GDN_W), pad_rows)
        xs = _out_proj(o_nsa, o_gdn, w_out_b, xs, ga1, gpo, l, ts, ts)
        xs = _ffn(xs, gpf, sc2, sh2, w_gate_b, w_up_b, w_down_b, ga2, gpof, l, ts, ts)
        s_states.append((cmp_new.reshape(bs, NSA_KV, 1, 2, HEAD_DIM), sel_new.reshape(bs, NSA_KV, 1, 2, HEAD_DIM),
                         win_out.reshape(bs, NSA_KV, wb, 2, HEAD_DIM), s_fin, conv_fin))

    p_st = [jnp.stack(s, axis=1) for s in zip(*p_states)]
    s_st = [jnp.stack(s, axis=1) for s in zip(*s_states)]
    return (xp.reshape(bp, t, d), xs[:bs].reshape(bs, 1, d), *p_st, *s_st)
```
